```python
import math
import jax
import jax.numpy as jnp
from jax import lax
import numpy as np

D_MODEL = 2048
BATCH = 4
SEQ = 2048
DEPTH = 2
DEC_BATCH = 32
DEC_SEQ = 4
PAST_LEN = 8192
PAGE_SIZE = 128

HG_HEADS = 24
HG_DK = 128
HG_DV = 128
HG_DIM = HG_HEADS * HG_DK
HG_VDIM = HG_HEADS * HG_DV
HG_CHUNK = 64
LB_FLOOR = 1e-30
M_DINNER = 2 * D_MODEL
M_HEADDIM = 64
M_HEADS = M_DINNER // M_HEADDIM
M_GROUPS = 8
M_DSTATE = 128
M_CONV_DIM = M_DINNER + 2 * M_GROUPS * M_DSTATE
CONV_WIDTH = 4
SSD_CHUNK = 128
A_GROUPS = ((128, 1), (512, 4), (2048, 16))
A_HEADS_PER_GROUP = 8
A_HEAD_DIM = 128
A_HEADS = A_HEADS_PER_GROUP * len(A_GROUPS)
A_DIM = A_HEADS * A_HEAD_DIM
A_OUT_DIM = A_HEADS_PER_GROUP * A_HEAD_DIM
A_QBLOCK = 128
ROPE_THETA = 10000.0
MASK_VALUE = -1e30
FFN_DIM = 5632
N_EXPERTS = 8
TOP_K = 2
MOE_BLOCK = 128
N_DENSE = (DEPTH + 1) // 2
N_MOE = DEPTH // 2
ALPHA = (2 * DEPTH) ** 0.25
BETA = (8 * DEPTH) ** -0.25
EPS = 1e-5

IN_SIZES = (HG_DIM, HG_DIM, HG_VDIM, HG_VDIM,
            M_DINNER, M_CONV_DIM, M_HEADS,
            A_DIM, A_DIM, A_DIM,
            3 * D_MODEL)
IN_SPLITS = tuple(sum(IN_SIZES[:i + 1]) for i in range(len(IN_SIZES) - 1))
N_IN = sum(IN_SIZES)

kernel_name = 'hybrid_hgrn2_ssd_dilated_attn_deepnorm_step'


def layer_norm(x, g, b):
    xf = x.astype(jnp.float32)
    xc = xf - xf.mean(-1, keepdims=True)
    var = (xc * xc).mean(-1, keepdims=True)
    return (xc * lax.rsqrt(var + EPS) * g.astype(jnp.float32) + b.astype(jnp.float32)).astype(x.dtype)


def rms_norm(x, w):
    xf = x.astype(jnp.float32)
    return (xf * lax.rsqrt((xf * xf).mean(-1, keepdims=True) + EPS) * w.astype(jnp.float32)).astype(x.dtype)


def rotary(x, pos):
    half = x.shape[-1] // 2
    inv = ROPE_THETA ** (-jnp.arange(half, dtype=jnp.float32) / half)
    ang = pos.astype(jnp.float32)[:, None] * inv[None, :]
    cos = jnp.cos(ang)[None, :, None, :]
    sin = jnp.sin(ang)[None, :, None, :]
    xf = x.astype(jnp.float32)
    x1, x2 = xf[..., :half], xf[..., half:]
    return jnp.concatenate([x1 * cos - x2 * sin, x2 * cos + x1 * sin], axis=-1).astype(x.dtype)


def to_chunks(a, size):
    B, T = a.shape[:2]
    n = -(-T // size)
    a = jnp.pad(a.astype(jnp.float32), [(0, 0), (0, n * size - T)] + [(0, 0)] * (a.ndim - 2))
    return jnp.swapaxes(a.reshape((B, n, size) + a.shape[2:]), 0, 1)


def from_chunks(a, T):
    n, B, size = a.shape[:3]
    return jnp.swapaxes(a, 0, 1).reshape((B, n * size) + a.shape[3:])[:, :T]


def masked_exp(diff, mask):
    return jnp.where(mask, jnp.exp(jnp.where(mask, diff, 0.0)), 0.0)


def hgrn2_recurrence(q, k, v, log_f, s0):
    T = q.shape[1]
    C = min(HG_CHUNK, T)
    causal = jnp.tril(jnp.ones((C, C), dtype=bool))[None, :, :, None, None]

    def step(S, inp):
        qc, kc, vc, gc = inp
        b = jnp.cumsum(gc, axis=1)
        decay = masked_exp(b[:, :, None] - b[:, None, :], causal)
        scores = jnp.einsum('btshk,bshk->bhts', qc[:, :, None] * decay, kc)
        o = (jnp.einsum('bhts,bshv->bthv', scores, vc)
             + jnp.einsum('bthk,bhkv->bthv', qc * jnp.exp(b), S))
        b_last = b[:, -1]
        S = (jnp.exp(b_last)[..., None] * S
             + jnp.einsum('bshk,bshv->bhkv', kc * jnp.exp(b_last[:, None] - b), vc))
        return S, o

    S, o = lax.scan(step, s0.astype(jnp.float32),
                    (to_chunks(q, C), to_chunks(k, C), to_chunks(v, C), to_chunks(log_f, C)))
    return from_chunks(o, T), S


def ssd_recurrence(x, dt, a, bm, cm, h0):
    B, T, H, P = x.shape
    G, N = bm.shape[2], bm.shape[3]
    J = H // G
    C = min(SSD_CHUNK, T)
    causal = jnp.tril(jnp.ones((C, C), dtype=bool))[None, :, :, None, None]
    a_g = a.astype(jnp.float32).reshape(G, J)

    def step(h, inp):
        xc, dtc, bc, cc = inp
        cum = jnp.cumsum(dtc * a_g, axis=1)
        seg = masked_exp(cum[:, :, None] - cum[:, None, :], causal)
        cb = jnp.einsum('btgn,bsgn->btsg', cc, bc)
        xdt = xc * dtc[..., None]
        y = jnp.einsum('btsgj,bsgjp->btgjp', cb[..., None] * seg, xdt)
        y = y + jnp.einsum('btgn,bgjpn->btgjp', cc, h) * jnp.exp(cum)[..., None]
        last = cum[:, -1]
        h = (jnp.exp(last)[..., None, None] * h
             + jnp.einsum('bsgn,bsgjp->bgjpn', bc, xdt * jnp.exp(last[:, None] - cum)[..., None]))
        return h, y

    h, y = lax.scan(step, h0.astype(jnp.float32).reshape(B, G, J, P, N),
                    (to_chunks(x.reshape(B, T, G, J, P), C), to_chunks(dt.reshape(B, T, G, J), C),
                     to_chunks(bm, C), to_chunks(cm, C)))
    return from_chunks(y, T).reshape(B, T, H, P), h.reshape(B, H, P, N)


def causal_depthwise_conv(u, state, w, b):
    T = u.shape[1]
    full = jnp.concatenate([state.astype(u.dtype), u], axis=1)
    out = b + full[:, 0:T] * w[0]
    for i in range(1, CONV_WIDTH):
        out = out + full[:, i:i + T] * w[i]
    return out, full[:, T:]


def dilated_window_attention(q, k_all, v_all, q_offset, window, dilation):
    B, Tq, H, Dh = q.shape
    qb = math.gcd(Tq, A_QBLOCK)
    n_keys = window // dilation + 1
    taps = dilation * jnp.arange(n_keys)
    scale = Dh ** -0.5

    def block(i):
        qs = lax.dynamic_slice_in_dim(q, i * qb, qb, axis=1)
        rows = q_offset + i * qb + jnp.arange(qb)
        idx = rows[:, None] - taps[None, :]
        valid = (idx >= 0)[None, None]
        idx = jnp.maximum(idx, 0)
        kg = k_all[:, idx]
        vg = v_all[:, idx]
        s = jnp.einsum('bqhd,bqjhd->bhqj', qs, kg).astype(jnp.float32) * scale
        s = jnp.where(valid, s, MASK_VALUE)
        m = s.max(-1, keepdims=True)
        p = jnp.where(valid, jnp.exp(s - m), 0.0)
        den = p.sum(-1, keepdims=True)
        o = jnp.einsum('bhqj,bqjhd->bqhd', (p / den).astype(v_all.dtype), vg)
        lse = jnp.swapaxes((m + jnp.log(den))[..., 0], 1, 2)
        return o, lse

    o, lse = lax.map(block, jnp.arange(Tq // qb))
    o = jnp.moveaxis(o, 0, 1).reshape(B, Tq, H, Dh)
    lse = jnp.moveaxis(lse, 0, 1).reshape(B, Tq, H)
    return o, lse


def swiglu(x, w1, w3, w2):
    return (jax.nn.silu(x @ w1) * (x @ w3)) @ w2


def moe_swiglu(x2, w_router, b_router, w1, w3, w2):
    n_tok, d = x2.shape
    logits = jnp.einsum('nd,de->ne', x2.astype(jnp.float32), w_router.astype(jnp.float32)) + b_router.astype(jnp.float32)
    top_val, top_idx = lax.top_k(logits, TOP_K)
    gates = jax.nn.softmax(top_val, axis=-1)
    n_slots = n_tok * TOP_K
    e_flat = top_idx.reshape(-1)
    tok_flat = jnp.repeat(jnp.arange(n_tok, dtype=jnp.int32), TOP_K)
    g_flat = gates.reshape(-1)
    order = jnp.argsort(e_flat)
    e_sorted = e_flat[order]
    counts = jnp.bincount(e_flat, length=N_EXPERTS)
    padded = (counts + MOE_BLOCK - 1) // MOE_BLOCK * MOE_BLOCK
    pad_end = jnp.cumsum(padded)
    pad_start = pad_end - padded
    cnt_start = jnp.cumsum(counts) - counts
    dest = pad_start[e_sorted] + jnp.arange(n_slots) - cnt_start[e_sorted]
    n_blocks = -(-(n_slots + N_EXPERTS * (MOE_BLOCK - 1)) // MOE_BLOCK)
    n_pad = n_blocks * MOE_BLOCK
    slot_tok = jnp.zeros((n_pad,), jnp.int32).at[dest].set(tok_flat[order])
    slot_gate = jnp.zeros((n_pad,), jnp.float32).at[dest].set(g_flat[order])
    block_expert = jnp.minimum(
        jnp.searchsorted(pad_end, jnp.arange(n_blocks) * MOE_BLOCK, side='right'), N_EXPERTS - 1)
    xs = x2[slot_tok].reshape(n_blocks, MOE_BLOCK, d)

    def expert_block(args):
        xb, e = args
        return swiglu(xb, w1[e], w3[e], w2[e])

    ys = lax.map(expert_block, (xs, block_expert)).reshape(n_pad, d)
    out = jnp.zeros((n_tok, d), jnp.float32).at[slot_tok].add(ys.astype(jnp.float32) * slot_gate[:, None])
    return out.astype(x2.dtype)


def parallel_mixer(x, pos, hg_s0, ssm_h0, conv_s0, kv_bufs, sample,
                   w_in, lb, hg_norm_w, conv_w, conv_b, dt_bias, a_log, d_skip, ssm_norm_w,
                   w_proj_a, w_proj_b, w_proj_c, w_out):
    B, T, _ = x.shape
    h = jnp.einsum('btd,de->bte', x, w_in)
    hq, hf, hi, hg, mz, mxbc, mdt, aq, ak, av, gate_logits = jnp.split(h, IN_SPLITS, axis=-1)

    lbh = lb.reshape(HG_HEADS, HG_DK)
    fx = hf.astype(jnp.float32).reshape(B, T, HG_HEADS, HG_DK)
    log_f = jnp.logaddexp(jnp.log(jnp.maximum(lbh, LB_FLOOR)), jnp.log1p(-lbh) + jax.nn.log_sigmoid(fx))
    k_in = (1.0 - lbh) * jax.nn.sigmoid(-fx)
    q_h = jax.nn.silu(hq).reshape(B, T, HG_HEADS, HG_DK)
    o_a, hg_new = hgrn2_recurrence(q_h, k_in, hi.reshape(B, T, HG_HEADS, HG_DV), log_f, hg_s0)
    o_a = rms_norm(o_a, hg_norm_w.reshape(HG_HEADS, HG_DV)).reshape(B, T, HG_VDIM).astype(x.dtype) * jax.nn.silu(hg)

    xbc, conv_new = causal_depthwise_conv(mxbc, conv_s0, conv_w, conv_b)
    xbc = jax.nn.silu(xbc)
    mx, mb, mc = jnp.split(xbc, (M_DINNER, M_DINNER + M_GROUPS * M_DSTATE), axis=-1)
    dt = jax.nn.softplus(mdt.astype(jnp.float32) + dt_bias.astype(jnp.float32))
    a = -jnp.exp(a_log.astype(jnp.float32))
    mx_h = mx.reshape(B, T, M_HEADS, M_HEADDIM)
    y_b, ssm_new = ssd_recurrence(mx_h, dt, a, mb.reshape(B, T, M_GROUPS, M_DSTATE),
                                  mc.reshape(B, T, M_GROUPS, M_DSTATE), ssm_h0)
    y_b = y_b + d_skip.astype(jnp.float32)[:, None] * mx_h.astype(jnp.float32)
    y_b = y_b.reshape(B, T, M_DINNER) * jax.nn.silu(mz.astype(jnp.float32))
    y_b = rms_norm(y_b.reshape(B, T, M_GROUPS, M_DINNER // M_GROUPS),
                   ssm_norm_w.reshape(M_GROUPS, M_DINNER // M_GROUPS)).reshape(B, T, M_DINNER).astype(x.dtype)

    q_c = rotary(aq.reshape(B, T, A_HEADS, A_HEAD_DIM), pos)
    k_c = rotary(ak.reshape(B, T, A_HEADS, A_HEAD_DIM), pos)
    v_c = av.reshape(B, T, A_HEADS, A_HEAD_DIM)
    outs, lses, new_kv = [], [], []
    for gi, (window, dil) in enumerate(A_GROUPS):
        hs = slice(gi * A_HEADS_PER_GROUP, (gi + 1) * A_HEADS_PER_GROUP)
        k_buf, v_buf = kv_bufs[2 * gi], kv_bufs[2 * gi + 1]
        k_g, v_g = k_c[:, :, hs], v_c[:, :, hs]
        k_all = jnp.concatenate([k_buf.astype(k_g.dtype), k_g], axis=1)
        v_all = jnp.concatenate([v_buf.astype(v_g.dtype), v_g], axis=1)
        o_g, lse_g = dilated_window_attention(q_c[:, :, hs], k_all, v_all, k_buf.shape[1], window, dil)
        outs.append(o_g)
        lses.append(lse_g)
        keep = T if sample else min(window, T)
        new_kv.append(k_g[:, T - keep:])
        new_kv.append(v_g[:, T - keep:])
    w_mix = jax.nn.softmax(jnp.stack(lses), axis=0)
    o_c = jnp.einsum('gbth,gbthd->bthd', w_mix.astype(x.dtype), jnp.stack(outs)).reshape(B, T, A_OUT_DIM)

    g_a, g_b, g_c = jnp.split(jax.nn.sigmoid(gate_logits), 3, axis=-1)
    merged = g_a * (o_a @ w_proj_a) + g_b * (y_b @ w_proj_b) + g_c * (o_c @ w_proj_c)
    return merged @ w_out, hg_new, ssm_new, conv_new, new_kv


def trunk(x, pos, hg_s, ssm_s, conv_s, kv_s, sample,
          w_in, hg_lb, hg_norm_w, conv_w, conv_b, dt_bias, a_log, d_skip, ssm_norm_w,
          w_proj_a, w_proj_b, w_proj_c, w_out, ln1_g, ln1_b, ln2_g, ln2_b,
          ffn_w1, ffn_w3, ffn_w2, moe_router, moe_router_b, moe_w1, moe_w3, moe_w2):
    B, T, D = x.shape
    lb_sm = jax.nn.softmax(hg_lb.astype(jnp.float32), axis=0)
    lb_all = jnp.cumsum(lb_sm, axis=0) - lb_sm[0]
    hg_out, ssm_out, conv_out, kv_out = [], [], [], []
    for l in range(DEPTH):
        mix, hg_n, ssm_n, conv_n, kv_n = parallel_mixer(
            x, pos, hg_s[l], ssm_s[l], conv_s[l], [c[l] for c in kv_s], sample,
            w_in[l], lb_all[l], hg_norm_w[l], conv_w[l], conv_b[l], dt_bias[l], a_log[l], d_skip[l],
            ssm_norm_w[l], w_proj_a[l], w_proj_b[l], w_proj_c[l], w_out[l])
        x = layer_norm(ALPHA * x + mix, ln1_g[l], ln1_b[l])
        if l % 2 == 0:
            f = swiglu(x, ffn_w1[l // 2], ffn_w3[l // 2], ffn_w2[l // 2])
        else:
            f = moe_swiglu(x.reshape(B * T, D), moe_router[l // 2], moe_router_b[l // 2],
                           moe_w1[l // 2], moe_w3[l // 2], moe_w2[l // 2]).reshape(B, T, D)
        x = layer_norm(ALPHA * x + f, ln2_g[l], ln2_b[l])
        hg_out.append(hg_n)
        ssm_out.append(ssm_n)
        conv_out.append(conv_n)
        kv_out.append(kv_n)
    kv_stacked = [jnp.stack([kv_l[i] for kv_l in kv_out]) for i in range(2 * len(A_GROUPS))]
    return x, jnp.stack(hg_out), jnp.stack(ssm_out), jnp.stack(conv_out), kv_stacked


def setup_inputs(seed: int = 0) -> dict:
    key = jax.random.key(seed)
    ks = iter(jax.random.split(key, 40))
    f32 = jnp.float32

    def nrm(shape, scale):
        return jax.random.normal(next(ks), shape, f32) * scale

    wb = [min(w, PAST_LEN) for w, _ in A_GROUPS]
    x_prompt = nrm((BATCH, SEQ, D_MODEL), 1.0)
    x_sample = nrm((DEC_BATCH, DEC_SEQ, D_MODEL), 1.0)
    state_hgrn = nrm((DEPTH, DEC_BATCH, HG_HEADS, HG_DK, HG_DV), 0.5)
    state_ssm = nrm((DEPTH, DEC_BATCH, M_HEADS, M_HEADDIM, M_DSTATE), 0.5)
    state_conv = nrm((DEPTH, DEC_BATCH, CONV_WIDTH - 1, M_CONV_DIM), 1.0)
    kv_shape = lambda rows: (DEPTH, DEC_BATCH, rows, A_HEADS_PER_GROUP, A_HEAD_DIM)
    cache_k0 = nrm(kv_shape(wb[0]), 1.0)
    cache_v0 = nrm(kv_shape(wb[0]), 1.0)
    cache_k1 = nrm(kv_shape(wb[1]), 1.0)
    cache_v1 = nrm(kv_shape(wb[1]), 1.0)
    cache_k2 = nrm(kv_shape(wb[2]), 1.0)
    cache_v2 = nrm(kv_shape(wb[2]), 1.0)
    w_in = nrm((DEPTH, D_MODEL, N_IN), D_MODEL ** -0.5)
    hg_lb = nrm((DEPTH, HG_DIM), 0.5)
    hg_norm_w = 1.0 + nrm((DEPTH, HG_VDIM), 0.05)
    conv_w = nrm((DEPTH, CONV_WIDTH, M_CONV_DIM), CONV_WIDTH ** -0.5)
    conv_b = nrm((DEPTH, M_CONV_DIM), 0.02)
    dt0 = jnp.exp(jax.random.uniform(next(ks), (DEPTH, M_HEADS), f32, math.log(1e-3), math.log(1e-1)))
    dt_bias = dt0 + jnp.log(-jnp.expm1(-dt0))
    a_log = jnp.log(jax.random.uniform(next(ks), (DEPTH, M_HEADS), f32, 1.0, 16.0))
    d_skip = 1.0 + nrm((DEPTH, M_HEADS), 0.1)
    ssm_norm_w = 1.0 + nrm((DEPTH, M_DINNER), 0.05)
    w_proj_a = nrm((DEPTH, HG_VDIM, D_MODEL), HG_VDIM ** -0.5)
    w_proj_b = nrm((DEPTH, M_DINNER, D_MODEL), M_DINNER ** -0.5)
    w_proj_c = nrm((DEPTH, A_OUT_DIM, D_MODEL), A_OUT_DIM ** -0.5)
    w_out = nrm((DEPTH, D_MODEL, D_MODEL), BETA * D_MODEL ** -0.5)
    ln1_g = 1.0 + nrm((DEPTH, D_MODEL), 0.05)
    ln1_b = nrm((DEPTH, D_MODEL), 0.02)
    ln2_g = 1.0 + nrm((DEPTH, D_MODEL), 0.05)
    ln2_b = nrm((DEPTH, D_MODEL), 0.02)
    ffn_w1 = nrm((N_DENSE, D_MODEL, FFN_DIM), D_MODEL ** -0.5)
    ffn_w3 = nrm((N_DENSE, D_MODEL, FFN_DIM), D_MODEL ** -0.5)
    ffn_w2 = nrm((N_DENSE, FFN_DIM, D_MODEL), BETA * FFN_DIM ** -0.5)
    moe_router = nrm((N_MOE, D_MODEL, N_EXPERTS), D_MODEL ** -0.5)
    moe_router_b = nrm((N_MOE, N_EXPERTS), 0.01)
    moe_w1 = nrm((N_MOE, N_EXPERTS, D_MODEL, FFN_DIM), D_MODEL ** -0.5)
    moe_w3 = nrm((N_MOE, N_EXPERTS, D_MODEL, FFN_DIM), D_MODEL ** -0.5)
    moe_w2 = nrm((N_MOE, N_EXPERTS, FFN_DIM, D_MODEL), BETA * FFN_DIM ** -0.5)
    return {'x_prompt': x_prompt, 'x_sample': x_sample,
            'state_hgrn': state_hgrn, 'state_ssm': state_ssm, 'state_conv': state_conv,
            'cache_k0': cache_k0, 'cache_v0': cache_v0, 'cache_k1': cache_k1, 'cache_v1': cache_v1,
            'cache_k2': cache_k2, 'cache_v2': cache_v2,
            'w_in': w_in, 'hg_lb': hg_lb, 'hg_norm_w': hg_norm_w, 'conv_w': conv_w, 'conv_b': conv_b,
            'dt_bias': dt_bias, 'a_log': a_log, 'd_skip': d_skip, 'ssm_norm_w': ssm_norm_w,
            'w_proj_a': w_proj_a, 'w_proj_b': w_proj_b, 'w_proj_c': w_proj_c, 'w_out': w_out,
            'ln1_g': ln1_g, 'ln1_b': ln1_b, 'ln2_g': ln2_g, 'ln2_b': ln2_b,
            'ffn_w1': ffn_w1, 'ffn_w3': ffn_w3, 'ffn_w2': ffn_w2,
            'moe_router': moe_router, 'moe_router_b': moe_router_b,
            'moe_w1': moe_w1, 'moe_w3': moe_w3, 'moe_w2': moe_w2}


def reference(x_prompt, x_sample, state_hgrn, state_ssm, state_conv,
              cache_k0, cache_v0, cache_k1, cache_v1, cache_k2, cache_v2,
              w_in, hg_lb, hg_norm_w, conv_w, conv_b, dt_bias, a_log, d_skip, ssm_norm_w,
              w_proj_a, w_proj_b, w_proj_c, w_out, ln1_g, ln1_b, ln2_g, ln2_b,
              ffn_w1, ffn_w3, ffn_w2, moe_router, moe_router_b, moe_w1, moe_w3, moe_w2):
    weights = (w_in, hg_lb, hg_norm_w, conv_w, conv_b, dt_bias, a_log, d_skip, ssm_norm_w,
               w_proj_a, w_proj_b, w_proj_c, w_out, ln1_g, ln1_b, ln2_g, ln2_b,
               ffn_w1, ffn_w3, ffn_w2, moe_router, moe_router_b, moe_w1, moe_w3, moe_w2)
    bp, tp, _ = x_prompt.shape
    ts = x_sample.shape[1]
    zero_hg = jnp.zeros((DEPTH, bp, HG_HEADS, HG_DK, HG_DV), jnp.float32)
    zero_ssm = jnp.zeros((DEPTH, bp, M_HEADS, M_HEADDIM, M_DSTATE), jnp.float32)
    zero_conv = jnp.zeros((DEPTH, bp, CONV_WIDTH - 1, M_CONV_DIM), x_prompt.dtype)
    empty_kv = [jnp.zeros((DEPTH, bp, 0, A_HEADS_PER_GROUP, A_HEAD_DIM), x_prompt.dtype)
                for _ in range(2 * len(A_GROUPS))]
    y_prompt, p_hgrn, p_ssm, p_conv, p_kv = trunk(
        x_prompt, jnp.arange(tp), zero_hg, zero_ssm, zero_conv, empty_kv, False, *weights)
    y_sample, s_hgrn, s_ssm, s_conv, s_kv = trunk(
        x_sample, PAST_LEN + jnp.arange(ts), state_hgrn, state_ssm, state_conv,
        [cache_k0, cache_v0, cache_k1, cache_v1, cache_k2, cache_v2], True, *weights)
    p_k0, p_v0, p_k1, p_v1, p_k2, p_v2 = p_kv
    s_k0, s_v0, s_k1, s_v1, s_k2, s_v2 = s_kv
    return (y_prompt, y_sample, p_hgrn, p_ssm, p_conv, p_k0, p_v0, p_k1, p_v1, p_k2, p_v2,
            s_hgrn, s_ssm, s_conv, s_k0, s_v0, s_k1, s_v1, s_k2, s_v2)
```

```python
import functools
import math

import numpy as np
import jax
import jax.numpy as jnp
from jax import lax
from jax.experimental import pallas as pl
from jax.experimental.pallas import tpu as pltpu

F32 = jnp.float32
BF16 = jnp.bfloat16

LANE = 128
SROWS = 8
VMEM_CAP = 56 * 1024 * 1024

A_GROUPS = ((128, 1), (512, 4), (2048, 16))
PAST_LEN = 8192
ROPE_THETA = 10000.0
MASK_VALUE = -1e30
LB_FLOOR = 1e-30
EPS = 1e-5
CONV_WIDTH = 4
HG_CHUNK = 128
SSD_CHUNK = 128
MOE_TILE = 256


def _cparams(sem, vmem_bytes=None):
    kw = dict(dimension_semantics=sem)
    if vmem_bytes is not None:
        kw["vmem_limit_bytes"] = int(min(max(vmem_bytes, 16 * 1024 * 1024), VMEM_CAP))
    return pltpu.CompilerParams(**kw)


def _pick(n, cands, *offsets):
    for c in cands:
        if n % c == 0 and all(o % c == 0 for o in offsets):
            return c
    raise ValueError(f"no tile for {n} in {cands} (offsets {offsets})")


def _sigmoid(x):
    return 1.0 / (1.0 + jnp.exp(-x))


def _silu(x):
    return x * _sigmoid(x)


def _softplus(x):
    return jnp.maximum(x, 0.0) + jnp.log1p(jnp.exp(-jnp.abs(x)))


def _dot(a, b):
    return jnp.dot(a, b, preferred_element_type=F32)


def _dot_nt(a, b):
    return lax.dot_general(a, b, (((1,), (1,)), ((), ())), preferred_element_type=F32)


def _dot_tn(a, b):
    return lax.dot_general(a, b, (((0,), (0,)), ((), ())), preferred_element_type=F32)


def _split2(x):
    hi = x.astype(BF16)
    lo = (x - hi.astype(F32)).astype(BF16)
    return hi, lo


def _split3(x):
    p1 = x.astype(BF16)
    r1 = x - p1.astype(F32)
    p2 = r1.astype(BF16)
    p3 = (r1 - p2.astype(F32)).astype(BF16)
    return p1, p2, p3


def _mm_kernel(x_ref, w_ref, o_ref):
    o_ref[...] = _dot(x_ref[...], w_ref[...]).astype(o_ref.dtype)


def _matmul(x, w, out_dtype, name):
    m, k = x.shape
    n = w.shape[1]
    tm = _pick(m, (768, 512, 384, 256, 128, 88, 64, 8))
    tn = _pick(n, (1408, 1152, 1024, 512, 384, 256, 128))
    vm = 2 * (tm * k * 2 + k * tn * 2 + tm * tn * 4) + (4 << 20)
    return pl.pallas_call(
        _mm_kernel,
        out_shape=jax.ShapeDtypeStruct((m, n), out_dtype),
        grid=(m // tm, n // tn),
        in_specs=[pl.BlockSpec((tm, k), lambda i, j: (i, 0)),
                  pl.BlockSpec((k, tn), lambda i, j: (0, j))],
        out_specs=pl.BlockSpec((tm, tn), lambda i, j: (i, j)),
        compiler_params=_cparams(("parallel", "parallel"), vm),
        name=name,
    )(x, w)


def _conv_kernel(u_ref, prev_ref, w_ref, b_ref, o_ref, *, rows, blocks_per_seq, zero_first):
    u = u_ref[...]
    prev = prev_ref[...]
    if zero_first:
        first = (pl.program_id(0) % blocks_per_seq) == 0
        prev = jnp.where(first, 0.0, prev)
    full = jnp.concatenate([prev, u], axis=0)
    acc = b_ref[...] + full[SROWS:SROWS + rows] * w_ref[CONV_WIDTH - 1:CONV_WIDTH, :]
    for i in range(CONV_WIDTH - 1):
        off = SROWS - (CONV_WIDTH - 1) + i
        acc = acc + full[off:off + rows] * w_ref[i:i + 1, :]
    o_ref[...] = _silu(acc)


def _conv_silu(h, off_xbc, conv_dim, w, b, row0, nrows, rows, blocks_per_seq, prev_arr, name):
    tc = _pick(conv_dim, (1536, 1024, 512, 256, 128), off_xbc)
    cb0 = off_xbc // tc
    rb0 = row0 // rows
    nb = nrows // rows
    if prev_arr is None:
        rpb = rows // SROWS
        prev_spec = pl.BlockSpec((SROWS, tc), lambda i, j: (jnp.maximum((rb0 + i) * rpb - 1, 0), cb0 + j))
        prev_in = h
    else:
        prev_spec = pl.BlockSpec((SROWS, tc), lambda i, j: (i, j))
        prev_in = prev_arr
    return pl.pallas_call(
        functools.partial(_conv_kernel, rows=rows, blocks_per_seq=blocks_per_seq, zero_first=prev_arr is None),
        out_shape=jax.ShapeDtypeStruct((nrows, conv_dim), F32),
        grid=(nb, conv_dim // tc),
        in_specs=[pl.BlockSpec((rows, tc), lambda i, j: (rb0 + i, cb0 + j)),
                  prev_spec,
                  pl.BlockSpec((CONV_WIDTH, tc), lambda i, j: (0, j)),
                  pl.BlockSpec((1, tc), lambda i, j: (0, j))],
        out_specs=pl.BlockSpec((rows, tc), lambda i, j: (i, j)),
        compiler_params=_cparams(("parallel", "parallel")),
        name=name,
    )(h, prev_in, w, b)


def _hgrn_consts(c):
    t = np.arange(c)[:, None]
    u = np.arange(c)[None, :]
    mats, masks = [], [np.eye(c)]
    h = 1
    while h < c:
        tb, ub = t // h, u // h
        mats.append((tb % 2 == 1) & (ub == tb) & (u <= t))
        mats.append((tb % 2 == 0) & (ub == tb) & (u > t))
        masks.append((tb % 2 == 1) & (ub == tb - 1))
        h *= 2
    mats.append(u <= t)
    mats.append(u > t)
    mall = np.concatenate([m.astype(np.float32) for m in mats], axis=0)
    return jnp.asarray(mall, BF16), jnp.asarray(np.stack(masks).astype(np.float32))


def _hgrn_gates(fx, hq, lbc):
    la, l1, oml = lbc[0:1], lbc[1:2], lbc[2:3]
    ls = jnp.minimum(fx, 0.0) - jnp.log1p(jnp.exp(-jnp.abs(fx)))
    ct = l1 + ls
    g = jnp.maximum(la, ct) + jnp.log1p(jnp.exp(-jnp.abs(la - ct)))
    kk = oml * (1.0 / (1.0 + jnp.exp(fx)))
    return g, kk, _silu(hq)


def _hgrn_out(o, nw, hg):
    ms = jnp.mean(o * o, axis=-1, keepdims=True)
    return (o * lax.rsqrt(ms + EPS) * nw) * _silu(hg)


def _hgrn_prompt_kernel(hq_ref, hf_ref, hi_ref, hg_ref, lbc_ref, nw_ref, mall_ref, mask_ref,
                        o_ref, s_ref, st_scr, *, hb, c, nlev):
    ci = pl.program_id(2)

    @pl.when(ci == 0)
    def _():
        st_scr[...] = jnp.zeros_like(st_scr)

    for hh in range(hb):
        sl = slice(hh * LANE, (hh + 1) * LANE)
        g, kk, q = _hgrn_gates(hf_ref[:, sl], hq_ref[:, sl], lbc_ref[:, sl])
        v = hi_ref[:, sl]
        vb = v.astype(BF16)
        g_hi, g_lo = _split2(g)
        e2 = _dot(mall_ref[...], jnp.concatenate([g_hi, g_lo], axis=1))
        ex = e2[:, :LANE] + e2[:, LANE:]
        a = _dot_nt(q.astype(BF16), kk.astype(BF16)) * mask_ref[0]
        for lv in range(nlev):
            eq = ex[(2 * lv) * c:(2 * lv + 1) * c]
            ek = ex[(2 * lv + 1) * c:(2 * lv + 2) * c]
            ql = (q * jnp.exp(eq)).astype(BF16)
            kl = (kk * jnp.exp(ek)).astype(BF16)
            a = a + _dot_nt(ql, kl) * mask_ref[lv + 1]
        bcum = ex[(2 * nlev) * c:(2 * nlev + 1) * c]
        erev = ex[(2 * nlev + 1) * c:(2 * nlev + 2) * c]
        st = st_scr[hh]
        o = _dot(a.astype(BF16), vb) + _dot_nt((q * jnp.exp(bcum)).astype(BF16), st.astype(BF16))
        st_new = st * jnp.exp(bcum[c - 1:c, :]) + _dot_tn(vb, (kk * jnp.exp(erev)).astype(BF16))
        st_scr[hh] = st_new
        o_ref[:, sl] = _hgrn_out(o, nw_ref[:, sl], hg_ref[:, sl]).astype(o_ref.dtype)

        @pl.when(ci == pl.num_programs(2) - 1)
        def _():
            s_ref[0, hh] = st_new.T


def _hgrn_prompt(h, offs, lbc, nw, b, t, heads, name):
    c = HG_CHUNK
    hb = _pick(heads * LANE, (4 * LANE, 2 * LANE, LANE), *(offs[k] for k in ("hq", "hf", "hi", "hg"))) // LANE
    w = hb * LANE
    nlev = int(math.log2(c))
    mall, masks = _hgrn_consts(c)
    nc = t // c

    def hspec(off):
        return pl.BlockSpec((c, w), lambda bi, hi, ci: (bi * nc + ci, off // w + hi))

    return pl.pallas_call(
        functools.partial(_hgrn_prompt_kernel, hb=hb, c=c, nlev=nlev),
        out_shape=(jax.ShapeDtypeStruct((b * t, heads * LANE), BF16),
                   jax.ShapeDtypeStruct((b, heads, LANE, LANE), F32)),
        grid=(b, heads // hb, nc),
        in_specs=[hspec(offs["hq"]), hspec(offs["hf"]), hspec(offs["hi"]), hspec(offs["hg"]),
                  pl.BlockSpec((3, w), lambda bi, hi, ci: (0, hi)),
                  pl.BlockSpec((1, w), lambda bi, hi, ci: (0, hi)),
                  pl.BlockSpec(mall.shape, lambda bi, hi, ci: (0, 0)),
                  pl.BlockSpec(masks.shape, lambda bi, hi, ci: (0, 0, 0))],
        out_specs=(pl.BlockSpec((c, w), lambda bi, hi, ci: (bi * nc + ci, hi)),
                   pl.BlockSpec((1, hb, LANE, LANE), lambda bi, hi, ci: (bi, hi, 0, 0))),
        scratch_shapes=[pltpu.VMEM((hb, LANE, LANE), F32)],
        compiler_params=_cparams(("parallel", "parallel", "arbitrary"), 40 << 20),
        name=name,
    )(h, h, h, h, lbc, nw, mall, masks)


def _row_select(rows_list, n):
    width = rows_list[0].shape[1]
    ridx = lax.broadcasted_iota(jnp.int32, (n, width), 0)
    out = jnp.zeros((n, width), F32)
    for i, r in enumerate(rows_list):
        out = jnp.where(ridx == i, r, out)
    return out


def _hgrn_sample_kernel(hq_ref, hf_ref, hi_ref, hg_ref, lbc_ref, nw_ref, s0_ref, o_ref, s_ref, *, hb, ts):
    ridx = lax.broadcasted_iota(jnp.int32, (SROWS, LANE), 0)
    real = ridx < ts
    for hh in range(hb):
        sl = slice(hh * LANE, (hh + 1) * LANE)
        g, kk, q = _hgrn_gates(hf_ref[:, sl], hq_ref[:, sl], lbc_ref[:, sl])
        v = hi_ref[:, sl]
        g = jnp.where(real, g, 0.0)
        kk = jnp.where(real, kk, 0.0)
        brow = []
        for i in range(ts):
            brow.append(g[i:i + 1] if i == 0 else brow[-1] + g[i:i + 1])
        s0 = s0_ref[0, hh]
        bt = _row_select(brow, SROWS)
        o_inter = _dot(q * jnp.exp(bt), s0)
        orow = []
        for i in range(ts):
            acc = jnp.zeros((1, LANE), F32)
            for s in range(i + 1):
                wgt = jnp.sum(q[i:i + 1] * kk[s:s + 1] * jnp.exp(brow[i] - brow[s]), axis=-1, keepdims=True)
                acc = acc + wgt * v[s:s + 1]
            orow.append(acc)
        o = o_inter + _row_select(orow, SROWS)
        o_ref[:, sl] = _hgrn_out(o, nw_ref[:, sl], hg_ref[:, sl]).astype(o_ref.dtype)
        cols = [jnp.exp(brow[-1])] + [kk[s:s + 1] * jnp.exp(brow[-1] - brow[s]) for s in range(ts)]
        xt = jnp.concatenate([_row_select(cols, SROWS), jnp.zeros((LANE - SROWS, LANE), F32)], axis=0).T
        s_new = xt[:, 0:1] * s0
        for s in range(ts):
            s_new = s_new + xt[:, 1 + s:2 + s] * v[s:s + 1]
        s_ref[0, hh] = s_new


def _hgrn_sample(h, offs, lbc, nw, s0, row0, bs, ts, heads, name):
    hb = _pick(heads * LANE, (4 * LANE, 2 * LANE, LANE), *(offs[k] for k in ("hq", "hf", "hi", "hg"))) // LANE
    w = hb * LANE
    rb0 = row0 // SROWS

    def hspec(off):
        return pl.BlockSpec((SROWS, w), lambda bi, hi: (rb0 + bi, off // w + hi))

    sspec = pl.BlockSpec((1, hb, LANE, LANE), lambda bi, hi: (bi, hi, 0, 0))
    return pl.pallas_call(
        functools.partial(_hgrn_sample_kernel, hb=hb, ts=ts),
        out_shape=(jax.ShapeDtypeStruct((bs * SROWS, heads * LANE), BF16),
                   jax.ShapeDtypeStruct((bs, heads, LANE, LANE), F32)),
        grid=(bs, heads // hb),
        in_specs=[hspec(offs["hq"]), hspec(offs["hf"]), hspec(offs["hi"]), hspec(offs["hg"]),
                  pl.BlockSpec((3, w), lambda bi, hi: (0, hi)),
                  pl.BlockSpec((1, w), lambda bi, hi: (0, hi)),
                  sspec],
        out_specs=(pl.BlockSpec((SROWS, w), lambda bi, hi: (bi, hi)), sspec),
        compiler_params=_cparams(("parallel", "parallel")),
        name=name,
    )(h, h, h, h, lbc, nw, s0)


def _ssd_finish(ys, xs, zs, dsk, nw_ref, o_ref, width):
    gated = [(y + d * x) * _silu(z) for y, x, z, d in zip(ys, xs, zs, dsk)]
    ssq = sum(jnp.sum(t * t, axis=-1, keepdims=True) for t in gated)
    r = lax.rsqrt(ssq / width + EPS)
    for p, t in enumerate(gated):
        sl = slice(p * LANE, (p + 1) * LANE)
        o_ref[:, sl] = (t * r * nw_ref[:, sl]).astype(o_ref.dtype)


def _ssd_prompt_kernel(x_ref, b_ref, c_ref, z_ref, dt_ref, dtt_ref, pr_ref, pc_ref, nw_ref, tri_ref, triu_ref,
                       o_ref, s_ref, ht_scr, *, c, hpg, pdim):
    ci = pl.program_id(2)
    npair = hpg * pdim // LANE
    hpp = LANE // pdim

    @pl.when(ci == 0)
    def _():
        ht_scr[...] = jnp.zeros_like(ht_scr)

    pr = pr_ref[0]
    pc = pc_ref[0]
    dt_c = _softplus(dt_ref[0] + pr[0:1])
    da_c = dt_c * pr[1:2]
    da_r = _softplus(dtt_ref[0] + pc[:, 0:1]) * pc[:, 1:2]
    tri = tri_ref[...]
    triu = triu_ref[...]
    cum_c = sum(_dot(tri, p.astype(F32)) for p in _split3(da_c))
    cum_r = sum(_dot(p.astype(F32), triu) for p in _split3(da_r))
    bm = b_ref[...].astype(BF16)
    cm = c_ref[...].astype(BF16)
    cb = _dot_nt(cm, bm)
    trow = lax.broadcasted_iota(jnp.int32, (c, c), 0)
    scol = lax.broadcasted_iota(jnp.int32, (c, c), 1)
    causal = trow >= scol
    lane = lax.broadcasted_iota(jnp.int32, (1, LANE), 1)

    def per_head(vals):
        out = vals[-1]
        for k in range(hpp - 2, -1, -1):
            out = jnp.where(lane < (k + 1) * pdim, vals[k], out)
        return out

    ys, xs, zs, dsk = [], [], [], []
    for p in range(npair):
        sl = slice(p * LANE, (p + 1) * LANE)
        js = [p * hpp + k for k in range(hpp)]
        xp = x_ref[:, sl]
        xdt = xp * per_head([dt_c[:, j:j + 1] for j in js])
        xdtb = xdt.astype(BF16)
        yj = []
        for j in js:
            seg = jnp.exp(jnp.where(causal, cum_c[:, j:j + 1] - cum_r[j:j + 1, :], MASK_VALUE))
            yj.append(_dot((cb * seg).astype(BF16), xdtb))
        y = per_head(yj)
        htp = ht_scr[p]
        y = y + _dot(cm, htp.astype(BF16)) * per_head([jnp.exp(cum_c[:, j:j + 1]) for j in js])
        last = [cum_c[c - 1:c, j:j + 1] for j in js]
        wgt = per_head([jnp.exp(l - cum_c[:, j:j + 1]) for l, j in zip(last, js)])
        ht_new = htp * per_head([jnp.exp(l) for l in last]) + _dot_tn(bm, (xdt * wgt).astype(BF16))
        ht_scr[p] = ht_new
        ys.append(y)
        xs.append(xp)
        zs.append(z_ref[:, sl])
        dsk.append(per_head([pr[2:3, j:j + 1] for j in js]))

        @pl.when(ci == pl.num_programs(2) - 1)
        def _():
            s_ref[0, sl, :] = ht_new.T

    _ssd_finish(ys, xs, zs, dsk, nw_ref, o_ref, hpg * pdim)


def _ssd_prompt(xa, h, off_z, mdt_g, mdt_gt, prm_r, prm_c, nw, b, t, groups, hpg, pdim, dstate, name):
    c = SSD_CHUNK
    gw = hpg * pdim
    dinner = groups * gw
    nc = t // c
    npair = gw // LANE
    tt = np.arange(c)
    tri = jnp.asarray((tt[None, :] <= tt[:, None]).astype(np.float32))
    rowmap = lambda bi, gi, ci: (bi * nc + ci, gi)
    return pl.pallas_call(
        functools.partial(_ssd_prompt_kernel, c=c, hpg=hpg, pdim=pdim),
        out_shape=(jax.ShapeDtypeStruct((b * t, dinner), BF16),
                   jax.ShapeDtypeStruct((b, groups * gw, dstate), F32)),
        grid=(b, groups, nc),
        in_specs=[pl.BlockSpec((c, gw), rowmap),
                  pl.BlockSpec((c, dstate), lambda bi, gi, ci: (bi * nc + ci, dinner // dstate + gi)),
                  pl.BlockSpec((c, dstate), lambda bi, gi, ci: (bi * nc + ci, dinner // dstate + groups + gi)),
                  pl.BlockSpec((c, gw), lambda bi, gi, ci: (bi * nc + ci, off_z // gw + gi)),
                  pl.BlockSpec((1, c, hpg), lambda bi, gi, ci: (gi, bi * nc + ci, 0)),
                  pl.BlockSpec((1, hpg, c), lambda bi, gi, ci: (gi, 0, bi * nc + ci)),
                  pl.BlockSpec((1, 3, hpg), lambda bi, gi, ci: (gi, 0, 0)),
                  pl.BlockSpec((1, hpg, 3), lambda bi, gi, ci: (gi, 0, 0)),
                  pl.BlockSpec((1, gw), lambda bi, gi, ci: (0, gi)),
                  pl.BlockSpec((c, c), lambda bi, gi, ci: (0, 0)),
                  pl.BlockSpec((c, c), lambda bi, gi, ci: (0, 0))],
        out_specs=(pl.BlockSpec((c, gw), rowmap),
                   pl.BlockSpec((1, gw, dstate), lambda bi, gi, ci: (bi, gi, 0))),
        scratch_shapes=[pltpu.VMEM((npair, dstate, LANE), F32)],
        compiler_params=_cparams(("parallel", "parallel", "arbitrary"), 40 << 20),
        name=name,
    )(xa, xa, xa, h, mdt_g, mdt_gt, prm_r, prm_c, nw, tri, tri.T)


def _ssd_sample_kernel(x_ref, b_ref, c_ref, z_ref, dtx_ref, px_ref, nw_ref, s0_ref, o_ref, s_ref, *, ts, gw):
    npair = gw // LANE
    dstate = b_ref.shape[1]
    ridx = lax.broadcasted_iota(jnp.int32, (SROWS, gw), 0)
    dt = jnp.where(ridx < ts, _softplus(dtx_ref[...] + px_ref[0:1, :]), 0.0)
    dec = jnp.exp(dt * px_ref[1:2, :])
    x = x_ref[...]
    xdt = x * dt
    pad = jnp.zeros((LANE - SROWS, dstate), F32)
    bt = jnp.concatenate([b_ref[...], pad], axis=0).T
    ct = jnp.concatenate([c_ref[...], pad], axis=0).T
    ys, xs, zs, dsk = [], [], [], []
    for p in range(npair):
        sl = slice(p * LANE, (p + 1) * LANE)
        ht = s0_ref[0, sl, :].T
        yrow = []
        for i in range(ts):
            ht = ht * dec[i:i + 1, sl] + bt[:, i:i + 1] * xdt[i:i + 1, sl]
            yrow.append(jnp.sum(ht * ct[:, i:i + 1], axis=0, keepdims=True))
        s_ref[0, sl, :] = ht.T
        ys.append(_row_select(yrow, SROWS))
        xs.append(x[:, sl])
        zs.append(z_ref[:, sl])
        dsk.append(px_ref[2:3, sl])
    _ssd_finish(ys, xs, zs, dsk, nw_ref, o_ref, gw)


def _ssd_sample(xa_s, h, off_z, mdt_x, prm_x, nw, s0, row0, bs, ts, groups, gw, dstate, name):
    dinner = groups * gw
    rb0 = row0 // SROWS
    sspec = pl.BlockSpec((1, gw, dstate), lambda bi, gi: (bi, gi, 0))
    return pl.pallas_call(
        functools.partial(_ssd_sample_kernel, ts=ts, gw=gw),
        out_shape=(jax.ShapeDtypeStruct((bs * SROWS, dinner), BF16),
                   jax.ShapeDtypeStruct((bs, groups * gw, dstate), F32)),
        grid=(bs, groups),
        in_specs=[pl.BlockSpec((SROWS, gw), lambda bi, gi: (bi, gi)),
                  pl.BlockSpec((SROWS, dstate), lambda bi, gi: (bi, dinner // dstate + gi)),
                  pl.BlockSpec((SROWS, dstate), lambda bi, gi: (bi, dinner // dstate + groups + gi)),
                  pl.BlockSpec((SROWS, gw), lambda bi, gi: (rb0 + bi, off_z // gw + gi)),
                  pl.BlockSpec((SROWS, gw), lambda bi, gi: (bi, gi)),
                  pl.BlockSpec((3, gw), lambda bi, gi: (0, gi)),
                  pl.BlockSpec((1, gw), lambda bi, gi: (0, gi)),
                  sspec],
        out_specs=(pl.BlockSpec((SROWS, gw), lambda bi, gi: (bi, gi)), sspec),
        compiler_params=_cparams(("parallel", "parallel")),
        name=name,
    )(xa_s, xa_s, xa_s, h, mdt_x, prm_x, nw, s0)


def _rope_kernel(q_ref, k_ref, v_ref, cos_ref, sin_ref, qo_ref, ko_ref, vo_ref, *, nh):
    cos = cos_ref[...]
    sin = sin_ref[...]
    for hh in range(nh):
        sl = slice(hh * LANE, (hh + 1) * LANE)
        q = q_ref[:, sl]
        k = k_ref[:, sl]
        qo_ref[:, sl] = (q * cos + pltpu.roll(q, LANE // 2, 1) * sin).astype(qo_ref.dtype)
        ko_ref[:, sl] = k * cos + pltpu.roll(k, LANE // 2, 1) * sin
    vo_ref[...] = v_ref[...]


def _rope(h, offs, adim, cos2, sin2, name):
    n = h.shape[0]
    tr = _pick(n, (384, 256, 128, 88, 64, 8))
    tc = _pick(adim, (1024, 512, 256, 128), offs["aq"], offs["ak"], offs["av"])

    def hspec(off):
        return pl.BlockSpec((tr, tc), lambda i, j: (i, off // tc + j))

    ospec = pl.BlockSpec((tr, tc), lambda i, j: (i, j))
    tspec = pl.BlockSpec((tr, LANE), lambda i, j: (i, 0))
    return pl.pallas_call(
        functools.partial(_rope_kernel, nh=tc // LANE),
        out_shape=(jax.ShapeDtypeStruct((n, adim), BF16),
                   jax.ShapeDtypeStruct((n, adim), F32),
                   jax.ShapeDtypeStruct((n, adim), F32)),
        grid=(n // tr, adim // tc),
        in_specs=[hspec(offs["aq"]), hspec(offs["ak"]), hspec(offs["av"]), tspec, tspec],
        out_specs=(ospec, ospec, ospec),
        compiler_params=_cparams(("parallel", "parallel")),
        name=name,
    )(h, h, h, cos2, sin2)


def _attn_prompt_kernel(q_ref, kp_ref, kc_ref, vp_ref, vc_ref, o_ref, l_ref, *, qb, nh, scale):
    has_prev = pl.program_id(2) > 0
    row = lax.broadcasted_iota(jnp.int32, (qb, qb), 0)
    col = lax.broadcasted_iota(jnp.int32, (qb, qb), 1)
    vprev = jnp.logical_and(col >= row, has_prev)
    vcur = col <= row
    for hh in range(nh):
        sl = slice(hh * LANE, (hh + 1) * LANE)
        q = q_ref[:, sl]
        sp = jnp.where(vprev, _dot_nt(q, kp_ref[:, sl].astype(BF16)) * scale, MASK_VALUE)
        sc = jnp.where(vcur, _dot_nt(q, kc_ref[:, sl].astype(BF16)) * scale, MASK_VALUE)
        m = jnp.maximum(jnp.max(sp, axis=-1, keepdims=True), jnp.max(sc, axis=-1, keepdims=True))
        pp = jnp.where(vprev, jnp.exp(sp - m), 0.0)
        pc = jnp.where(vcur, jnp.exp(sc - m), 0.0)
        den = jnp.sum(pp, axis=-1, keepdims=True) + jnp.sum(pc, axis=-1, keepdims=True)
        inv = 1.0 / den
        o = (_dot((pp * inv).astype(BF16), vp_ref[:, sl].astype(BF16))
             + _dot((pc * inv).astype(BF16), vc_ref[:, sl].astype(BF16)))
        o_ref[:, sl] = o
        l_ref[:, sl] = jnp.broadcast_to(m + jnp.log(den), (qb, LANE))


def _attn_prompt(q_rot, k_rot, v_c, gi, window, dil, b, t, hpg, name):
    n, adim = q_rot.shape
    ow = hpg * LANE
    qb = window // dil
    l = t // dil
    nq = l // qb
    ncb = adim // ow
    qv = q_rot.reshape(n // dil, dil * adim)
    kv = k_rot.reshape(n // dil, dil * adim)
    vv = v_c.reshape(n // dil, dil * adim)
    cur = lambda bi, r, i: (bi * nq + i, r * ncb + gi)
    prev = lambda bi, r, i: (bi * nq + jnp.maximum(i - 1, 0), r * ncb + gi)
    ospec = pl.BlockSpec((qb, ow), lambda bi, r, i: (bi * nq + i, r))
    o, lse = pl.pallas_call(
        functools.partial(_attn_prompt_kernel, qb=qb, nh=hpg, scale=LANE ** -0.5),
        out_shape=(jax.ShapeDtypeStruct((b * l, dil * ow), F32),
                   jax.ShapeDtypeStruct((b * l, dil * ow), F32)),
        grid=(b, dil, nq),
        in_specs=[pl.BlockSpec((qb, ow), cur), pl.BlockSpec((qb, ow), prev), pl.BlockSpec((qb, ow), cur),
                  pl.BlockSpec((qb, ow), prev), pl.BlockSpec((qb, ow), cur)],
        out_specs=(ospec, ospec),
        compiler_params=_cparams(("parallel", "parallel", "arbitrary")),
        name=name,
    )(qv, kv, kv, vv, vv)
    return o.reshape(b * t, ow), lse.reshape(b * t, ow)


def _attn_sample_kernel(q_ref, kn_ref, vn_ref, kc_ref, vc_ref, o_ref, l_ref, *, ts, nh, window, dil, scale):
    nk = window // dil
    o_ref[...] = jnp.zeros_like(o_ref)
    l_ref[...] = jnp.zeros_like(l_ref)
    arow = lax.broadcasted_iota(jnp.int32, (nk, 1), 0)
    nrow = lax.broadcasted_iota(jnp.int32, (SROWS, 1), 0)
    for i in range(ts):
        rho = (window + i) % dil
        base = (window + i - rho) // dil
        j0 = i // dil + 1
        a_lo, a_hi = max(base - nk, 0), min(base - j0, nk - 1)
        vcache = jnp.logical_and(arow >= a_lo, arow <= a_hi)
        new_rows = [i - j * dil for j in range(i // dil + 1)]
        vnew = functools.reduce(jnp.logical_or, [nrow == r for r in new_rows])
        for hh in range(nh):
            sl = slice(hh * LANE, (hh + 1) * LANE)
            csl = slice((rho * nh + hh) * LANE, (rho * nh + hh + 1) * LANE)
            q = q_ref[i:i + 1, sl].astype(F32)
            sc = jnp.where(vcache, jnp.sum(kc_ref[0, :, csl] * q, axis=-1, keepdims=True) * scale, MASK_VALUE)
            sn = jnp.where(vnew, jnp.sum(kn_ref[:, sl] * q, axis=-1, keepdims=True) * scale, MASK_VALUE)
            m = jnp.maximum(jnp.max(sc, axis=0, keepdims=True), jnp.max(sn, axis=0, keepdims=True))
            pc = jnp.where(vcache, jnp.exp(sc - m), 0.0)
            pn = jnp.where(vnew, jnp.exp(sn - m), 0.0)
            den = jnp.sum(pc, axis=0, keepdims=True) + jnp.sum(pn, axis=0, keepdims=True)
            acc = (jnp.sum(pc * vc_ref[0, :, csl], axis=0, keepdims=True)
                   + jnp.sum(pn * vn_ref[:, sl], axis=0, keepdims=True))
            o_ref[i:i + 1, sl] = acc / den
            l_ref[i:i + 1, sl] = jnp.broadcast_to(m + jnp.log(den), (1, LANE))


def _attn_sample(q_rot, k_rot, v_c, ck, cv, gi, window, dil, row0, bs, ts, hpg, name):
    ow = hpg * LANE
    nk = window // dil
    assert ck.shape[1] == window and window % dil == 0 and (dil == 1 or dil >= ts)
    nres = min(dil, ts)
    ckv = ck.reshape(bs, nk, dil * ow)
    cvv = cv.reshape(bs, nk, dil * ow)
    rb0 = row0 // SROWS
    nspec = pl.BlockSpec((SROWS, ow), lambda bi: (rb0 + bi, gi))
    cspec = pl.BlockSpec((1, nk, nres * ow), lambda bi: (bi, 0, 0))
    ospec = pl.BlockSpec((SROWS, ow), lambda bi: (bi, 0))
    return pl.pallas_call(
        functools.partial(_attn_sample_kernel, ts=ts, nh=hpg, window=window, dil=dil, scale=LANE ** -0.5),
        out_shape=(jax.ShapeDtypeStruct((bs * SROWS, ow), F32),
                   jax.ShapeDtypeStruct((bs * SROWS, ow), F32)),
        grid=(bs,),
        in_specs=[nspec, nspec, nspec, cspec, cspec],
        out_specs=(ospec, ospec),
        compiler_params=_cparams(("parallel",), 40 << 20),
        name=name,
    )(q_rot, k_rot, v_c, ckv, cvv)


def _combine_kernel(*refs):
    ng = (len(refs) - 1) // 2
    os_, ls_, out = refs[:ng], refs[ng:2 * ng], refs[-1]
    ls = [l[...] for l in ls_]
    m = functools.reduce(jnp.maximum, ls)
    ws = [jnp.exp(l - m) for l in ls]
    den = functools.reduce(lambda a, b: a + b, ws)
    acc = functools.reduce(lambda a, b: a + b, [w * o[...] for w, o in zip(ws, os_)])
    out[...] = (acc / den).astype(out.dtype)


def _combine(outs, lses, name):
    n, ow = outs[0].shape
    tr = _pick(n, (512, 256, 128, 64, 8))
    spec = pl.BlockSpec((tr, ow), lambda i: (i, 0))
    return pl.pallas_call(
        _combine_kernel,
        out_shape=jax.ShapeDtypeStruct((n, ow), BF16),
        grid=(n // tr,),
        in_specs=[spec] * (2 * len(outs)),
        out_specs=spec,
        compiler_params=_cparams(("parallel",)),
        name=name,
    )(*outs, *lses)


def _merge_kernel(oa_ref, yb_ref, oc_ref, wa_ref, wb_ref, wc_ref, ga_ref, gb_ref, gc_ref, o_ref):
    acc = _sigmoid(ga_ref[...]) * _dot(oa_ref[...], wa_ref[...])
    acc = acc + _sigmoid(gb_ref[...]) * _dot(yb_ref[...], wb_ref[...])
    acc = acc + _sigmoid(gc_ref[...]) * _dot(oc_ref[...], wc_ref[...])
    o_ref[...] = acc.astype(o_ref.dtype)


def _merge(oa, yb, oc, wa, wb, wc, h, off_g, d, name):
    n = oa.shape[0]
    tm = _pick(n, (384, 256, 128, 88, 64, 8))
    tn = _pick(d, (512, 256, 128), off_g)
    ka, kb, kc = oa.shape[1], yb.shape[1], oc.shape[1]

    def gspec(k):
        return pl.BlockSpec((tm, tn), lambda j, i: (i, (off_g + k * d) // tn + j))

    return pl.pallas_call(
        _merge_kernel,
        out_shape=jax.ShapeDtypeStruct((n, d), BF16),
        grid=(d // tn, n // tm),
        in_specs=[pl.BlockSpec((tm, ka), lambda j, i: (i, 0)),
                  pl.BlockSpec((tm, kb), lambda j, i: (i, 0)),
                  pl.BlockSpec((tm, kc), lambda j, i: (i, 0)),
                  pl.BlockSpec((ka, tn), lambda j, i: (0, j)),
                  pl.BlockSpec((kb, tn), lambda j, i: (0, j)),
                  pl.BlockSpec((kc, tn), lambda j, i: (0, j)),
                  gspec(0), gspec(1), gspec(2)],
        out_specs=pl.BlockSpec((tm, tn), lambda j, i: (i, j)),
        compiler_params=_cparams(("parallel", "parallel"), 48 << 20),
        name=name,
    )(oa, yb, oc, wa, wb, wc, h, h, h)


def _layer_norm(y, g, b):
    mu = jnp.mean(y, axis=-1, keepdims=True)
    yc = y - mu
    var = jnp.mean(yc * yc, axis=-1, keepdims=True)
    return yc * lax.rsqrt(var + EPS) * g + b


def _outln_kernel(m_ref, w_ref, x_ref, g_ref, b_ref, o_ref, ob_ref, *, alpha):
    y = alpha * x_ref[...] + _dot(m_ref[...], w_ref[...])
    out = _layer_norm(y, g_ref[...], b_ref[...])
    o_ref[...] = out
    ob_ref[...] = out.astype(ob_ref.dtype)


def _outln(merged, w_out, x, g, b, alpha, name):
    n, d = x.shape
    tm = _pick(n, (384, 256, 128, 88, 64, 8))
    rspec = pl.BlockSpec((tm, d), lambda i: (i, 0))
    vspec = pl.BlockSpec((1, d), lambda i: (0, 0))
    return pl.pallas_call(
        functools.partial(_outln_kernel, alpha=alpha),
        out_shape=(jax.ShapeDtypeStruct((n, d), F32), jax.ShapeDtypeStruct((n, d), BF16)),
        grid=(n // tm,),
        in_specs=[rspec, pl.BlockSpec((d, d), lambda i: (0, 0)), rspec, vspec, vspec],
        out_specs=(rspec, rspec),
        compiler_params=_cparams(("parallel",), 48 << 20),
        name=name,
    )(merged, w_out, x, g, b)


def _resln_kernel(x_ref, f_ref, g_ref, b_ref, o_ref, ob_ref, *, alpha):
    out = _layer_norm(alpha * x_ref[...] + f_ref[...], g_ref[...], b_ref[...])
    o_ref[...] = out
    ob_ref[...] = out.astype(ob_ref.dtype)


def _resln(x, f, g, b, alpha, name):
    n, d = x.shape
    tm = _pick(n, (384, 256, 128, 88, 64, 8))
    rspec = pl.BlockSpec((tm, d), lambda i: (i, 0))
    vspec = pl.BlockSpec((1, d), lambda i: (0, 0))
    return pl.pallas_call(
        functools.partial(_resln_kernel, alpha=alpha),
        out_shape=(jax.ShapeDtypeStruct((n, d), F32), jax.ShapeDtypeStruct((n, d), BF16)),
        grid=(n // tm,),
        in_specs=[rspec, rspec, vspec, vspec],
        out_specs=(rspec, rspec),
        compiler_params=_cparams(("parallel",)),
        name=name,
    )(x, f, g, b)


def _ffn_a_kernel(te_ref, x_ref, w1_ref, w3_ref, o_ref):
    x = x_ref[...].astype(BF16)
    a = _dot(x, w1_ref[0])
    o_ref[...] = (_silu(a) * _dot(x, w3_ref[0])).astype(o_ref.dtype)


def _ffn_a(x, w1, w3, tile_expert, tm, name):
    n, d = x.shape
    f = w1.shape[2]
    tf = _pick(f, (1408, 1024, 512, 256, 128))
    xbytes = x.dtype.itemsize
    vm = 2 * (tm * d * xbytes + 2 * d * tf * 2 + tm * tf * 2) + 3 * tm * tf * 4 + (6 << 20)
    return pl.pallas_call(
        _ffn_a_kernel,
        out_shape=jax.ShapeDtypeStruct((n, f), BF16),
        grid_spec=pltpu.PrefetchScalarGridSpec(
            num_scalar_prefetch=1,
            grid=(f // tf, n // tm),
            in_specs=[pl.BlockSpec((tm, d), lambda j, i, te: (i, 0)),
                      pl.BlockSpec((1, d, tf), lambda j, i, te: (te[i], 0, j)),
                      pl.BlockSpec((1, d, tf), lambda j, i, te: (te[i], 0, j))],
            out_specs=pl.BlockSpec((tm, tf), lambda j, i, te: (i, j))),
        compiler_params=_cparams(("parallel", "arbitrary"), vm),
        name=name,
    )(tile_expert, x, w1, w3)


def _ffn_b_kernel(te_ref, h_ref, w2_ref, o_ref):
    o_ref[...] = _dot(h_ref[...], w2_ref[0])


def _ffn_b(hmid, w2, tile_expert, tm, name):
    n, f = hmid.shape
    d = w2.shape[2]
    tn = _pick(d, (512, 256, 128))
    vm = 2 * (tm * f * 2 + f * tn * 2 + tm * tn * 4) + (4 << 20)
    return pl.pallas_call(
        _ffn_b_kernel,
        out_shape=jax.ShapeDtypeStruct((n, d), F32),
        grid_spec=pltpu.PrefetchScalarGridSpec(
            num_scalar_prefetch=1,
            grid=(d // tn, n // tm),
            in_specs=[pl.BlockSpec((tm, f), lambda j, i, te: (i, 0)),
                      pl.BlockSpec((1, f, tn), lambda j, i, te: (te[i], 0, j))],
            out_specs=pl.BlockSpec((tm, tn), lambda j, i, te: (i, j))),
        compiler_params=_cparams(("parallel", "arbitrary"), vm),
        name=name,
    )(tile_expert, hmid, w2)


def _router_kernel(x_ref, w_ref, b_ref, e_ref, g_ref, *, n_exp):
    xs = _split3(x_ref[...])
    ws = _split3(w_ref[...])
    logits = b_ref[...]
    for i in range(3):
        for j in range(3 - i):
            logits = logits + _dot(xs[i], ws[j])
    lane = lax.broadcasted_iota(jnp.int32, logits.shape, 1)
    neg = jnp.float32(-jnp.inf)
    logits = jnp.where(lane < n_exp, logits, neg)
    m0 = jnp.max(logits, axis=-1, keepdims=True)
    i0 = jnp.min(jnp.where(logits == m0, lane, LANE), axis=-1, keepdims=True)
    rest = jnp.where(lane == i0, neg, logits)
    m1 = jnp.max(rest, axis=-1, keepdims=True)
    i1 = jnp.min(jnp.where(rest == m1, lane, LANE), axis=-1, keepdims=True)
    e1 = jnp.exp(m1 - m0)
    den = 1.0 + e1
    e_ref[...] = jnp.where(lane == 0, i0, jnp.where(lane == 1, i1, 0))
    g_ref[...] = jnp.where(lane == 0, 1.0 / den, jnp.where(lane == 1, e1 / den, 0.0))


def _router(x, w_router, b_router, name):
    n, d = x.shape
    n_exp = w_router.shape[1]
    wp = jnp.zeros((d, LANE), F32).at[:, :n_exp].set(w_router.astype(F32))
    bp = jnp.zeros((1, LANE), F32).at[0, :n_exp].set(b_router.astype(F32))
    tm = _pick(n, (384, 256, 128, 88, 64, 8))
    rspec = pl.BlockSpec((tm, LANE), lambda i: (i, 0))
    return pl.pallas_call(
        functools.partial(_router_kernel, n_exp=n_exp),
        out_shape=(jax.ShapeDtypeStruct((n, LANE), jnp.int32), jax.ShapeDtypeStruct((n, LANE), F32)),
        grid=(n // tm,),
        in_specs=[pl.BlockSpec((tm, d), lambda i: (i, 0)),
                  pl.BlockSpec((d, LANE), lambda i: (0, 0)),
                  pl.BlockSpec((1, LANE), lambda i: (0, 0))],
        out_specs=(rspec, rspec),
        compiler_params=_cparams(("parallel",)),
        name=name,
    )(x, wp, bp)


def _row_copy(src_hbm, row, dst_ref, dst_row, sem):
    return pltpu.make_async_copy(src_hbm.at[pl.ds(row, 1)], dst_ref.at[pl.ds(dst_row, 1)], sem)


def _gather_kernel(idx_ref, x_hbm, o_ref, sem, *, tm):
    base = pl.program_id(0) * tm

    def start(r, carry):
        _row_copy(x_hbm, idx_ref[base + r], o_ref, r, sem).start()
        return carry

    def wait(r, carry):
        _row_copy(x_hbm, 0, o_ref, r, sem).wait()
        return carry

    lax.fori_loop(0, tm, start, 0)
    lax.fori_loop(0, tm, wait, 0)


def _gather_rows(x, idx, tm, name):
    n_out = idx.shape[0]
    d = x.shape[1]
    return pl.pallas_call(
        functools.partial(_gather_kernel, tm=tm),
        out_shape=jax.ShapeDtypeStruct((n_out, d), x.dtype),
        grid_spec=pltpu.PrefetchScalarGridSpec(
            num_scalar_prefetch=1,
            grid=(n_out // tm,),
            in_specs=[pl.BlockSpec(memory_space=pl.ANY)],
            out_specs=pl.BlockSpec((tm, d), lambda i, idx_ref: (i, 0)),
            scratch_shapes=[pltpu.SemaphoreType.DMA(())]),
        compiler_params=_cparams(("arbitrary",)),
        name=name,
    )(idx, x)


def _moe_out_kernel(pos_ref, y_hbm, x_ref, gate_ref, g_ref, b_ref, o_ref, ob_ref, buf, sem, *, tm, alpha):
    base = pl.program_id(0) * tm

    def start(r, carry):
        for k in range(2):
            _row_copy(y_hbm, pos_ref[2 * (base + r) + k], buf.at[k], r, sem).start()
        return carry

    def wait(r, carry):
        for k in range(2):
            _row_copy(y_hbm, 0, buf.at[k], r, sem).wait()
        return carry

    lax.fori_loop(0, tm, start, 0)
    lax.fori_loop(0, tm, wait, 0)
    gt = gate_ref[...]
    f = buf[0] * gt[:, 0:1] + buf[1] * gt[:, 1:2]
    out = _layer_norm(alpha * x_ref[...] + f, g_ref[...], b_ref[...])
    o_ref[...] = out
    ob_ref[...] = out.astype(ob_ref.dtype)


def _moe_out(ys, pos, x, gates, g, b, alpha, name):
    n, d = x.shape
    tm = _pick(n, (256, 128, 88, 64, 8))
    rspec = pl.BlockSpec((tm, d), lambda i, p: (i, 0))
    vspec = pl.BlockSpec((1, d), lambda i, p: (0, 0))
    return pl.pallas_call(
        functools.partial(_moe_out_kernel, tm=tm, alpha=alpha),
        out_shape=(jax.ShapeDtypeStruct((n, d), F32), jax.ShapeDtypeStruct((n, d), BF16)),
        grid_spec=pltpu.PrefetchScalarGridSpec(
            num_scalar_prefetch=1,
            grid=(n // tm,),
            in_specs=[pl.BlockSpec(memory_space=pl.ANY), rspec,
                      pl.BlockSpec((tm, LANE), lambda i, p: (i, 0)), vspec, vspec],
            out_specs=(rspec, rspec),
            scratch_shapes=[pltpu.VMEM((2, tm, d), F32), pltpu.SemaphoreType.DMA(())]),
        compiler_params=_cparams(("arbitrary",)),
        name=name,
    )(pos, ys, x, gates, g, b)


def _moe_ffn(x, w_router, b_router, w1, w3, w2, g, b, alpha, tag):
    n, d = x.shape
    n_exp = w1.shape[0]
    tm = MOE_TILE
    e_pad, gates = _router(x, w_router, b_router, f"router_{tag}")
    e_flat = e_pad[:, :2].reshape(-1)
    n_slots = 2 * n
    onehot = (e_flat[:, None] == jnp.arange(n_exp, dtype=jnp.int32)[None, :]).astype(jnp.int32)
    csum = jnp.cumsum(onehot, axis=0)
    rank = jnp.take_along_axis(csum, e_flat[:, None], axis=1)[:, 0] - 1
    counts = csum[-1]
    padded = (counts + tm - 1) // tm * tm
    pad_end = jnp.cumsum(padded)
    pos = (pad_end - padded)[e_flat] + rank
    n_tiles = -(-(n_slots + n_exp * (tm - 1)) // tm)
    slot_tok = jnp.zeros((n_tiles * tm,), jnp.int32).at[pos].set(jnp.arange(n_slots, dtype=jnp.int32) // 2)
    tile_expert = jnp.minimum(
        jnp.searchsorted(pad_end, jnp.arange(n_tiles, dtype=jnp.int32) * tm, side="right"), n_exp - 1
    ).astype(jnp.int32)
    xs = _gather_rows(x, slot_tok, tm, f"moe_gather_{tag}")
    hmid = _ffn_a(xs, w1, w3, tile_expert, tm, f"moe_a_{tag}")
    ys = _ffn_b(hmid, w2, tile_expert, tm, f"moe_b_{tag}")
    return _moe_out(ys, pos.astype(jnp.int32), x, gates, g, b, alpha, f"moe_out_{tag}")


def kernel(x_prompt, x_sample, state_hgrn, state_ssm, state_conv, cache_k0, cache_v0, cache_k1, cache_v1,
           cache_k2, cache_v2, w_in, hg_lb, hg_norm_w, conv_w, conv_b, dt_bias, a_log, d_skip, ssm_norm_w,
           w_proj_a, w_proj_b, w_proj_c, w_out, ln1_g, ln1_b, ln2_g, ln2_b, ffn_w1, ffn_w3, ffn_w2,
           moe_router, moe_router_b, moe_w1, moe_w3, moe_w2):
    caches = ((cache_k0, cache_v0), (cache_k1, cache_v1), (cache_k2, cache_v2))
    depth = w_in.shape[0]
    b, t, d = x_prompt.shape
    bs, ts, _ = x_sample.shape
    _, _, hg_heads, hg_dk, hg_dv = state_hgrn.shape
    _, _, m_heads, pdim, dstate = state_ssm.shape
    conv_dim = state_conv.shape[-1]
    hpg, adh = cache_k0.shape[-2:]
    assert hg_dk == LANE and hg_dv == LANE and dstate == LANE and adh == LANE and ts <= SROWS
    hg_dim = hg_heads * LANE
    dinner = m_heads * pdim
    groups = (conv_dim - dinner) // (2 * dstate)
    m_hpg = m_heads // groups
    gw = m_hpg * pdim
    adim = len(A_GROUPS) * hpg * LANE
    assert gw % LANE == 0 and LANE % pdim == 0 and t % HG_CHUNK == 0 and t % SSD_CHUNK == 0
    ow = hpg * LANE
    alpha = (2 * depth) ** 0.25
    bt = b * t
    n = bt + bs * SROWS

    sizes = (("hq", hg_dim), ("hf", hg_dim), ("hi", hg_dim), ("hg", hg_dim), ("mz", dinner), ("xbc", conv_dim),
             ("mdt", m_heads), ("aq", adim), ("ak", adim), ("av", adim), ("gate", 3 * d))
    src, acc = {}, 0
    for name, sz in sizes:
        src[name] = acc
        acc += sz
    assert acc == w_in.shape[2]
    order = [k for k, _ in sizes if k != "mdt"] + ["mdt"]
    offs, acc = {}, 0
    for name in order:
        offs[name] = acc
        acc += dict(sizes)[name]
    mdt_pad = (-acc) % LANE
    assert offs["mz"] % gw == 0 and all(offs[k] % ow == 0 for k in ("aq", "ak", "av"))

    def pack_w_in(w):
        parts = [w[:, src[k]:src[k] + dict(sizes)[k]] for k in order]
        parts.append(jnp.zeros((d, mdt_pad), w.dtype))
        return jnp.concatenate(parts, axis=1).astype(BF16)

    xs_pad = jnp.zeros((bs, SROWS, d), F32).at[:, :ts].set(x_sample)
    x = jnp.concatenate([x_prompt.reshape(bt, d), xs_pad.reshape(bs * SROWS, d)], axis=0)
    xb = x.astype(BF16)

    half = LANE // 2
    inv = ROPE_THETA ** (-jnp.arange(half, dtype=F32) / half)
    srow = jnp.arange(SROWS)
    pos = jnp.concatenate([jnp.tile(jnp.arange(t), b),
                           jnp.tile(jnp.where(srow < ts, PAST_LEN + srow, 0), bs)]).astype(F32)
    ang = pos[:, None] * inv[None, :]
    cos2 = jnp.concatenate([jnp.cos(ang), jnp.cos(ang)], axis=1)
    sin2 = jnp.concatenate([-jnp.sin(ang), jnp.sin(ang)], axis=1)

    lb_sm = jax.nn.softmax(hg_lb.astype(F32), axis=0)
    lb_all = jnp.cumsum(lb_sm, axis=0) - lb_sm[0]

    out_hg_p, out_ssm_p, out_conv_p, out_hg_s, out_ssm_s, out_conv_s = [], [], [], [], [], []
    out_kv_p = [[] for _ in range(2 * len(A_GROUPS))]
    out_kv_s = [[] for _ in range(2 * len(A_GROUPS))]

    for l in range(depth):
        h = _matmul(xb, pack_w_in(w_in[l]), F32, f"in_proj_{l}")

        lb = lb_all[l]
        lbc = jnp.stack([jnp.log(jnp.maximum(lb, LB_FLOOR)), jnp.log1p(-lb), 1.0 - lb])
        nw_a = hg_norm_w[l][None, :]
        oa_p, hg_p = _hgrn_prompt(h, offs, lbc, nw_a, b, t, hg_heads, f"hgrn_prompt_{l}")
        oa_s, hg_s = _hgrn_sample(h, offs, lbc, nw_a, state_hgrn[l], bt, bs, ts, hg_heads, f"hgrn_sample_{l}")
        o_a = jnp.concatenate([oa_p, oa_s], axis=0)

        cw, cbias = conv_w[l], conv_b[l][None, :]
        conv_rows = _pick(t, (256, 128, 64, 32, 16, 8))
        xa_p = _conv_silu(h, offs["xbc"], conv_dim, cw, cbias, 0, bt, conv_rows, t // conv_rows, None,
                          f"conv_prompt_{l}")
        prev_s = jnp.zeros((bs, SROWS, conv_dim), F32).at[:, SROWS - (CONV_WIDTH - 1):].set(state_conv[l])
        xa_s = _conv_silu(h, offs["xbc"], conv_dim, cw, cbias, bt, bs * SROWS, SROWS, 1,
                          prev_s.reshape(bs * SROWS, conv_dim), f"conv_sample_{l}")
        mdt = h[:, offs["mdt"]:offs["mdt"] + m_heads]
        a_neg = -jnp.exp(a_log[l].astype(F32))
        prm = jnp.stack([dt_bias[l].astype(F32), a_neg, d_skip[l].astype(F32)])
        prm_r = prm.reshape(3, groups, m_hpg).transpose(1, 0, 2)
        mdt_g = mdt[:bt].reshape(bt, groups, m_hpg).transpose(1, 0, 2)
        nw_b = ssm_norm_w[l][None, :]
        yb_p, ssm_p = _ssd_prompt(xa_p, h, offs["mz"], mdt_g, mdt_g.transpose(0, 2, 1), prm_r,
                                  prm_r.transpose(0, 2, 1), nw_b, b, t, groups, m_hpg, pdim, dstate,
                                  f"ssd_prompt_{l}")
        yb_s, ssm_s = _ssd_sample(xa_s, h, offs["mz"], jnp.repeat(mdt[bt:], pdim, axis=1),
                                  jnp.repeat(prm, pdim, axis=1), nw_b,
                                  state_ssm[l].reshape(bs, m_heads * pdim, dstate), bt, bs, ts, groups, gw, dstate,
                                  f"ssd_sample_{l}")
        y_b = jnp.concatenate([yb_p, yb_s], axis=0)

        q_rot, k_rot, v_c = _rope(h, offs, adim, cos2, sin2, f"rope_{l}")
        outs_p, lses_p, outs_s, lses_s = [], [], [], []
        for gi, (window, dil) in enumerate(A_GROUPS):
            o_g, l_g = _attn_prompt(q_rot, k_rot, v_c, gi, window, dil, b, t, hpg, f"attn_prompt_{l}_{gi}")
            outs_p.append(o_g)
            lses_p.append(l_g)
            o_g, l_g = _attn_sample(q_rot, k_rot, v_c, caches[gi][0][l], caches[gi][1][l], gi, window, dil,
                                    bt, bs, ts, hpg, f"attn_sample_{l}_{gi}")
            outs_s.append(o_g)
            lses_s.append(l_g)
        o_c = jnp.concatenate([_combine(outs_p, lses_p, f"attn_mix_prompt_{l}"),
                               _combine(outs_s, lses_s, f"attn_mix_sample_{l}")], axis=0)

        merged = _merge(o_a, y_b, o_c, w_proj_a[l].astype(BF16), w_proj_b[l].astype(BF16),
                        w_proj_c[l].astype(BF16), h, offs["gate"], d, f"merge_{l}")
        x, xb = _outln(merged, w_out[l].astype(BF16), x, ln1_g[l][None, :], ln1_b[l][None, :], alpha, f"out_ln1_{l}")

        if l % 2 == 0:
            i = l // 2
            tm = _pick(n, (768, 512, 384, 256, 128, 88, 64, 8))
            te = jnp.zeros((n // tm,), jnp.int32)
            hmid = _ffn_a(xb, ffn_w1[i:i + 1].astype(BF16), ffn_w3[i:i + 1].astype(BF16), te, tm, f"ffn_a_{l}")
            f = _ffn_b(hmid, ffn_w2[i:i + 1].astype(BF16), te, tm, f"ffn_b_{l}")
            x, xb = _resln(x, f, ln2_g[l][None, :], ln2_b[l][None, :], alpha, f"ln2_{l}")
        else:
            i = l // 2
            x, xb = _moe_ffn(x, moe_router[i], moe_router_b[i], moe_w1[i].astype(BF16), moe_w3[i].astype(BF16),
                             moe_w2[i].astype(BF16), ln2_g[l][None, :], ln2_b[l][None, :], alpha, str(l))

        keep_c = CONV_WIDTH - 1
        xbc_p = h[:bt, offs["xbc"]:offs["xbc"] + conv_dim].reshape(b, t, conv_dim)
        out_conv_p.append(jnp.concatenate([jnp.zeros((b, keep_c, conv_dim), F32), xbc_p], axis=1)[:, -keep_c:])
        xbc_s = h[bt:, offs["xbc"]:offs["xbc"] + conv_dim].reshape(bs, SROWS, conv_dim)[:, :ts]
        out_conv_s.append(jnp.concatenate([state_conv[l], xbc_s], axis=1)[:, -keep_c:])
        out_hg_p.append(hg_p)
        out_hg_s.append(hg_s)
        out_ssm_p.append(ssm_p.reshape(b, m_heads, pdim, dstate))
        out_ssm_s.append(ssm_s.reshape(bs, m_heads, pdim, dstate))
        kp = k_rot[:bt].reshape(b, t, adim // LANE, LANE)
        vp = v_c[:bt].reshape(b, t, adim // LANE, LANE)
        ksm = k_rot[bt:].reshape(bs, SROWS, adim // LANE, LANE)[:, :ts]
        vsm = v_c[bt:].reshape(bs, SROWS, adim // LANE, LANE)[:, :ts]
        for gi, (window, _) in enumerate(A_GROUPS):
            hs = slice(gi * hpg, (gi + 1) * hpg)
            keep = min(window, t)
            out_kv_p[2 * gi].append(kp[:, t - keep:, hs])
            out_kv_p[2 * gi + 1].append(vp[:, t - keep:, hs])
            out_kv_s[2 * gi].append(ksm[:, :, hs])
            out_kv_s[2 * gi + 1].append(vsm[:, :, hs])

    y_prompt = x[:bt].reshape(b, t, d)
    y_sample = x[bt:].reshape(bs, SROWS, d)[:, :ts]
    return (y_prompt, y_sample, jnp.stack(out_hg_p), jnp.stack(out_ssm_p), jnp.stack(out_conv_p),
            *[jnp.stack(o) for o in out_kv_p],
            jnp.stack(out_hg_s), jnp.stack(out_ssm_s), jnp.stack(out_conv_s),
            *[jnp.stack(o) for o in out_kv_s])
```

```python
import functools
import math

import numpy as np
import jax
import jax.numpy as jnp
from jax import lax
from jax.experimental import pallas as pl
from jax.experimental.pallas import tpu as pltpu

F32 = jnp.float32
BF16 = jnp.bfloat16

LANE = 128
SROWS = 8
VMEM_CAP = 56 * 1024 * 1024

A_GROUPS = ((128, 1), (512, 4), (2048, 16))
PAST_LEN = 8192
ROPE_THETA = 10000.0
MASK_VALUE = -1e30
LB_FLOOR = 1e-30
EPS = 1e-5
CONV_WIDTH = 4
HG_CHUNK = 128
SSD_CHUNK = 128
MOE_TILE = 256


def _cparams(sem, vmem_bytes=None):
    kw = dict(dimension_semantics=sem)
    if vmem_bytes is not None:
        kw["vmem_limit_bytes"] = int(min(max(vmem_bytes, 16 * 1024 * 1024), VMEM_CAP))
    return pltpu.CompilerParams(**kw)


def _pick(n, cands, *offsets):
    for c in cands:
        if n % c == 0 and all(o % c == 0 for o in offsets):
            return c
    raise ValueError(f"no tile for {n} in {cands} (offsets {offsets})")


def _sigmoid(x):
    return 1.0 / (1.0 + jnp.exp(-x))


def _silu(x):
    return x * _sigmoid(x)


def _softplus(x):
    return jnp.maximum(x, 0.0) + jnp.log1p(jnp.exp(-jnp.abs(x)))


def _dot(a, b):
    return jnp.dot(a, b, preferred_element_type=F32)


def _dot_nt(a, b):
    return lax.dot_general(a, b, (((1,), (1,)), ((), ())), preferred_element_type=F32)


def _dot_tn(a, b):
    return lax.dot_general(a, b, (((0,), (0,)), ((), ())), preferred_element_type=F32)


def _split2(x):
    hi = x.astype(BF16)
    lo = (x - hi.astype(F32)).astype(BF16)
    return hi, lo


def _split3(x):
    p1 = x.astype(BF16)
    r1 = x - p1.astype(F32)
    p2 = r1.astype(BF16)
    p3 = (r1 - p2.astype(F32)).astype(BF16)
    return p1, p2, p3


def _mm_kernel(x_ref, w_ref, o_ref):
    o_ref[...] = _dot(x_ref[...], w_ref[...]).astype(o_ref.dtype)


def _matmul(x, w, out_dtype, name):
    m, k = x.shape
    n = w.shape[1]
    tm = _pick(m, (768, 512, 384, 256, 128, 88, 64, 8))
    tn = _pick(n, (1408, 1152, 1024, 512, 384, 256, 128))
    vm = 2 * (tm * k * 2 + k * tn * 2 + tm * tn * 4) + (4 << 20)
    return pl.pallas_call(
        _mm_kernel,
        out_shape=jax.ShapeDtypeStruct((m, n), out_dtype),
        grid=(m // tm, n // tn),
        in_specs=[pl.BlockSpec((tm, k), lambda i, j: (i, 0)),
                  pl.BlockSpec((k, tn), lambda i, j: (0, j))],
        out_specs=pl.BlockSpec((tm, tn), lambda i, j: (i, j)),
        compiler_params=_cparams(("parallel", "parallel"), vm),
        name=name,
    )(x, w)


def _conv_kernel(u_ref, prev_ref, w_ref, b_ref, o_ref, *, rows, blocks_per_seq, zero_first):
    u = u_ref[...]
    prev = prev_ref[...]
    if zero_first:
        first = (pl.program_id(0) % blocks_per_seq) == 0
        prev = jnp.where(first, 0.0, prev)
    full = jnp.concatenate([prev, u], axis=0)
    acc = b_ref[...] + full[SROWS:SROWS + rows] * w_ref[CONV_WIDTH - 1:CONV_WIDTH, :]
    for i in range(CONV_WIDTH - 1):
        off = SROWS - (CONV_WIDTH - 1) + i
        acc = acc + full[off:off + rows] * w_ref[i:i + 1, :]
    o_ref[...] = _silu(acc)


def _conv_silu(h, off_xbc, conv_dim, w, b, row0, nrows, rows, blocks_per_seq, prev_arr, name):
    tc = _pick(conv_dim, (1536, 1024, 512, 256, 128), off_xbc)
    cb0 = off_xbc // tc
    rb0 = row0 // rows
    nb = nrows // rows
    if prev_arr is None:
        rpb = rows // SROWS
        prev_spec = pl.BlockSpec((SROWS, tc), lambda i, j: (jnp.maximum((rb0 + i) * rpb - 1, 0), cb0 + j))
        prev_in = h
    else:
        prev_spec = pl.BlockSpec((SROWS, tc), lambda i, j: (i, j))
        prev_in = prev_arr
    return pl.pallas_call(
        functools.partial(_conv_kernel, rows=rows, blocks_per_seq=blocks_per_seq, zero_first=prev_arr is None),
        out_shape=jax.ShapeDtypeStruct((nrows, conv_dim), F32),
        grid=(nb, conv_dim // tc),
        in_specs=[pl.BlockSpec((rows, tc), lambda i, j: (rb0 + i, cb0 + j)),
                  prev_spec,
                  pl.BlockSpec((CONV_WIDTH, tc), lambda i, j: (0, j)),
                  pl.BlockSpec((1, tc), lambda i, j: (0, j))],
        out_specs=pl.BlockSpec((rows, tc), lambda i, j: (i, j)),
        compiler_params=_cparams(("parallel", "parallel")),
        name=name,
    )(h, prev_in, w, b)


def _hgrn_consts(c):
    t = np.arange(c)[:, None]
    u = np.arange(c)[None, :]
    mats, masks = [], [np.eye(c)]
    h = 1
    while h < c:
        tb, ub = t // h, u // h
        mats.append(((tb % 2 == 1) & (ub == tb) & (u <= t)) | ((tb % 2 == 0) & (ub == tb) & (u > t)))
        masks.append((tb % 2 == 1) & (ub == tb - 1))
        h *= 2
    mats.append(u <= t)
    mall = np.concatenate([m.astype(np.float32) for m in mats], axis=0)
    return jnp.asarray(mall, BF16), jnp.asarray(np.stack(masks).astype(np.float32))


def _hgrn_gates(fx, hq, lbc):
    la, l1, oml = lbc[0:1], lbc[1:2], lbc[2:3]
    ls = jnp.minimum(fx, 0.0) - jnp.log1p(jnp.exp(-jnp.abs(fx)))
    ct = l1 + ls
    g = jnp.maximum(la, ct) + jnp.log1p(jnp.exp(-jnp.abs(la - ct)))
    kk = oml * (1.0 / (1.0 + jnp.exp(fx)))
    return g, kk, _silu(hq)


def _hgrn_out(o, nw, hg):
    ms = jnp.mean(o * o, axis=-1, keepdims=True)
    return (o * lax.rsqrt(ms + EPS) * nw) * _silu(hg)


def _hgrn_prompt_kernel(hq_ref, hf_ref, hi_ref, hg_ref, lbc_ref, nw_ref, mall_ref, mask_ref,
                        o_ref, s_ref, st_scr, *, hb, c, nlev):
    ci = pl.program_id(2)

    @pl.when(ci == 0)
    def _():
        st_scr[...] = jnp.zeros_like(st_scr)

    for hh in range(hb):
        sl = slice(hh * LANE, (hh + 1) * LANE)
        g, kk, q = _hgrn_gates(hf_ref[:, sl], hq_ref[:, sl], lbc_ref[:, sl])
        v = hi_ref[:, sl]
        vb = v.astype(BF16)
        g_hi, g_lo = _split2(g)
        e2 = _dot(mall_ref[...], jnp.concatenate([g_hi, g_lo], axis=1))
        ex = e2[:, :LANE] + e2[:, LANE:]
        a = _dot_nt(q.astype(BF16), kk.astype(BF16)) * mask_ref[0]
        for lv in range(nlev):
            w = jnp.exp(ex[lv * c:(lv + 1) * c])
            a = a + _dot_nt((q * w).astype(BF16), (kk * w).astype(BF16)) * mask_ref[lv + 1]
        bcum = ex[nlev * c:(nlev + 1) * c]
        erev = bcum[c - 1:c, :] - bcum
        st = st_scr[hh]
        o = _dot(a.astype(BF16), vb) + _dot_nt((q * jnp.exp(bcum)).astype(BF16), st.astype(BF16))
        st_new = st * jnp.exp(bcum[c - 1:c, :]) + _dot_tn(vb, (kk * jnp.exp(erev)).astype(BF16))
        st_scr[hh] = st_new
        o_ref[:, sl] = _hgrn_out(o, nw_ref[:, sl], hg_ref[:, sl]).astype(o_ref.dtype)

        @pl.when(ci == pl.num_programs(2) - 1)
        def _():
            s_ref[0, hh] = st_new.T


def _hgrn_prompt(h, offs, lbc, nw, b, t, heads, name):
    c = HG_CHUNK
    hb = _pick(heads * LANE, (4 * LANE, 2 * LANE, LANE), *(offs[k] for k in ("hq", "hf", "hi", "hg"))) // LANE
    w = hb * LANE
    nlev = int(math.log2(c))
    mall, masks = _hgrn_consts(c)
    nc = t // c

    def hspec(off):
        return pl.BlockSpec((c, w), lambda bi, hi, ci: (bi * nc + ci, off // w + hi))

    return pl.pallas_call(
        functools.partial(_hgrn_prompt_kernel, hb=hb, c=c, nlev=nlev),
        out_shape=(jax.ShapeDtypeStruct((b * t, heads * LANE), BF16),
                   jax.ShapeDtypeStruct((b, heads, LANE, LANE), F32)),
        grid=(b, heads // hb, nc),
        in_specs=[hspec(offs["hq"]), hspec(offs["hf"]), hspec(offs["hi"]), hspec(offs["hg"]),
                  pl.BlockSpec((3, w), lambda bi, hi, ci: (0, hi)),
                  pl.BlockSpec((1, w), lambda bi, hi, ci: (0, hi)),
                  pl.BlockSpec(mall.shape, lambda bi, hi, ci: (0, 0)),
                  pl.BlockSpec(masks.shape, lambda bi, hi, ci: (0, 0, 0))],
        out_specs=(pl.BlockSpec((c, w), lambda bi, hi, ci: (bi * nc + ci, hi)),
                   pl.BlockSpec((1, hb, LANE, LANE), lambda bi, hi, ci: (bi, hi, 0, 0))),
        scratch_shapes=[pltpu.VMEM((hb, LANE, LANE), F32)],
        compiler_params=_cparams(("parallel", "parallel", "arbitrary"), 40 << 20),
        name=name,
    )(h, h, h, h, lbc, nw, mall, masks)


def _row_select(rows_list, n):
    width = rows_list[0].shape[1]
    ridx = lax.broadcasted_iota(jnp.int32, (n, width), 0)
    out = jnp.zeros((n, width), F32)
    for i, r in enumerate(rows_list):
        out = jnp.where(ridx == i, r, out)
    return out


def _hgrn_sample_kernel(hq_ref, hf_ref, hi_ref, hg_ref, lbc_ref, nw_ref, s0_ref, o_ref, s_ref, *, hb, ts):
    ridx = lax.broadcasted_iota(jnp.int32, (SROWS, LANE), 0)
    real = ridx < ts
    for hh in range(hb):
        sl = slice(hh * LANE, (hh + 1) * LANE)
        g, kk, q = _hgrn_gates(hf_ref[:, sl], hq_ref[:, sl], lbc_ref[:, sl])
        v = hi_ref[:, sl]
        g = jnp.where(real, g, 0.0)
        kk = jnp.where(real, kk, 0.0)
        brow = []
        for i in range(ts):
            brow.append(g[i:i + 1] if i == 0 else brow[-1] + g[i:i + 1])
        s0 = s0_ref[0, hh]
        bt = _row_select(brow, SROWS)
        o_inter = _dot(q * jnp.exp(bt), s0)
        orow = []
        for i in range(ts):
            acc = jnp.zeros((1, LANE), F32)
            for s in range(i + 1):
                wgt = jnp.sum(q[i:i + 1] * kk[s:s + 1] * jnp.exp(brow[i] - brow[s]), axis=-1, keepdims=True)
                acc = acc + wgt * v[s:s + 1]
            orow.append(acc)
        o = o_inter + _row_select(orow, SROWS)
        o_ref[:, sl] = _hgrn_out(o, nw_ref[:, sl], hg_ref[:, sl]).astype(o_ref.dtype)
        cols = [jnp.exp(brow[-1])] + [kk[s:s + 1] * jnp.exp(brow[-1] - brow[s]) for s in range(ts)]
        xt = jnp.concatenate([_row_select(cols, SROWS), jnp.zeros((LANE - SROWS, LANE), F32)], axis=0).T
        s_new = xt[:, 0:1] * s0
        for s in range(ts):
            s_new = s_new + xt[:, 1 + s:2 + s] * v[s:s + 1]
        s_ref[0, hh] = s_new


def _hgrn_sample(h, offs, lbc, nw, s0_all, layer, row0, bs, ts, heads, name):
    hb = _pick(heads * LANE, (4 * LANE, 2 * LANE, LANE), *(offs[k] for k in ("hq", "hf", "hi", "hg"))) // LANE
    w = hb * LANE
    rb0 = row0 // SROWS

    def hspec(off):
        return pl.BlockSpec((SROWS, w), lambda bi, hi: (rb0 + bi, off // w + hi))

    sspec = pl.BlockSpec((1, hb, LANE, LANE), lambda bi, hi: (bi, hi, 0, 0))
    s0spec = pl.BlockSpec((None, 1, hb, LANE, LANE), lambda bi, hi: (layer, bi, hi, 0, 0))
    return pl.pallas_call(
        functools.partial(_hgrn_sample_kernel, hb=hb, ts=ts),
        out_shape=(jax.ShapeDtypeStruct((bs * SROWS, heads * LANE), BF16),
                   jax.ShapeDtypeStruct((bs, heads, LANE, LANE), F32)),
        grid=(bs, heads // hb),
        in_specs=[hspec(offs["hq"]), hspec(offs["hf"]), hspec(offs["hi"]), hspec(offs["hg"]),
                  pl.BlockSpec((3, w), lambda bi, hi: (0, hi)),
                  pl.BlockSpec((1, w), lambda bi, hi: (0, hi)),
                  s0spec],
        out_specs=(pl.BlockSpec((SROWS, w), lambda bi, hi: (bi, hi)), sspec),
        compiler_params=_cparams(("parallel", "parallel")),
        name=name,
    )(h, h, h, h, lbc, nw, s0_all)


def _ssd_finish(ys, xs, zs, dsk, nw_ref, o_ref, width):
    gated = [(y + d * x) * _silu(z) for y, x, z, d in zip(ys, xs, zs, dsk)]
    ssq = sum(jnp.sum(t * t, axis=-1, keepdims=True) for t in gated)
    r = lax.rsqrt(ssq / width + EPS)
    for p, t in enumerate(gated):
        sl = slice(p * LANE, (p + 1) * LANE)
        o_ref[:, sl] = (t * r * nw_ref[:, sl]).astype(o_ref.dtype)


def _ssd_prompt_kernel(x_ref, b_ref, c_ref, z_ref, dt_ref, dtt_ref, pr_ref, pc_ref, nw_ref, tri_ref, triu_ref,
                       o_ref, s_ref, ht_scr, *, c, hpg, pdim):
    ci = pl.program_id(2)
    npair = hpg * pdim // LANE
    hpp = LANE // pdim

    @pl.when(ci == 0)
    def _():
        ht_scr[...] = jnp.zeros_like(ht_scr)

    pr = pr_ref[0]
    pc = pc_ref[0]
    dt_c = _softplus(dt_ref[0] + pr[0:1])
    da_c = dt_c * pr[1:2]
    da_r = _softplus(dtt_ref[0] + pc[:, 0:1]) * pc[:, 1:2]
    tri = tri_ref[...]
    triu = triu_ref[...]
    cum_c = sum(_dot(tri, p.astype(F32)) for p in _split3(da_c))
    cum_r = sum(_dot(p.astype(F32), triu) for p in _split3(da_r))
    bm = b_ref[...].astype(BF16)
    cm = c_ref[...].astype(BF16)
    cb = _dot_nt(cm, bm)
    trow = lax.broadcasted_iota(jnp.int32, (c, c), 0)
    scol = lax.broadcasted_iota(jnp.int32, (c, c), 1)
    causal = trow >= scol
    lane = lax.broadcasted_iota(jnp.int32, (1, LANE), 1)

    def per_head(vals):
        out = vals[-1]
        for k in range(hpp - 2, -1, -1):
            out = jnp.where(lane < (k + 1) * pdim, vals[k], out)
        return out

    ys, xs, zs, dsk = [], [], [], []
    for p in range(npair):
        sl = slice(p * LANE, (p + 1) * LANE)
        js = [p * hpp + k for k in range(hpp)]
        xp = x_ref[:, sl]
        xdt = xp * per_head([dt_c[:, j:j + 1] for j in js])
        xdtb = xdt.astype(BF16)
        yj = []
        for j in js:
            seg = jnp.exp(jnp.where(causal, cum_c[:, j:j + 1] - cum_r[j:j + 1, :], MASK_VALUE))
            yj.append(_dot((cb * seg).astype(BF16), xdtb))
        y = per_head(yj)
        htp = ht_scr[p]
        y = y + _dot(cm, htp.astype(BF16)) * per_head([jnp.exp(cum_c[:, j:j + 1]) for j in js])
        last = [cum_c[c - 1:c, j:j + 1] for j in js]
        wgt = per_head([jnp.exp(l - cum_c[:, j:j + 1]) for l, j in zip(last, js)])
        ht_new = htp * per_head([jnp.exp(l) for l in last]) + _dot_tn(bm, (xdt * wgt).astype(BF16))
        ht_scr[p] = ht_new
        ys.append(y)
        xs.append(xp)
        zs.append(z_ref[:, sl])
        dsk.append(per_head([pr[2:3, j:j + 1] for j in js]))

        @pl.when(ci == pl.num_programs(2) - 1)
        def _():
            s_ref[0, sl, :] = ht_new.T

    _ssd_finish(ys, xs, zs, dsk, nw_ref, o_ref, hpg * pdim)


def _ssd_prompt(xa, h, off_z, mdt_g, mdt_gt, prm_r, prm_c, nw, b, t, groups, hpg, pdim, dstate, name):
    c = SSD_CHUNK
    gw = hpg * pdim
    dinner = groups * gw
    nc = t // c
    npair = gw // LANE
    tt = np.arange(c)
    tri = jnp.asarray((tt[None, :] <= tt[:, None]).astype(np.float32))
    rowmap = lambda bi, gi, ci: (bi * nc + ci, gi)
    return pl.pallas_call(
        functools.partial(_ssd_prompt_kernel, c=c, hpg=hpg, pdim=pdim),
        out_shape=(jax.ShapeDtypeStruct((b * t, dinner), BF16),
                   jax.ShapeDtypeStruct((b, groups * gw, dstate), F32)),
        grid=(b, groups, nc),
        in_specs=[pl.BlockSpec((c, gw), rowmap),
                  pl.BlockSpec((c, dstate), lambda bi, gi, ci: (bi * nc + ci, dinner // dstate + gi)),
                  pl.BlockSpec((c, dstate), lambda bi, gi, ci: (bi * nc + ci, dinner // dstate + groups + gi)),
                  pl.BlockSpec((c, gw), lambda bi, gi, ci: (bi * nc + ci, off_z // gw + gi)),
                  pl.BlockSpec((1, c, hpg), lambda bi, gi, ci: (gi, bi * nc + ci, 0)),
                  pl.BlockSpec((1, hpg, c), lambda bi, gi, ci: (gi, 0, bi * nc + ci)),
                  pl.BlockSpec((1, 3, hpg), lambda bi, gi, ci: (gi, 0, 0)),
                  pl.BlockSpec((1, hpg, 3), lambda bi, gi, ci: (gi, 0, 0)),
                  pl.BlockSpec((1, gw), lambda bi, gi, ci: (0, gi)),
                  pl.BlockSpec((c, c), lambda bi, gi, ci: (0, 0)),
                  pl.BlockSpec((c, c), lambda bi, gi, ci: (0, 0))],
        out_specs=(pl.BlockSpec((c, gw), rowmap),
                   pl.BlockSpec((1, gw, dstate), lambda bi, gi, ci: (bi, gi, 0))),
        scratch_shapes=[pltpu.VMEM((npair, dstate, LANE), F32)],
        compiler_params=_cparams(("parallel", "parallel", "arbitrary"), 40 << 20),
        name=name,
    )(xa, xa, xa, h, mdt_g, mdt_gt, prm_r, prm_c, nw, tri, tri.T)


def _ssd_sample_kernel(x_ref, b_ref, c_ref, z_ref, dtx_ref, px_ref, nw_ref, s0_ref, o_ref, s_ref, *, ts, gw):
    npair = gw // LANE
    dstate = b_ref.shape[1]
    ridx = lax.broadcasted_iota(jnp.int32, (SROWS, gw), 0)
    dt = jnp.where(ridx < ts, _softplus(dtx_ref[...] + px_ref[0:1, :]), 0.0)
    dec = jnp.exp(dt * px_ref[1:2, :])
    x = x_ref[...]
    xdt = x * dt
    pad = jnp.zeros((LANE - SROWS, dstate), F32)
    bt = jnp.concatenate([b_ref[...], pad], axis=0).T
    ct = jnp.concatenate([c_ref[...], pad], axis=0).T
    ys, xs, zs, dsk = [], [], [], []
    for p in range(npair):
        sl = slice(p * LANE, (p + 1) * LANE)
        ht = s0_ref[0, sl, :].T
        yrow = []
        for i in range(ts):
            ht = ht * dec[i:i + 1, sl] + bt[:, i:i + 1] * xdt[i:i + 1, sl]
            yrow.append(jnp.sum(ht * ct[:, i:i + 1], axis=0, keepdims=True))
        s_ref[0, sl, :] = ht.T
        ys.append(_row_select(yrow, SROWS))
        xs.append(x[:, sl])
        zs.append(z_ref[:, sl])
        dsk.append(px_ref[2:3, sl])
    _ssd_finish(ys, xs, zs, dsk, nw_ref, o_ref, gw)


def _ssd_sample(xa_s, h, off_z, mdt_x, prm_x, nw, s0_all, layer, row0, bs, ts, groups, gw, dstate, name):
    dinner = groups * gw
    rb0 = row0 // SROWS
    sspec = pl.BlockSpec((1, gw, dstate), lambda bi, gi: (bi, gi, 0))
    s0spec = pl.BlockSpec((None, 1, gw, dstate), lambda bi, gi: (layer, bi, gi, 0))
    return pl.pallas_call(
        functools.partial(_ssd_sample_kernel, ts=ts, gw=gw),
        out_shape=(jax.ShapeDtypeStruct((bs * SROWS, dinner), BF16),
                   jax.ShapeDtypeStruct((bs, groups * gw, dstate), F32)),
        grid=(bs, groups),
        in_specs=[pl.BlockSpec((SROWS, gw), lambda bi, gi: (bi, gi)),
                  pl.BlockSpec((SROWS, dstate), lambda bi, gi: (bi, dinner // dstate + gi)),
                  pl.BlockSpec((SROWS, dstate), lambda bi, gi: (bi, dinner // dstate + groups + gi)),
                  pl.BlockSpec((SROWS, gw), lambda bi, gi: (rb0 + bi, off_z // gw + gi)),
                  pl.BlockSpec((SROWS, gw), lambda bi, gi: (bi, gi)),
                  pl.BlockSpec((3, gw), lambda bi, gi: (0, gi)),
                  pl.BlockSpec((1, gw), lambda bi, gi: (0, gi)),
                  s0spec],
        out_specs=(pl.BlockSpec((SROWS, gw), lambda bi, gi: (bi, gi)), sspec),
        compiler_params=_cparams(("parallel", "parallel")),
        name=name,
    )(xa_s, xa_s, xa_s, h, mdt_x, prm_x, nw, s0_all)


def _rot(x, cos, sin):
    return x * cos + pltpu.roll(x, LANE // 2, 1) * sin


def _residue_perm(dil, rows):
    i = np.arange(rows)
    p = np.zeros((rows, rows), np.float32)
    p[(i % dil) * (rows // dil) + i // dil, i] = 1.0
    return p


def _rope_prompt_kernel(q_ref, k_ref, v_ref, cos_ref, sin_ref, p_ref, qd_ref, kd_ref, vd_ref, ks_ref,
                        *, nh, dil, rows):
    n = rows // dil
    cos = cos_ref[...]
    sin = sin_ref[...]

    def emit(xb, dst_ref, sl):
        y = _dot(p_ref[...], xb).astype(BF16) if dil > 1 else xb
        for r in range(dil):
            dst_ref[0, r, :, sl] = y[r * n:(r + 1) * n]

    for hh in range(nh):
        sl = slice(hh * LANE, (hh + 1) * LANE)
        emit(_rot(q_ref[:, sl], cos, sin).astype(BF16), qd_ref, sl)
        kr = _rot(k_ref[:, sl], cos, sin)
        ks_ref[:, sl] = kr
        emit(kr.astype(BF16), kd_ref, sl)
        emit(v_ref[:, sl].astype(BF16), vd_ref, sl)


def _rope_prompt(h, offs, gi, dil, b, t, hpg, name):
    ow = hpg * LANE
    rows = _pick(t, (256, 128, 64, 32, 16))
    assert rows % (2 * SROWS * dil) == 0 and all(offs[k] % ow == 0 for k in ("aq", "ak", "av"))
    l, nb, n = t // dil, t // rows, rows // dil
    half = LANE // 2
    inv = ROPE_THETA ** (-jnp.arange(half, dtype=F32) / half)
    ang = jnp.arange(t, dtype=F32)[:, None] * inv[None, :]
    cos2 = jnp.concatenate([jnp.cos(ang), jnp.cos(ang)], axis=-1)
    sin2 = jnp.concatenate([-jnp.sin(ang), jnp.sin(ang)], axis=-1)
    perm = jnp.asarray(_residue_perm(dil, rows), BF16)

    def hspec(key):
        cb = offs[key] // ow + gi
        return pl.BlockSpec((rows, ow), lambda bi, j: (bi * nb + j, cb))

    tspec = pl.BlockSpec((rows, LANE), lambda bi, j: (j, 0))
    dspec = pl.BlockSpec((1, dil, n, ow), lambda bi, j: (bi, 0, j, 0))
    dshape = jax.ShapeDtypeStruct((b, dil, l, ow), BF16)
    return pl.pallas_call(
        functools.partial(_rope_prompt_kernel, nh=hpg, dil=dil, rows=rows),
        out_shape=(dshape, dshape, dshape, jax.ShapeDtypeStruct((b * t, ow), F32)),
        grid=(b, nb),
        in_specs=[hspec("aq"), hspec("ak"), hspec("av"), tspec, tspec,
                  pl.BlockSpec((rows, rows), lambda bi, j: (0, 0))],
        out_specs=(dspec, dspec, dspec, pl.BlockSpec((rows, ow), lambda bi, j: (bi * nb + j, 0))),
        compiler_params=_cparams(("parallel", "parallel")),
        name=name,
    )(h, h, h, cos2, sin2, perm)


def _rope_sample_kernel(q_ref, k_ref, cos_ref, sin_ref, qo_ref, ko_ref, *, nh):
    cos = cos_ref[...]
    sin = sin_ref[...]
    for hh in range(nh):
        sl = slice(hh * LANE, (hh + 1) * LANE)
        qo_ref[:, sl] = _rot(q_ref[:, sl], cos, sin)
        ko_ref[:, sl] = _rot(k_ref[:, sl], cos, sin)


def _rope_sample(h, offs, adim, row0, nrows, ts, name):
    tr = _pick(nrows, (256, 128, 64, 32, 16, 8), row0)
    tc = _pick(adim, (1024, 512, 256, 128), offs["aq"], offs["ak"])
    half = LANE // 2
    inv = ROPE_THETA ** (-jnp.arange(half, dtype=F32) / half)
    srow = jnp.arange(SROWS)
    pos = jnp.tile(jnp.where(srow < ts, PAST_LEN + srow, 0), nrows // SROWS).astype(F32)
    ang = pos[:, None] * inv[None, :]
    cos2 = jnp.concatenate([jnp.cos(ang), jnp.cos(ang)], axis=1)
    sin2 = jnp.concatenate([-jnp.sin(ang), jnp.sin(ang)], axis=1)
    rb0 = row0 // tr

    def hspec(off):
        return pl.BlockSpec((tr, tc), lambda i, j: (rb0 + i, off // tc + j))

    ospec = pl.BlockSpec((tr, tc), lambda i, j: (i, j))
    tspec = pl.BlockSpec((tr, LANE), lambda i, j: (i, 0))
    return pl.pallas_call(
        functools.partial(_rope_sample_kernel, nh=tc // LANE),
        out_shape=(jax.ShapeDtypeStruct((nrows, adim), F32), jax.ShapeDtypeStruct((nrows, adim), F32)),
        grid=(nrows // tr, adim // tc),
        in_specs=[hspec(offs["aq"]), hspec(offs["ak"]), tspec, tspec],
        out_specs=(ospec, ospec),
        compiler_params=_cparams(("parallel", "parallel")),
        name=name,
    )(h, h, cos2, sin2)


def _attn_prompt_kernel(q_ref, kp_ref, kc_ref, vp_ref, vc_ref, o_ref, l_ref, *, qb, nh, scale):
    has_prev = pl.program_id(2) > 0
    row = lax.broadcasted_iota(jnp.int32, (qb, qb), 0)
    col = lax.broadcasted_iota(jnp.int32, (qb, qb), 1)
    vprev = jnp.logical_and(col >= row, has_prev)
    vcur = col <= row
    for hh in range(nh):
        sl = slice(hh * LANE, (hh + 1) * LANE)
        q = q_ref[:, sl]
        sp = jnp.where(vprev, _dot_nt(q, kp_ref[:, sl].astype(BF16)) * scale, MASK_VALUE)
        sc = jnp.where(vcur, _dot_nt(q, kc_ref[:, sl].astype(BF16)) * scale, MASK_VALUE)
        m = jnp.maximum(jnp.max(sp, axis=-1, keepdims=True), jnp.max(sc, axis=-1, keepdims=True))
        pp = jnp.where(vprev, jnp.exp(sp - m), 0.0)
        pc = jnp.where(vcur, jnp.exp(sc - m), 0.0)
        den = jnp.sum(pp, axis=-1, keepdims=True) + jnp.sum(pc, axis=-1, keepdims=True)
        inv = 1.0 / den
        o = (_dot((pp * inv).astype(BF16), vp_ref[:, sl].astype(BF16))
             + _dot((pc * inv).astype(BF16), vc_ref[:, sl].astype(BF16)))
        o_ref[:, sl] = o
        l_ref[:, sl] = jnp.broadcast_to(m + jnp.log(den), (qb, LANE))


def _attn_prompt(qd, kd, vd, window, dil, name):
    b, _, l, ow = qd.shape
    qb = window // dil
    nq = l // qb
    cur = pl.BlockSpec((None, None, qb, ow), lambda bi, r, i: (bi, r, i, 0))
    prev = pl.BlockSpec((None, None, qb, ow), lambda bi, r, i: (bi, r, jnp.maximum(i - 1, 0), 0))
    oshape = jax.ShapeDtypeStruct((b, dil, l, ow), F32)
    return pl.pallas_call(
        functools.partial(_attn_prompt_kernel, qb=qb, nh=ow // LANE, scale=LANE ** -0.5),
        out_shape=(oshape, oshape),
        grid=(b, dil, nq),
        in_specs=[cur, prev, cur, prev, cur],
        out_specs=(cur, cur),
        compiler_params=_cparams(("parallel", "parallel", "arbitrary")),
        name=name,
    )(qd, kd, kd, vd, vd)


def _attn_sample_kernel(q_ref, kn_ref, vn_ref, kc_ref, vc_ref, o_ref, l_ref, *, ts, window, dil, scale):
    nk = window // dil
    o_ref[...] = jnp.zeros_like(o_ref)
    l_ref[...] = jnp.zeros_like(l_ref)
    arow = lax.broadcasted_iota(jnp.int32, (nk, 1, 1), 0)
    nrow = lax.broadcasted_iota(jnp.int32, (SROWS, 1, 1), 0)
    kn = kn_ref[...]
    vn = vn_ref[...]
    for i in range(ts):
        rho = (window + i) % dil
        base = (window + i - rho) // dil
        j0 = i // dil + 1
        a_lo, a_hi = max(base - nk, 0), min(base - j0, nk - 1)
        vcache = jnp.logical_and(arow >= a_lo, arow <= a_hi)
        new_rows = [i - j * dil for j in range(i // dil + 1)]
        vnew = functools.reduce(jnp.logical_or, [nrow == r for r in new_rows])
        q = q_ref[i][None]
        kc = kc_ref[:, rho]
        sc = jnp.where(vcache, jnp.sum(kc * q, axis=-1, keepdims=True) * scale, MASK_VALUE)
        sn = jnp.where(vnew, jnp.sum(kn * q, axis=-1, keepdims=True) * scale, MASK_VALUE)
        m = jnp.maximum(jnp.max(sc, axis=0, keepdims=True), jnp.max(sn, axis=0, keepdims=True))
        pc = jnp.where(vcache, jnp.exp(sc - m), 0.0)
        pn = jnp.where(vnew, jnp.exp(sn - m), 0.0)
        den = jnp.sum(pc, axis=0, keepdims=True) + jnp.sum(pn, axis=0, keepdims=True)
        acc = jnp.sum(pc * vc_ref[:, rho], axis=0) + jnp.sum(pn * vn, axis=0)
        o_ref[i] = acc / den[0]
        l_ref[i] = jnp.broadcast_to(m[0] + jnp.log(den[0]), acc.shape)


def _attn_sample(q3, k3, v3, ck_all, cv_all, layer, window, dil, bs, ts, name):
    hpg = q3.shape[1]
    nk = window // dil
    assert ck_all.shape[2] == window and window % dil == 0 and (dil == 1 or dil >= ts)
    nres = min(dil, ts)
    shape6 = ck_all.shape[:2] + (nk, dil, hpg, LANE)
    nspec = pl.BlockSpec((SROWS, hpg, LANE), lambda bi: (bi, 0, 0))
    cspec = pl.BlockSpec((None, None, nk, nres, hpg, LANE), lambda bi: (layer, bi, 0, 0, 0, 0))
    oshape = jax.ShapeDtypeStruct((bs * SROWS, hpg, LANE), F32)
    return pl.pallas_call(
        functools.partial(_attn_sample_kernel, ts=ts, window=window, dil=dil, scale=LANE ** -0.5),
        out_shape=(oshape, oshape),
        grid=(bs,),
        in_specs=[nspec, nspec, nspec, cspec, cspec],
        out_specs=(nspec, nspec),
        compiler_params=_cparams(("parallel",), 40 << 20),
        name=name,
    )(q3, k3, v3, ck_all.reshape(shape6), cv_all.reshape(shape6))


def _combine_kernel(*refs):
    ng = (len(refs) - 1) // 2
    os_, ls_, out = refs[:ng], refs[ng:2 * ng], refs[-1]
    out[...] = _mix_groups([o[...] for o in os_], [l[...] for l in ls_]).astype(out.dtype)


def _mix_groups(os_, ls):
    m = functools.reduce(jnp.maximum, ls)
    ws = [jnp.exp(l - m) for l in ls]
    den = functools.reduce(lambda a, b: a + b, ws)
    acc = functools.reduce(lambda a, b: a + b, [w * o for w, o in zip(ws, os_)])
    return acc / den


def _combine_prompt_kernel(*refs, dils):
    ng = len(dils)
    o_refs, l_refs, p_refs, out = refs[:ng], refs[ng:2 * ng], refs[2 * ng:3 * ng], refs[3 * ng]

    def token_order(ref, p_ref, d):
        x = jnp.concatenate([ref[0, r] for r in range(d)], axis=0)
        if d == 1:
            return x
        return sum(_dot(p_ref[...], piece) for piece in _split3(x))

    os_ = [token_order(o, p, d) for o, p, d in zip(o_refs, p_refs, dils)]
    ls = [token_order(l, p, d) for l, p, d in zip(l_refs, p_refs, dils)]
    out[...] = _mix_groups(os_, ls).astype(out.dtype)


def _combine_prompt(outs, lses, dils, name):
    b, _, _, ow = outs[0].shape
    t = outs[0].shape[1] * outs[0].shape[2]
    rows = _pick(t, (256, 128, 64, 32, 16))
    assert all(rows % (SROWS * d) == 0 for d in dils)
    nb = t // rows
    specs = [pl.BlockSpec((1, d, rows // d, ow), lambda bi, j: (bi, 0, j, 0)) for d in dils]
    perms = [jnp.asarray(_residue_perm(d, rows).T, BF16) for d in dils]
    pspec = pl.BlockSpec((rows, rows), lambda bi, j: (0, 0))
    return pl.pallas_call(
        functools.partial(_combine_prompt_kernel, dils=tuple(dils)),
        out_shape=jax.ShapeDtypeStruct((b * t, ow), BF16),
        grid=(b, nb),
        in_specs=specs + specs + [pspec] * len(dils),
        out_specs=pl.BlockSpec((rows, ow), lambda bi, j: (bi * nb + j, 0)),
        compiler_params=_cparams(("parallel", "parallel"), 40 << 20),
        name=name,
    )(*outs, *lses, *perms)


def _combine(outs, lses, name):
    n, ow = outs[0].shape
    tr = _pick(n, (512, 256, 128, 64, 8))
    spec = pl.BlockSpec((tr, ow), lambda i: (i, 0))
    return pl.pallas_call(
        _combine_kernel,
        out_shape=jax.ShapeDtypeStruct((n, ow), BF16),
        grid=(n // tr,),
        in_specs=[spec] * (2 * len(outs)),
        out_specs=spec,
        compiler_params=_cparams(("parallel",)),
        name=name,
    )(*outs, *lses)


def _merge_kernel(oa_ref, yb_ref, oc_ref, wa_ref, wb_ref, wc_ref, ga_ref, gb_ref, gc_ref, o_ref):
    acc = _sigmoid(ga_ref[...]) * _dot(oa_ref[...], wa_ref[...])
    acc = acc + _sigmoid(gb_ref[...]) * _dot(yb_ref[...], wb_ref[...])
    acc = acc + _sigmoid(gc_ref[...]) * _dot(oc_ref[...], wc_ref[...])
    o_ref[...] = acc.astype(o_ref.dtype)


def _merge(oa, yb, oc, wa, wb, wc, h, off_g, d, name):
    n = oa.shape[0]
    tm = _pick(n, (384, 256, 128, 88, 64, 8))
    tn = _pick(d, (512, 256, 128), off_g)
    ka, kb, kc = oa.shape[1], yb.shape[1], oc.shape[1]

    def gspec(k):
        return pl.BlockSpec((tm, tn), lambda j, i: (i, (off_g + k * d) // tn + j))

    return pl.pallas_call(
        _merge_kernel,
        out_shape=jax.ShapeDtypeStruct((n, d), BF16),
        grid=(d // tn, n // tm),
        in_specs=[pl.BlockSpec((tm, ka), lambda j, i: (i, 0)),
                  pl.BlockSpec((tm, kb), lambda j, i: (i, 0)),
                  pl.BlockSpec((tm, kc), lambda j, i: (i, 0)),
                  pl.BlockSpec((ka, tn), lambda j, i: (0, j)),
                  pl.BlockSpec((kb, tn), lambda j, i: (0, j)),
                  pl.BlockSpec((kc, tn), lambda j, i: (0, j)),
                  gspec(0), gspec(1), gspec(2)],
        out_specs=pl.BlockSpec((tm, tn), lambda j, i: (i, j)),
        compiler_params=_cparams(("parallel", "parallel"), 48 << 20),
        name=name,
    )(oa, yb, oc, wa, wb, wc, h, h, h)


def _layer_norm(y, g, b):
    mu = jnp.mean(y, axis=-1, keepdims=True)
    yc = y - mu
    var = jnp.mean(yc * yc, axis=-1, keepdims=True)
    return yc * lax.rsqrt(var + EPS) * g + b


def _outln_kernel(m_ref, w_ref, x_ref, g_ref, b_ref, o_ref, ob_ref, *, alpha):
    y = alpha * x_ref[...] + _dot(m_ref[...], w_ref[...])
    out = _layer_norm(y, g_ref[...], b_ref[...])
    o_ref[...] = out
    ob_ref[...] = out.astype(ob_ref.dtype)


def _outln(merged, w_out, x, g, b, alpha, name):
    n, d = x.shape
    tm = _pick(n, (384, 256, 128, 88, 64, 8))
    rspec = pl.BlockSpec((tm, d), lambda i: (i, 0))
    vspec = pl.BlockSpec((1, d), lambda i: (0, 0))
    return pl.pallas_call(
        functools.partial(_outln_kernel, alpha=alpha),
        out_shape=(jax.ShapeDtypeStruct((n, d), F32), jax.ShapeDtypeStruct((n, d), BF16)),
        grid=(n // tm,),
        in_specs=[rspec, pl.BlockSpec((d, d), lambda i: (0, 0)), rspec, vspec, vspec],
        out_specs=(rspec, rspec),
        compiler_params=_cparams(("parallel",), 48 << 20),
        name=name,
    )(merged, w_out, x, g, b)


def _resln_kernel(x_ref, f_ref, g_ref, b_ref, o_ref, ob_ref, *, alpha):
    out = _layer_norm(alpha * x_ref[...] + f_ref[...], g_ref[...], b_ref[...])
    o_ref[...] = out
    ob_ref[...] = out.astype(ob_ref.dtype)


def _resln(x, f, g, b, alpha, name):
    n, d = x.shape
    tm = _pick(n, (384, 256, 128, 88, 64, 8))
    rspec = pl.BlockSpec((tm, d), lambda i: (i, 0))
    vspec = pl.BlockSpec((1, d), lambda i: (0, 0))
    return pl.pallas_call(
        functools.partial(_resln_kernel, alpha=alpha),
        out_shape=(jax.ShapeDtypeStruct((n, d), F32), jax.ShapeDtypeStruct((n, d), BF16)),
        grid=(n // tm,),
        in_specs=[rspec, rspec, vspec, vspec],
        out_specs=(rspec, rspec),
        compiler_params=_cparams(("parallel",)),
        name=name,
    )(x, f, g, b)


def _ffn_a_kernel(te_ref, x_ref, w1_ref, w3_ref, o_ref):
    x = x_ref[...].astype(BF16)
    a = _dot(x, w1_ref[0])
    o_ref[...] = (_silu(a) * _dot(x, w3_ref[0])).astype(o_ref.dtype)


def _ffn_a(x, w1, w3, tile_expert, tm, name):
    n, d = x.shape
    f = w1.shape[2]
    tf = _pick(f, (1408, 1024, 512, 256, 128))
    xbytes = x.dtype.itemsize
    vm = 2 * (tm * d * xbytes + 2 * d * tf * 2 + tm * tf * 2) + 3 * tm * tf * 4 + (6 << 20)
    return pl.pallas_call(
        _ffn_a_kernel,
        out_shape=jax.ShapeDtypeStruct((n, f), BF16),
        grid_spec=pltpu.PrefetchScalarGridSpec(
            num_scalar_prefetch=1,
            grid=(f // tf, n // tm),
            in_specs=[pl.BlockSpec((tm, d), lambda j, i, te: (i, 0)),
                      pl.BlockSpec((1, d, tf), lambda j, i, te: (te[i], 0, j)),
                      pl.BlockSpec((1, d, tf), lambda j, i, te: (te[i], 0, j))],
            out_specs=pl.BlockSpec((tm, tf), lambda j, i, te: (i, j))),
        compiler_params=_cparams(("parallel", "arbitrary"), vm),
        name=name,
    )(tile_expert, x, w1, w3)


def _ffn_b_kernel(te_ref, h_ref, w2_ref, o_ref):
    o_ref[...] = _dot(h_ref[...], w2_ref[0])


def _ffn_b(hmid, w2, tile_expert, tm, name):
    n, f = hmid.shape
    d = w2.shape[2]
    tn = _pick(d, (512, 256, 128))
    vm = 2 * (tm * f * 2 + f * tn * 2 + tm * tn * 4) + (4 << 20)
    return pl.pallas_call(
        _ffn_b_kernel,
        out_shape=jax.ShapeDtypeStruct((n, d), F32),
        grid_spec=pltpu.PrefetchScalarGridSpec(
            num_scalar_prefetch=1,
            grid=(d // tn, n // tm),
            in_specs=[pl.BlockSpec((tm, f), lambda j, i, te: (i, 0)),
                      pl.BlockSpec((1, f, tn), lambda j, i, te: (te[i], 0, j))],
            out_specs=pl.BlockSpec((tm, tn), lambda j, i, te: (i, j))),
        compiler_params=_cparams(("parallel", "arbitrary"), vm),
        name=name,
    )(tile_expert, hmid, w2)


def _router_kernel(x_ref, w_ref, b_ref, e_ref, g_ref, *, n_exp):
    xs = _split3(x_ref[...])
    ws = _split3(w_ref[...])
    logits = b_ref[...]
    for i in range(3):
        for j in range(3 - i):
            logits = logits + _dot(xs[i], ws[j])
    lane = lax.broadcasted_iota(jnp.int32, logits.shape, 1)
    neg = jnp.float32(-jnp.inf)
    logits = jnp.where(lane < n_exp, logits, neg)
    m0 = jnp.max(logits, axis=-1, keepdims=True)
    i0 = jnp.min(jnp.where(logits == m0, lane, LANE), axis=-1, keepdims=True)
    rest = jnp.where(lane == i0, neg, logits)
    m1 = jnp.max(rest, axis=-1, keepdims=True)
    i1 = jnp.min(jnp.where(rest == m1, lane, LANE), axis=-1, keepdims=True)
    e1 = jnp.exp(m1 - m0)
    den = 1.0 + e1
    e_ref[...] = jnp.where(lane == 0, i0, jnp.where(lane == 1, i1, 0))
    g_ref[...] = jnp.where(lane == 0, 1.0 / den, jnp.where(lane == 1, e1 / den, 0.0))


def _router(x, w_router, b_router, name):
    n, d = x.shape
    n_exp = w_router.shape[1]
    wp = jnp.zeros((d, LANE), F32).at[:, :n_exp].set(w_router.astype(F32))
    bp = jnp.zeros((1, LANE), F32).at[0, :n_exp].set(b_router.astype(F32))
    tm = _pick(n, (384, 256, 128, 88, 64, 8))
    rspec = pl.BlockSpec((tm, LANE), lambda i: (i, 0))
    return pl.pallas_call(
        functools.partial(_router_kernel, n_exp=n_exp),
        out_shape=(jax.ShapeDtypeStruct((n, LANE), jnp.int32), jax.ShapeDtypeStruct((n, LANE), F32)),
        grid=(n // tm,),
        in_specs=[pl.BlockSpec((tm, d), lambda i: (i, 0)),
                  pl.BlockSpec((d, LANE), lambda i: (0, 0)),
                  pl.BlockSpec((1, LANE), lambda i: (0, 0))],
        out_specs=(rspec, rspec),
        compiler_params=_cparams(("parallel",)),
        name=name,
    )(x, wp, bp)


def _row_copy(src_hbm, row, dst_ref, dst_row, sem):
    return pltpu.make_async_copy(src_hbm.at[pl.ds(row, 1)], dst_ref.at[pl.ds(dst_row, 1)], sem)


def _gather_kernel(idx_ref, x_hbm, o_ref, sem, *, tm):
    base = pl.program_id(0) * tm

    def start(r, carry):
        _row_copy(x_hbm, idx_ref[base + r], o_ref, r, sem).start()
        return carry

    def wait(r, carry):
        _row_copy(x_hbm, 0, o_ref, r, sem).wait()
        return carry

    lax.fori_loop(0, tm, start, 0)
    lax.fori_loop(0, tm, wait, 0)


def _gather_rows(x, idx, tm, name):
    n_out = idx.shape[0]
    d = x.shape[1]
    return pl.pallas_call(
        functools.partial(_gather_kernel, tm=tm),
        out_shape=jax.ShapeDtypeStruct((n_out, d), x.dtype),
        grid_spec=pltpu.PrefetchScalarGridSpec(
            num_scalar_prefetch=1,
            grid=(n_out // tm,),
            in_specs=[pl.BlockSpec(memory_space=pl.ANY)],
            out_specs=pl.BlockSpec((tm, d), lambda i, idx_ref: (i, 0)),
            scratch_shapes=[pltpu.SemaphoreType.DMA(())]),
        compiler_params=_cparams(("arbitrary",)),
        name=name,
    )(idx, x)


def _moe_out_kernel(pos_ref, y_hbm, x_ref, gate_ref, g_ref, b_ref, o_ref, ob_ref, buf, sem, *, tm, alpha):
    base = pl.program_id(0) * tm

    def start(r, carry):
        for k in range(2):
            _row_copy(y_hbm, pos_ref[2 * (base + r) + k], buf.at[k], r, sem).start()
        return carry

    def wait(r, carry):
        for k in range(2):
            _row_copy(y_hbm, 0, buf.at[k], r, sem).wait()
        return carry

    lax.fori_loop(0, tm, start, 0)
    lax.fori_loop(0, tm, wait, 0)
    gt = gate_ref[...]
    f = buf[0] * gt[:, 0:1] + buf[1] * gt[:, 1:2]
    out = _layer_norm(alpha * x_ref[...] + f, g_ref[...], b_ref[...])
    o_ref[...] = out
    ob_ref[...] = out.astype(ob_ref.dtype)


def _moe_out(ys, pos, x, gates, g, b, alpha, name):
    n, d = x.shape
    tm = _pick(n, (256, 128, 88, 64, 8))
    rspec = pl.BlockSpec((tm, d), lambda i, p: (i, 0))
    vspec = pl.BlockSpec((1, d), lambda i, p: (0, 0))
    return pl.pallas_call(
        functools.partial(_moe_out_kernel, tm=tm, alpha=alpha),
        out_shape=(jax.ShapeDtypeStruct((n, d), F32), jax.ShapeDtypeStruct((n, d), BF16)),
        grid_spec=pltpu.PrefetchScalarGridSpec(
            num_scalar_prefetch=1,
            grid=(n // tm,),
            in_specs=[pl.BlockSpec(memory_space=pl.ANY), rspec,
                      pl.BlockSpec((tm, LANE), lambda i, p: (i, 0)), vspec, vspec],
            out_specs=(rspec, rspec),
            scratch_shapes=[pltpu.VMEM((2, tm, d), F32), pltpu.SemaphoreType.DMA(())]),
        compiler_params=_cparams(("arbitrary",)),
        name=name,
    )(pos, ys, x, gates, g, b)


def _moe_ffn(x, w_router, b_router, w1, w3, w2, g, b, alpha, tag):
    n, d = x.shape
    n_exp = w1.shape[0]
    tm = MOE_TILE
    e_pad, gates = _router(x, w_router, b_router, f"router_{tag}")
    e_flat = e_pad[:, :2].reshape(-1)
    n_slots = 2 * n
    onehot = (e_flat[:, None] == jnp.arange(n_exp, dtype=jnp.int32)[None, :]).astype(jnp.int32)
    csum = jnp.cumsum(onehot, axis=0)
    rank = jnp.take_along_axis(csum, e_flat[:, None], axis=1)[:, 0] - 1
    counts = csum[-1]
    padded = (counts + tm - 1) // tm * tm
    pad_end = jnp.cumsum(padded)
    pos = (pad_end - padded)[e_flat] + rank
    n_tiles = -(-(n_slots + n_exp * (tm - 1)) // tm)
    slot_tok = jnp.zeros((n_tiles * tm,), jnp.int32).at[pos].set(jnp.arange(n_slots, dtype=jnp.int32) // 2)
    tile_expert = jnp.minimum(
        jnp.searchsorted(pad_end, jnp.arange(n_tiles, dtype=jnp.int32) * tm, side="right"), n_exp - 1
    ).astype(jnp.int32)
    xs = _gather_rows(x, slot_tok, tm, f"moe_gather_{tag}")
    hmid = _ffn_a(xs, w1, w3, tile_expert, tm, f"moe_a_{tag}")
    ys = _ffn_b(hmid, w2, tile_expert, tm, f"moe_b_{tag}")
    return _moe_out(ys, pos.astype(jnp.int32), x, gates, g, b, alpha, f"moe_out_{tag}")


def kernel(x_prompt, x_sample, state_hgrn, state_ssm, state_conv, cache_k0, cache_v0, cache_k1, cache_v1,
           cache_k2, cache_v2, w_in, hg_lb, hg_norm_w, conv_w, conv_b, dt_bias, a_log, d_skip, ssm_norm_w,
           w_proj_a, w_proj_b, w_proj_c, w_out, ln1_g, ln1_b, ln2_g, ln2_b, ffn_w1, ffn_w3, ffn_w2,
           moe_router, moe_router_b, moe_w1, moe_w3, moe_w2):
    caches = ((cache_k0, cache_v0), (cache_k1, cache_v1), (cache_k2, cache_v2))
    depth = w_in.shape[0]
    b, t, d = x_prompt.shape
    bs, ts, _ = x_sample.shape
    _, _, hg_heads, hg_dk, hg_dv = state_hgrn.shape
    _, _, m_heads, pdim, dstate = state_ssm.shape
    conv_dim = state_conv.shape[-1]
    hpg, adh = cache_k0.shape[-2:]
    assert hg_dk == LANE and hg_dv == LANE and dstate == LANE and adh == LANE and ts <= SROWS
    hg_dim = hg_heads * LANE
    dinner = m_heads * pdim
    groups = (conv_dim - dinner) // (2 * dstate)
    m_hpg = m_heads // groups
    gw = m_hpg * pdim
    adim = len(A_GROUPS) * hpg * LANE
    assert gw % LANE == 0 and LANE % pdim == 0 and t % HG_CHUNK == 0 and t % SSD_CHUNK == 0
    ow = hpg * LANE
    alpha = (2 * depth) ** 0.25
    bt = b * t
    n = bt + bs * SROWS

    sizes = (("hq", hg_dim), ("hf", hg_dim), ("hi", hg_dim), ("hg", hg_dim), ("mz", dinner), ("xbc", conv_dim),
             ("mdt", m_heads), ("aq", adim), ("ak", adim), ("av", adim), ("gate", 3 * d))
    src, acc = {}, 0
    for name, sz in sizes:
        src[name] = acc
        acc += sz
    assert acc == w_in.shape[2]
    order = [k for k, _ in sizes if k != "mdt"] + ["mdt"]
    offs, acc = {}, 0
    for name in order:
        offs[name] = acc
        acc += dict(sizes)[name]
    mdt_pad = (-acc) % LANE
    assert offs["mz"] % gw == 0 and all(offs[k] % ow == 0 for k in ("aq", "ak", "av"))

    def pack_w_in(w):
        parts = [w[:, src[k]:src[k] + dict(sizes)[k]] for k in order]
        parts.append(jnp.zeros((d, mdt_pad), w.dtype))
        return jnp.concatenate(parts, axis=1).astype(BF16)

    xs_pad = jnp.zeros((bs, SROWS, d), F32).at[:, :ts].set(x_sample)
    x = jnp.concatenate([x_prompt.reshape(bt, d), xs_pad.reshape(bs * SROWS, d)], axis=0)
    xb = x.astype(BF16)

    state_ssm_r = state_ssm.reshape(depth, bs, m_heads * pdim, dstate)
    lb_sm = jax.nn.softmax(hg_lb.astype(F32), axis=0)
    lb_all = jnp.cumsum(lb_sm, axis=0) - lb_sm[0]

    out_hg_p, out_ssm_p, out_conv_p, out_hg_s, out_ssm_s, out_conv_s = [], [], [], [], [], []
    out_kv_p = [[] for _ in range(2 * len(A_GROUPS))]
    out_kv_s = [[] for _ in range(2 * len(A_GROUPS))]

    for l in range(depth):
        h = _matmul(xb, pack_w_in(w_in[l]), F32, f"in_proj_{l}")

        lb = lb_all[l]
        lbc = jnp.stack([jnp.log(jnp.maximum(lb, LB_FLOOR)), jnp.log1p(-lb), 1.0 - lb])
        nw_a = hg_norm_w[l][None, :]
        oa_p, hg_p = _hgrn_prompt(h, offs, lbc, nw_a, b, t, hg_heads, f"hgrn_prompt_{l}")
        oa_s, hg_s = _hgrn_sample(h, offs, lbc, nw_a, state_hgrn, l, bt, bs, ts, hg_heads, f"hgrn_sample_{l}")
        o_a = jnp.concatenate([oa_p, oa_s], axis=0)

        cw, cbias = conv_w[l], conv_b[l][None, :]
        conv_rows = _pick(t, (256, 128, 64, 32, 16, 8))
        xa_p = _conv_silu(h, offs["xbc"], conv_dim, cw, cbias, 0, bt, conv_rows, t // conv_rows, None,
                          f"conv_prompt_{l}")
        prev_s = jnp.zeros((bs, SROWS, conv_dim), F32).at[:, SROWS - (CONV_WIDTH - 1):].set(state_conv[l])
        xa_s = _conv_silu(h, offs["xbc"], conv_dim, cw, cbias, bt, bs * SROWS, SROWS, 1,
                          prev_s.reshape(bs * SROWS, conv_dim), f"conv_sample_{l}")
        mdt = h[:, offs["mdt"]:offs["mdt"] + m_heads]
        a_neg = -jnp.exp(a_log[l].astype(F32))
        prm = jnp.stack([dt_bias[l].astype(F32), a_neg, d_skip[l].astype(F32)])
        prm_r = prm.reshape(3, groups, m_hpg).transpose(1, 0, 2)
        mdt_g = mdt[:bt].reshape(bt, groups, m_hpg).transpose(1, 0, 2)
        nw_b = ssm_norm_w[l][None, :]
        yb_p, ssm_p = _ssd_prompt(xa_p, h, offs["mz"], mdt_g, mdt_g.transpose(0, 2, 1), prm_r,
                                  prm_r.transpose(0, 2, 1), nw_b, b, t, groups, m_hpg, pdim, dstate,
                                  f"ssd_prompt_{l}")
        yb_s, ssm_s = _ssd_sample(xa_s, h, offs["mz"], jnp.repeat(mdt[bt:], pdim, axis=1),
                                  jnp.repeat(prm, pdim, axis=1), nw_b,
                                  state_ssm_r, l, bt, bs, ts, groups, gw, dstate, f"ssd_sample_{l}")
        y_b = jnp.concatenate([yb_p, yb_s], axis=0)

        q_s, k_s = _rope_sample(h, offs, adim, bt, bs * SROWS, ts, f"rope_sample_{l}")
        v_s = h[bt:, offs["av"]:offs["av"] + adim]
        outs_p, lses_p, outs_s, lses_s, k_std = [], [], [], [], []
        for gi, (window, dil) in enumerate(A_GROUPS):
            qd, kd, vd, ks = _rope_prompt(h, offs, gi, dil, b, t, hpg, f"rope_prompt_{l}_{gi}")
            k_std.append(ks)
            o_g, l_g = _attn_prompt(qd, kd, vd, window, dil, f"attn_prompt_{l}_{gi}")
            outs_p.append(o_g)
            lses_p.append(l_g)
            g3 = lambda a: a[:, gi * ow:(gi + 1) * ow].reshape(bs * SROWS, hpg, LANE)
            o_g, l_g = _attn_sample(g3(q_s), g3(k_s), g3(v_s), caches[gi][0], caches[gi][1], l, window, dil,
                                    bs, ts, f"attn_sample_{l}_{gi}")
            outs_s.append(o_g.reshape(bs * SROWS, ow))
            lses_s.append(l_g.reshape(bs * SROWS, ow))
        o_c = jnp.concatenate([_combine_prompt(outs_p, lses_p, [dl for _, dl in A_GROUPS], f"attn_mix_prompt_{l}"),
                               _combine(outs_s, lses_s, f"attn_mix_sample_{l}")], axis=0)

        merged = _merge(o_a, y_b, o_c, w_proj_a[l].astype(BF16), w_proj_b[l].astype(BF16),
                        w_proj_c[l].astype(BF16), h, offs["gate"], d, f"merge_{l}")
        x, xb = _outln(merged, w_out[l].astype(BF16), x, ln1_g[l][None, :], ln1_b[l][None, :], alpha, f"out_ln1_{l}")

        if l % 2 == 0:
            i = l // 2
            tm = _pick(n, (768, 512, 384, 256, 128, 88, 64, 8))
            te = jnp.zeros((n // tm,), jnp.int32)
            hmid = _ffn_a(xb, ffn_w1[i:i + 1].astype(BF16), ffn_w3[i:i + 1].astype(BF16), te, tm, f"ffn_a_{l}")
            f = _ffn_b(hmid, ffn_w2[i:i + 1].astype(BF16), te, tm, f"ffn_b_{l}")
            x, xb = _resln(x, f, ln2_g[l][None, :], ln2_b[l][None, :], alpha, f"ln2_{l}")
        else:
            i = l // 2
            x, xb = _moe_ffn(x, moe_router[i], moe_router_b[i], moe_w1[i].astype(BF16), moe_w3[i].astype(BF16),
                             moe_w2[i].astype(BF16), ln2_g[l][None, :], ln2_b[l][None, :], alpha, str(l))

        keep_c = CONV_WIDTH - 1
        assert t >= keep_c
        hp3 = h[:bt].reshape(b, t, h.shape[1])
        out_conv_p.append(hp3[:, t - keep_c:, offs["xbc"]:offs["xbc"] + conv_dim])
        xbc_s = h[bt:, offs["xbc"]:offs["xbc"] + conv_dim].reshape(bs, SROWS, conv_dim)[:, :ts]
        out_conv_s.append(jnp.concatenate([state_conv[l], xbc_s], axis=1)[:, -keep_c:])
        out_hg_p.append(hg_p)
        out_hg_s.append(hg_s)
        out_ssm_p.append(ssm_p.reshape(b, m_heads, pdim, dstate))
        out_ssm_s.append(ssm_s.reshape(bs, m_heads, pdim, dstate))
        ksm = k_s.reshape(bs, SROWS, adim // LANE, LANE)[:, :ts]
        vsm = v_s.reshape(bs, SROWS, adim // LANE, LANE)[:, :ts]
        for gi, (window, _) in enumerate(A_GROUPS):
            hs = slice(gi * hpg, (gi + 1) * hpg)
            keep = min(window, t)
            v_cols = slice(offs["av"] + gi * ow, offs["av"] + (gi + 1) * ow)
            out_kv_p[2 * gi].append(k_std[gi].reshape(b, t, hpg, LANE)[:, t - keep:])
            out_kv_p[2 * gi + 1].append(hp3[:, t - keep:, v_cols].reshape(b, keep, hpg, LANE))
            out_kv_s[2 * gi].append(ksm[:, :, hs])
            out_kv_s[2 * gi + 1].append(vsm[:, :, hs])

    y_prompt = x[:bt].reshape(b, t, d)
    y_sample = x[bt:].reshape(bs, SROWS, d)[:, :ts]
    return (y_prompt, y_sample, jnp.stack(out_hg_p), jnp.stack(out_ssm_p), jnp.stack(out_conv_p),
            *[jnp.stack(o) for o in out_kv_p],
            jnp.stack(out_hg_s), jnp.stack(out_ssm_s), jnp.stack(out_conv_s),
            *[jnp.stack(o) for o in out_kv_s])
```

```python
import functools
import math

import numpy as np
import jax
import jax.numpy as jnp
from jax import lax
from jax.experimental import pallas as pl
from jax.experimental.pallas import tpu as pltpu

F32 = jnp.float32
BF16 = jnp.bfloat16

LANE = 128
SROWS = 8
VMEM_CAP = 56 * 1024 * 1024

A_GROUPS = ((128, 1), (512, 4), (2048, 16))
PAST_LEN = 8192
ROPE_THETA = 10000.0
MASK_VALUE = -1e30
LB_FLOOR = 1e-30
EPS = 1e-5
CONV_WIDTH = 4
HG_CHUNK = 128
SSD_CHUNK = 128
MOE_TILE = 256


def _cparams(sem, vmem_bytes=None):
    kw = dict(dimension_semantics=sem)
    if vmem_bytes is not None:
        kw["vmem_limit_bytes"] = int(min(max(vmem_bytes, 16 * 1024 * 1024), VMEM_CAP))
    return pltpu.CompilerParams(**kw)


def _pick(n, cands, *offsets):
    for c in cands:
        if n % c == 0 and all(o % c == 0 for o in offsets):
            return c
    raise ValueError(f"no tile for {n} in {cands} (offsets {offsets})")


def _sigmoid(x):
    return 1.0 / (1.0 + jnp.exp(-x))


def _silu(x):
    return x * _sigmoid(x)


def _softplus(x):
    return jnp.maximum(x, 0.0) + jnp.log1p(jnp.exp(-jnp.abs(x)))


def _dot(a, b):
    return jnp.dot(a, b, preferred_element_type=F32)


def _dot_nt(a, b):
    return lax.dot_general(a, b, (((1,), (1,)), ((), ())), preferred_element_type=F32)


def _dot_tn(a, b):
    return lax.dot_general(a, b, (((0,), (0,)), ((), ())), preferred_element_type=F32)


def _split2(x):
    hi = x.astype(BF16)
    lo = (x - hi.astype(F32)).astype(BF16)
    return hi, lo


def _split3(x):
    p1 = x.astype(BF16)
    r1 = x - p1.astype(F32)
    p2 = r1.astype(BF16)
    p3 = (r1 - p2.astype(F32)).astype(BF16)
    return p1, p2, p3


def _mm_kernel(x_ref, w_ref, o_ref):
    o_ref[...] = _dot(x_ref[...], w_ref[...]).astype(o_ref.dtype)


def _matmul(x, w, out_dtype, name):
    m, k = x.shape
    n = w.shape[1]
    tm = _pick(m, (768, 512, 384, 256, 128, 88, 64, 8))
    tn = _pick(n, (1408, 1152, 1024, 512, 384, 256, 128))
    vm = 2 * (tm * k * 2 + k * tn * 2 + tm * tn * 4) + (4 << 20)
    return pl.pallas_call(
        _mm_kernel,
        out_shape=jax.ShapeDtypeStruct((m, n), out_dtype),
        grid=(m // tm, n // tn),
        in_specs=[pl.BlockSpec((tm, k), lambda i, j: (i, 0)),
                  pl.BlockSpec((k, tn), lambda i, j: (0, j))],
        out_specs=pl.BlockSpec((tm, tn), lambda i, j: (i, j)),
        compiler_params=_cparams(("parallel", "parallel"), vm),
        name=name,
    )(x, w)


def _conv_kernel(u_ref, prev_ref, w_ref, b_ref, o_ref, *, rows, blocks_per_seq, zero_first):
    u = u_ref[...]
    prev = prev_ref[...]
    if zero_first:
        first = (pl.program_id(0) % blocks_per_seq) == 0
        prev = jnp.where(first, 0.0, prev)
    full = jnp.concatenate([prev, u], axis=0)
    acc = b_ref[...] + full[SROWS:SROWS + rows] * w_ref[CONV_WIDTH - 1:CONV_WIDTH, :]
    for i in range(CONV_WIDTH - 1):
        off = SROWS - (CONV_WIDTH - 1) + i
        acc = acc + full[off:off + rows] * w_ref[i:i + 1, :]
    o_ref[...] = _silu(acc)


def _conv_silu(h, off_xbc, conv_dim, w, b, row0, nrows, rows, blocks_per_seq, prev_arr, name):
    tc = _pick(conv_dim, (2048, 1536, 1024, 512, 256, 128), off_xbc)
    cb0 = off_xbc // tc
    rb0 = row0 // rows
    nb = nrows // rows
    if prev_arr is None:
        rpb = rows // SROWS
        prev_spec = pl.BlockSpec((SROWS, tc), lambda i, j: (jnp.maximum((rb0 + i) * rpb - 1, 0), cb0 + j))
        prev_in = h
    else:
        prev_spec = pl.BlockSpec((SROWS, tc), lambda i, j: (i, j))
        prev_in = prev_arr
    return pl.pallas_call(
        functools.partial(_conv_kernel, rows=rows, blocks_per_seq=blocks_per_seq, zero_first=prev_arr is None),
        out_shape=jax.ShapeDtypeStruct((nrows, conv_dim), F32),
        grid=(nb, conv_dim // tc),
        in_specs=[pl.BlockSpec((rows, tc), lambda i, j: (rb0 + i, cb0 + j)),
                  prev_spec,
                  pl.BlockSpec((CONV_WIDTH, tc), lambda i, j: (0, j)),
                  pl.BlockSpec((1, tc), lambda i, j: (0, j))],
        out_specs=pl.BlockSpec((rows, tc), lambda i, j: (i, j)),
        compiler_params=_cparams(("parallel", "parallel")),
        name=name,
    )(h, prev_in, w, b)


def _hgrn_consts(c):
    t = np.arange(c)[:, None]
    u = np.arange(c)[None, :]
    mats, masks = [], [np.eye(c)]
    h = 1
    while h < c:
        tb, ub = t // h, u // h
        mats.append(((tb % 2 == 1) & (ub == tb) & (u <= t)) | ((tb % 2 == 0) & (ub == tb) & (u > t)))
        masks.append((tb % 2 == 1) & (ub == tb - 1))
        h *= 2
    mats.append(u <= t)
    mall = np.concatenate([m.astype(np.float32) for m in mats], axis=0)
    return jnp.asarray(mall, BF16), jnp.asarray(np.stack(masks).astype(np.float32))


def _hgrn_gates(fx, hq, lbc):
    la, l1, oml = lbc[0:1], lbc[1:2], lbc[2:3]
    ls = jnp.minimum(fx, 0.0) - jnp.log1p(jnp.exp(-jnp.abs(fx)))
    ct = l1 + ls
    g = jnp.maximum(la, ct) + jnp.log1p(jnp.exp(-jnp.abs(la - ct)))
    kk = oml * (1.0 / (1.0 + jnp.exp(fx)))
    return g, kk, _silu(hq)


def _hgrn_out(o, nw, hg):
    ms = jnp.mean(o * o, axis=-1, keepdims=True)
    return (o * lax.rsqrt(ms + EPS) * nw) * _silu(hg)


def _hgrn_prompt_kernel(hq_ref, hf_ref, hi_ref, hg_ref, lbc_ref, nw_ref, mall_ref, mask_ref,
                        o_ref, s_ref, st_scr, *, hb, c, nlev):
    ci = pl.program_id(2)

    @pl.when(ci == 0)
    def _():
        st_scr[...] = jnp.zeros_like(st_scr)

    for hh in range(hb):
        sl = slice(hh * LANE, (hh + 1) * LANE)
        g, kk, q = _hgrn_gates(hf_ref[:, sl], hq_ref[:, sl], lbc_ref[:, sl])
        v = hi_ref[:, sl]
        vb = v.astype(BF16)
        g_hi, g_lo = _split2(g)
        e2 = _dot(mall_ref[...], jnp.concatenate([g_hi, g_lo], axis=1))
        ex = e2[:, :LANE] + e2[:, LANE:]
        a = _dot_nt(q.astype(BF16), kk.astype(BF16)) * mask_ref[0]
        for lv in range(nlev):
            w = jnp.exp(ex[lv * c:(lv + 1) * c])
            a = a + _dot_nt((q * w).astype(BF16), (kk * w).astype(BF16)) * mask_ref[lv + 1]
        bcum = ex[nlev * c:(nlev + 1) * c]
        erev = bcum[c - 1:c, :] - bcum
        st = st_scr[hh]
        o = _dot(a.astype(BF16), vb) + _dot_nt((q * jnp.exp(bcum)).astype(BF16), st.astype(BF16))
        st_new = st * jnp.exp(bcum[c - 1:c, :]) + _dot_tn(vb, (kk * jnp.exp(erev)).astype(BF16))
        st_scr[hh] = st_new
        o_ref[:, sl] = _hgrn_out(o, nw_ref[:, sl], hg_ref[:, sl]).astype(o_ref.dtype)

    @pl.when(ci == pl.num_programs(2) - 1)
    def _():
        for hh in range(hb):
            s_ref[0, hh] = st_scr[hh].T


def _hgrn_prompt(h, offs, lbc, nw, b, t, heads, name):
    c = HG_CHUNK
    hb = _pick(heads * LANE, (8 * LANE, 4 * LANE, 2 * LANE, LANE),
               *(offs[k] for k in ("hq", "hf", "hi", "hg"))) // LANE
    w = hb * LANE
    nlev = int(math.log2(c))
    mall, masks = _hgrn_consts(c)
    nc = t // c

    def hspec(off):
        return pl.BlockSpec((c, w), lambda bi, hi, ci: (bi * nc + ci, off // w + hi))

    return pl.pallas_call(
        functools.partial(_hgrn_prompt_kernel, hb=hb, c=c, nlev=nlev),
        out_shape=(jax.ShapeDtypeStruct((b * t, heads * LANE), BF16),
                   jax.ShapeDtypeStruct((b, heads, LANE, LANE), F32)),
        grid=(b, heads // hb, nc),
        in_specs=[hspec(offs["hq"]), hspec(offs["hf"]), hspec(offs["hi"]), hspec(offs["hg"]),
                  pl.BlockSpec((3, w), lambda bi, hi, ci: (0, hi)),
                  pl.BlockSpec((1, w), lambda bi, hi, ci: (0, hi)),
                  pl.BlockSpec(mall.shape, lambda bi, hi, ci: (0, 0)),
                  pl.BlockSpec(masks.shape, lambda bi, hi, ci: (0, 0, 0))],
        out_specs=(pl.BlockSpec((c, w), lambda bi, hi, ci: (bi * nc + ci, hi)),
                   pl.BlockSpec((1, hb, LANE, LANE), lambda bi, hi, ci: (bi, hi, 0, 0))),
        scratch_shapes=[pltpu.VMEM((hb, LANE, LANE), F32)],
        compiler_params=_cparams(("parallel", "parallel", "arbitrary"), 40 << 20),
        name=name,
    )(h, h, h, h, lbc, nw, mall, masks)


def _row_select(rows_list, n):
    width = rows_list[0].shape[1]
    ridx = lax.broadcasted_iota(jnp.int32, (n, width), 0)
    out = jnp.zeros((n, width), F32)
    for i, r in enumerate(rows_list):
        out = jnp.where(ridx == i, r, out)
    return out


def _hgrn_sample_kernel(hq_ref, hf_ref, hi_ref, hg_ref, lbc_ref, nw_ref, s0_ref, o_ref, s_ref, *, hb, ts):
    ridx = lax.broadcasted_iota(jnp.int32, (SROWS, LANE), 0)
    real = ridx < ts
    for hh in range(hb):
        sl = slice(hh * LANE, (hh + 1) * LANE)
        g, kk, q = _hgrn_gates(hf_ref[:, sl], hq_ref[:, sl], lbc_ref[:, sl])
        v = hi_ref[:, sl]
        g = jnp.where(real, g, 0.0)
        kk = jnp.where(real, kk, 0.0)
        brow = []
        for i in range(ts):
            brow.append(g[i:i + 1] if i == 0 else brow[-1] + g[i:i + 1])
        s0 = s0_ref[0, hh]
        bt = _row_select(brow, SROWS)
        o_inter = _dot(q * jnp.exp(bt), s0)
        orow = []
        for i in range(ts):
            acc = jnp.zeros((1, LANE), F32)
            for s in range(i + 1):
                wgt = jnp.sum(q[i:i + 1] * kk[s:s + 1] * jnp.exp(brow[i] - brow[s]), axis=-1, keepdims=True)
                acc = acc + wgt * v[s:s + 1]
            orow.append(acc)
        o = o_inter + _row_select(orow, SROWS)
        o_ref[:, sl] = _hgrn_out(o, nw_ref[:, sl], hg_ref[:, sl]).astype(o_ref.dtype)
        cols = [jnp.exp(brow[-1])] + [kk[s:s + 1] * jnp.exp(brow[-1] - brow[s]) for s in range(ts)]
        xt = jnp.concatenate([_row_select(cols, SROWS), jnp.zeros((LANE - SROWS, LANE), F32)], axis=0).T
        s_new = xt[:, 0:1] * s0
        for s in range(ts):
            s_new = s_new + xt[:, 1 + s:2 + s] * v[s:s + 1]
        s_ref[0, hh] = s_new


def _hgrn_sample(h, offs, lbc, nw, s0_all, layer, row0, bs, ts, heads, name):
    hb = _pick(heads * LANE, (8 * LANE, 4 * LANE, 2 * LANE, LANE),
               *(offs[k] for k in ("hq", "hf", "hi", "hg"))) // LANE
    w = hb * LANE
    rb0 = row0 // SROWS

    def hspec(off):
        return pl.BlockSpec((SROWS, w), lambda bi, hi: (rb0 + bi, off // w + hi))

    sspec = pl.BlockSpec((1, hb, LANE, LANE), lambda bi, hi: (bi, hi, 0, 0))
    s0spec = pl.BlockSpec((None, 1, hb, LANE, LANE), lambda bi, hi: (layer, bi, hi, 0, 0))
    return pl.pallas_call(
        functools.partial(_hgrn_sample_kernel, hb=hb, ts=ts),
        out_shape=(jax.ShapeDtypeStruct((bs * SROWS, heads * LANE), BF16),
                   jax.ShapeDtypeStruct((bs, heads, LANE, LANE), F32)),
        grid=(bs, heads // hb),
        in_specs=[hspec(offs["hq"]), hspec(offs["hf"]), hspec(offs["hi"]), hspec(offs["hg"]),
                  pl.BlockSpec((3, w), lambda bi, hi: (0, hi)),
                  pl.BlockSpec((1, w), lambda bi, hi: (0, hi)),
                  s0spec],
        out_specs=(pl.BlockSpec((SROWS, w), lambda bi, hi: (bi, hi)), sspec),
        compiler_params=_cparams(("parallel", "parallel")),
        name=name,
    )(h, h, h, h, lbc, nw, s0_all)


def _ssd_finish(ys, xs, zs, dsk, nw_ref, o_ref, width, col0=0):
    gated = [(y + d * x) * _silu(z) for y, x, z, d in zip(ys, xs, zs, dsk)]
    ssq = sum(jnp.sum(t * t, axis=-1, keepdims=True) for t in gated)
    r = lax.rsqrt(ssq / width + EPS)
    for p, t in enumerate(gated):
        sl = slice(col0 + p * LANE, col0 + (p + 1) * LANE)
        o_ref[:, sl] = (t * r * nw_ref[:, sl]).astype(o_ref.dtype)


def _ssd_prompt_kernel(x_ref, b_ref, c_ref, z_ref, dt_ref, dtt_ref, pr_ref, pc_ref, nw_ref, tri_ref, triu_ref,
                       o_ref, s_ref, ht_scr, *, c, hpg, pdim):
    ci = pl.program_id(2)
    npair = hpg * pdim // LANE
    hpp = LANE // pdim

    @pl.when(ci == 0)
    def _():
        ht_scr[...] = jnp.zeros_like(ht_scr)

    pr = pr_ref[0]
    pc = pc_ref[0]
    dt_c = _softplus(dt_ref[0] + pr[0:1])
    da_c = dt_c * pr[1:2]
    da_r = _softplus(dtt_ref[0] + pc[:, 0:1]) * pc[:, 1:2]
    tri = tri_ref[...]
    triu = triu_ref[...]
    cum_c = sum(_dot(tri, p.astype(F32)) for p in _split3(da_c))
    cum_r = sum(_dot(p.astype(F32), triu) for p in _split3(da_r))
    bm = b_ref[...].astype(BF16)
    cm = c_ref[...].astype(BF16)
    cb = _dot_nt(cm, bm)
    trow = lax.broadcasted_iota(jnp.int32, (c, c), 0)
    scol = lax.broadcasted_iota(jnp.int32, (c, c), 1)
    causal = trow >= scol
    lane = lax.broadcasted_iota(jnp.int32, (1, LANE), 1)

    def per_head(vals):
        out = vals[-1]
        for k in range(hpp - 2, -1, -1):
            out = jnp.where(lane < (k + 1) * pdim, vals[k], out)
        return out

    ys, xs, zs, dsk = [], [], [], []
    for p in range(npair):
        sl = slice(p * LANE, (p + 1) * LANE)
        js = [p * hpp + k for k in range(hpp)]
        xp = x_ref[:, sl]
        xdt = xp * per_head([dt_c[:, j:j + 1] for j in js])
        xdtb = xdt.astype(BF16)
        yj = []
        for j in js:
            seg = jnp.exp(jnp.where(causal, cum_c[:, j:j + 1] - cum_r[j:j + 1, :], MASK_VALUE))
            yj.append(_dot((cb * seg).astype(BF16), xdtb))
        y = per_head(yj)
        htp = ht_scr[p]
        y = y + _dot(cm, htp.astype(BF16)) * per_head([jnp.exp(cum_c[:, j:j + 1]) for j in js])
        last = [cum_c[c - 1:c, j:j + 1] for j in js]
        wgt = per_head([jnp.exp(l - cum_c[:, j:j + 1]) for l, j in zip(last, js)])
        ht_new = htp * per_head([jnp.exp(l) for l in last]) + _dot_tn(bm, (xdt * wgt).astype(BF16))
        ht_scr[p] = ht_new
        ys.append(y)
        xs.append(xp)
        zs.append(z_ref[:, sl])
        dsk.append(per_head([pr[2:3, j:j + 1] for j in js]))

    _ssd_finish(ys, xs, zs, dsk, nw_ref, o_ref, hpg * pdim)

    @pl.when(ci == pl.num_programs(2) - 1)
    def _():
        for p in range(npair):
            s_ref[0, p * LANE:(p + 1) * LANE, :] = ht_scr[p].T


def _ssd_prompt(xa, h, off_z, mdt_g, mdt_gt, prm_r, prm_c, nw, b, t, groups, hpg, pdim, dstate, name):
    c = SSD_CHUNK
    gw = hpg * pdim
    dinner = groups * gw
    nc = t // c
    npair = gw // LANE
    tt = np.arange(c)
    tri = jnp.asarray((tt[None, :] <= tt[:, None]).astype(np.float32))
    rowmap = lambda bi, gi, ci: (bi * nc + ci, gi)
    return pl.pallas_call(
        functools.partial(_ssd_prompt_kernel, c=c, hpg=hpg, pdim=pdim),
        out_shape=(jax.ShapeDtypeStruct((b * t, dinner), BF16),
                   jax.ShapeDtypeStruct((b, groups * gw, dstate), F32)),
        grid=(b, groups, nc),
        in_specs=[pl.BlockSpec((c, gw), rowmap),
                  pl.BlockSpec((c, dstate), lambda bi, gi, ci: (bi * nc + ci, dinner // dstate + gi)),
                  pl.BlockSpec((c, dstate), lambda bi, gi, ci: (bi * nc + ci, dinner // dstate + groups + gi)),
                  pl.BlockSpec((c, gw), lambda bi, gi, ci: (bi * nc + ci, off_z // gw + gi)),
                  pl.BlockSpec((1, c, hpg), lambda bi, gi, ci: (gi, bi * nc + ci, 0)),
                  pl.BlockSpec((1, hpg, c), lambda bi, gi, ci: (gi, 0, bi * nc + ci)),
                  pl.BlockSpec((1, 3, hpg), lambda bi, gi, ci: (gi, 0, 0)),
                  pl.BlockSpec((1, hpg, 3), lambda bi, gi, ci: (gi, 0, 0)),
                  pl.BlockSpec((1, gw), lambda bi, gi, ci: (0, gi)),
                  pl.BlockSpec((c, c), lambda bi, gi, ci: (0, 0)),
                  pl.BlockSpec((c, c), lambda bi, gi, ci: (0, 0))],
        out_specs=(pl.BlockSpec((c, gw), rowmap),
                   pl.BlockSpec((1, gw, dstate), lambda bi, gi, ci: (bi, gi, 0))),
        scratch_shapes=[pltpu.VMEM((npair, dstate, LANE), F32)],
        compiler_params=_cparams(("parallel", "parallel", "arbitrary"), 40 << 20),
        name=name,
    )(xa, xa, xa, h, mdt_g, mdt_gt, prm_r, prm_c, nw, tri, tri.T)


def _ssd_sample_kernel(x_ref, b_ref, c_ref, z_ref, dtx_ref, px_ref, nw_ref, s0_ref, o_ref, s_ref, *, ts, gw, ngb):
    npair = gw // LANE
    dstate = b_ref.shape[1] // ngb
    ridx = lax.broadcasted_iota(jnp.int32, (SROWS, ngb * gw), 0)
    dt = jnp.where(ridx < ts, _softplus(dtx_ref[...] + px_ref[0:1, :]), 0.0)
    dec = jnp.exp(dt * px_ref[1:2, :])
    x = x_ref[...]
    xdt = x * dt
    pad = jnp.zeros((LANE - SROWS, dstate), F32)
    for gs in range(ngb):
        nsl = slice(gs * dstate, (gs + 1) * dstate)
        bt = jnp.concatenate([b_ref[:, nsl], pad], axis=0).T
        ct = jnp.concatenate([c_ref[:, nsl], pad], axis=0).T
        ys, xs, zs, dsk = [], [], [], []
        for p in range(npair):
            sl = slice(gs * gw + p * LANE, gs * gw + (p + 1) * LANE)
            ht = s0_ref[0, sl, :].T
            yrow = []
            for i in range(ts):
                ht = ht * dec[i:i + 1, sl] + bt[:, i:i + 1] * xdt[i:i + 1, sl]
                yrow.append(jnp.sum(ht * ct[:, i:i + 1], axis=0, keepdims=True))
            s_ref[0, sl, :] = ht.T
            ys.append(_row_select(yrow, SROWS))
            xs.append(x[:, sl])
            zs.append(z_ref[:, sl])
            dsk.append(px_ref[2:3, sl])
        _ssd_finish(ys, xs, zs, dsk, nw_ref, o_ref, gw, col0=gs * gw)


def _ssd_sample(xa_s, h, off_z, mdt_x, prm_x, nw, s0_all, layer, row0, bs, ts, groups, gw, dstate, name):
    dinner = groups * gw
    rb0 = row0 // SROWS
    nb0 = dinner // dstate
    ngb = _pick(groups, (4, 2, 1), nb0, nb0 + groups, off_z // gw)
    w = ngb * gw
    sspec = pl.BlockSpec((1, w, dstate), lambda bi, gi: (bi, gi, 0))
    s0spec = pl.BlockSpec((None, 1, w, dstate), lambda bi, gi: (layer, bi, gi, 0))
    return pl.pallas_call(
        functools.partial(_ssd_sample_kernel, ts=ts, gw=gw, ngb=ngb),
        out_shape=(jax.ShapeDtypeStruct((bs * SROWS, dinner), BF16),
                   jax.ShapeDtypeStruct((bs, groups * gw, dstate), F32)),
        grid=(bs, groups // ngb),
        in_specs=[pl.BlockSpec((SROWS, w), lambda bi, gi: (bi, gi)),
                  pl.BlockSpec((SROWS, ngb * dstate), lambda bi, gi: (bi, nb0 // ngb + gi)),
                  pl.BlockSpec((SROWS, ngb * dstate), lambda bi, gi: (bi, (nb0 + groups) // ngb + gi)),
                  pl.BlockSpec((SROWS, w), lambda bi, gi: (rb0 + bi, off_z // w + gi)),
                  pl.BlockSpec((SROWS, w), lambda bi, gi: (bi, gi)),
                  pl.BlockSpec((3, w), lambda bi, gi: (0, gi)),
                  pl.BlockSpec((1, w), lambda bi, gi: (0, gi)),
                  s0spec],
        out_specs=(pl.BlockSpec((SROWS, w), lambda bi, gi: (bi, gi)), sspec),
        compiler_params=_cparams(("parallel", "parallel")),
        name=name,
    )(xa_s, xa_s, xa_s, h, mdt_x, prm_x, nw, s0_all)


def _rot(x, cos, sin):
    return x * cos + pltpu.roll(x, LANE // 2, 1) * sin


def _residue_perm(dil, rows):
    i = np.arange(rows)
    p = np.zeros((rows, rows), np.float32)
    p[(i % dil) * (rows // dil) + i // dil, i] = 1.0
    return p


def _rope_prompt_kernel(q_ref, k_ref, v_ref, cos_ref, sin_ref, p_ref, qd_ref, kd_ref, vd_ref, ks_ref,
                        *, nh, dil, rows):
    n = rows // dil
    cos = cos_ref[...]
    sin = sin_ref[...]

    def emit(xb, dst_ref, sl):
        y = _dot(p_ref[...], xb).astype(BF16) if dil > 1 else xb
        for r in range(dil):
            dst_ref[0, r, :, sl] = y[r * n:(r + 1) * n]

    for hh in range(nh):
        sl = slice(hh * LANE, (hh + 1) * LANE)
        emit(_rot(q_ref[:, sl], cos, sin).astype(BF16), qd_ref, sl)
        kr = _rot(k_ref[:, sl], cos, sin)
        ks_ref[:, sl] = kr
        emit(kr.astype(BF16), kd_ref, sl)
        emit(v_ref[:, sl].astype(BF16), vd_ref, sl)


def _rope_prompt(h, offs, gi, dil, b, t, hpg, name):
    ow = hpg * LANE
    rows = _pick(t, (256, 128, 64, 32, 16))
    assert rows % (2 * SROWS * dil) == 0 and all(offs[k] % ow == 0 for k in ("aq", "ak", "av"))
    l, nb, n = t // dil, t // rows, rows // dil
    half = LANE // 2
    inv = ROPE_THETA ** (-jnp.arange(half, dtype=F32) / half)
    ang = jnp.arange(t, dtype=F32)[:, None] * inv[None, :]
    cos2 = jnp.concatenate([jnp.cos(ang), jnp.cos(ang)], axis=-1)
    sin2 = jnp.concatenate([-jnp.sin(ang), jnp.sin(ang)], axis=-1)
    perm = jnp.asarray(_residue_perm(dil, rows), BF16)

    def hspec(key):
        cb = offs[key] // ow + gi
        return pl.BlockSpec((rows, ow), lambda bi, j: (bi * nb + j, cb))

    tspec = pl.BlockSpec((rows, LANE), lambda bi, j: (j, 0))
    dspec = pl.BlockSpec((1, dil, n, ow), lambda bi, j: (bi, 0, j, 0))
    dshape = jax.ShapeDtypeStruct((b, dil, l, ow), BF16)
    return pl.pallas_call(
        functools.partial(_rope_prompt_kernel, nh=hpg, dil=dil, rows=rows),
        out_shape=(dshape, dshape, dshape, jax.ShapeDtypeStruct((b * t, ow), F32)),
        grid=(b, nb),
        in_specs=[hspec("aq"), hspec("ak"), hspec("av"), tspec, tspec,
                  pl.BlockSpec((rows, rows), lambda bi, j: (0, 0))],
        out_specs=(dspec, dspec, dspec, pl.BlockSpec((rows, ow), lambda bi, j: (bi * nb + j, 0))),
        compiler_params=_cparams(("parallel", "parallel")),
        name=name,
    )(h, h, h, cos2, sin2, perm)


def _rope_sample_kernel(q_ref, k_ref, cos_ref, sin_ref, qo_ref, ko_ref, *, nh):
    cos = cos_ref[...]
    sin = sin_ref[...]
    for hh in range(nh):
        sl = slice(hh * LANE, (hh + 1) * LANE)
        qo_ref[:, sl] = _rot(q_ref[:, sl], cos, sin)
        ko_ref[:, sl] = _rot(k_ref[:, sl], cos, sin)


def _rope_sample(h, offs, adim, row0, nrows, ts, name):
    tr = _pick(nrows, (256, 128, 64, 32, 16, 8), row0)
    tc = _pick(adim, (1024, 512, 256, 128), offs["aq"], offs["ak"])
    half = LANE // 2
    inv = ROPE_THETA ** (-jnp.arange(half, dtype=F32) / half)
    srow = jnp.arange(SROWS)
    pos = jnp.tile(jnp.where(srow < ts, PAST_LEN + srow, 0), nrows // SROWS).astype(F32)
    ang = pos[:, None] * inv[None, :]
    cos2 = jnp.concatenate([jnp.cos(ang), jnp.cos(ang)], axis=1)
    sin2 = jnp.concatenate([-jnp.sin(ang), jnp.sin(ang)], axis=1)
    rb0 = row0 // tr

    def hspec(off):
        return pl.BlockSpec((tr, tc), lambda i, j: (rb0 + i, off // tc + j))

    ospec = pl.BlockSpec((tr, tc), lambda i, j: (i, j))
    tspec = pl.BlockSpec((tr, LANE), lambda i, j: (i, 0))
    return pl.pallas_call(
        functools.partial(_rope_sample_kernel, nh=tc // LANE),
        out_shape=(jax.ShapeDtypeStruct((nrows, adim), F32), jax.ShapeDtypeStruct((nrows, adim), F32)),
        grid=(nrows // tr, adim // tc),
        in_specs=[hspec(offs["aq"]), hspec(offs["ak"]), tspec, tspec],
        out_specs=(ospec, ospec),
        compiler_params=_cparams(("parallel", "parallel")),
        name=name,
    )(h, h, cos2, sin2)


def _attn_prompt_kernel(q_ref, kp_ref, kc_ref, vp_ref, vc_ref, o_ref, l_ref, s_scr, p_scr, *, qb, nh, scale):
    has_prev = pl.program_id(2) > 0
    row = lax.broadcasted_iota(jnp.int32, (qb, 2 * qb), 0)
    col = lax.broadcasted_iota(jnp.int32, (qb, 2 * qb), 1)
    in_prev = jnp.logical_and(jnp.logical_and(col < qb, col >= row), has_prev)
    valid = jnp.logical_or(in_prev, jnp.logical_and(col >= qb, col - qb <= row))
    for hh in range(nh):
        sl = slice(hh * LANE, (hh + 1) * LANE)
        kcat = jnp.concatenate([kp_ref[:, sl], kc_ref[:, sl]], axis=0)
        s_scr[hh] = _dot_nt(q_ref[:, sl], kcat)
    for hh in range(nh):
        sl = slice(hh * LANE, (hh + 1) * LANE)
        s = jnp.where(valid, s_scr[hh] * scale, MASK_VALUE)
        m = jnp.max(s, axis=-1, keepdims=True)
        p = jnp.exp(s - m)
        den = jnp.sum(p, axis=-1, keepdims=True)
        p_scr[hh] = (p * (1.0 / den)).astype(BF16)
        l_ref[:, sl] = jnp.broadcast_to(m + jnp.log(den), (qb, LANE))
    for hh in range(nh):
        sl = slice(hh * LANE, (hh + 1) * LANE)
        vcat = jnp.concatenate([vp_ref[:, sl], vc_ref[:, sl]], axis=0)
        o_ref[:, sl] = _dot(p_scr[hh], vcat)


def _attn_prompt(qd, kd, vd, window, dil, name):
    b, _, l, ow = qd.shape
    qb = window // dil
    nq = l // qb
    cur = pl.BlockSpec((None, None, qb, ow), lambda bi, r, i: (bi, r, i, 0))
    prev = pl.BlockSpec((None, None, qb, ow), lambda bi, r, i: (bi, r, jnp.maximum(i - 1, 0), 0))
    oshape = jax.ShapeDtypeStruct((b, dil, l, ow), F32)
    return pl.pallas_call(
        functools.partial(_attn_prompt_kernel, qb=qb, nh=ow // LANE, scale=LANE ** -0.5),
        out_shape=(oshape, oshape),
        grid=(b, dil, nq),
        in_specs=[cur, prev, cur, prev, cur],
        out_specs=(cur, cur),
        scratch_shapes=[pltpu.VMEM((ow // LANE, qb, 2 * qb), F32), pltpu.VMEM((ow // LANE, qb, 2 * qb), BF16)],
        compiler_params=_cparams(("parallel", "parallel", "arbitrary")),
        name=name,
    )(qd, kd, kd, vd, vd)


def _attn_sample_kernel(q_ref, kn_ref, vn_ref, kc_ref, vc_ref, o_ref, l_ref, *, ts, window, dil, scale):
    nk = window // dil
    o_ref[...] = jnp.zeros_like(o_ref)
    l_ref[...] = jnp.zeros_like(l_ref)
    arow = lax.broadcasted_iota(jnp.int32, (nk, 1, 1), 0)
    nrow = lax.broadcasted_iota(jnp.int32, (SROWS, 1, 1), 0)
    kn = kn_ref[...]
    vn = vn_ref[...]
    for i in range(ts):
        rho = (window + i) % dil
        base = (window + i - rho) // dil
        j0 = i // dil + 1
        a_lo, a_hi = max(base - nk, 0), min(base - j0, nk - 1)
        vcache = jnp.logical_and(arow >= a_lo, arow <= a_hi)
        new_rows = [i - j * dil for j in range(i // dil + 1)]
        vnew = functools.reduce(jnp.logical_or, [nrow == r for r in new_rows])
        q = q_ref[i][None]
        kc = kc_ref[:, rho]
        sc = jnp.where(vcache, jnp.sum(kc * q, axis=-1, keepdims=True) * scale, MASK_VALUE)
        sn = jnp.where(vnew, jnp.sum(kn * q, axis=-1, keepdims=True) * scale, MASK_VALUE)
        m = jnp.maximum(jnp.max(sc, axis=0, keepdims=True), jnp.max(sn, axis=0, keepdims=True))
        pc = jnp.where(vcache, jnp.exp(sc - m), 0.0)
        pn = jnp.where(vnew, jnp.exp(sn - m), 0.0)
        den = jnp.sum(pc, axis=0, keepdims=True) + jnp.sum(pn, axis=0, keepdims=True)
        acc = jnp.sum(pc * vc_ref[:, rho], axis=0) + jnp.sum(pn * vn, axis=0)
        o_ref[i] = acc / den[0]
        l_ref[i] = jnp.broadcast_to(m[0] + jnp.log(den[0]), acc.shape)


def _attn_sample(q3, k3, v3, ck_all, cv_all, layer, window, dil, bs, ts, name):
    hpg = q3.shape[1]
    nk = window // dil
    assert ck_all.shape[2] == window and window % dil == 0 and (dil == 1 or dil >= ts)
    nres = min(dil, ts)
    shape6 = ck_all.shape[:2] + (nk, dil, hpg, LANE)
    nspec = pl.BlockSpec((SROWS, hpg, LANE), lambda bi: (bi, 0, 0))
    cspec = pl.BlockSpec((None, None, nk, nres, hpg, LANE), lambda bi: (layer, bi, 0, 0, 0, 0))
    oshape = jax.ShapeDtypeStruct((bs * SROWS, hpg, LANE), F32)
    return pl.pallas_call(
        functools.partial(_attn_sample_kernel, ts=ts, window=window, dil=dil, scale=LANE ** -0.5),
        out_shape=(oshape, oshape),
        grid=(bs,),
        in_specs=[nspec, nspec, nspec, cspec, cspec],
        out_specs=(nspec, nspec),
        compiler_params=_cparams(("parallel",), 40 << 20),
        name=name,
    )(q3, k3, v3, ck_all.reshape(shape6), cv_all.reshape(shape6))


def _combine_kernel(*refs):
    ng = (len(refs) - 1) // 2
    os_, ls_, out = refs[:ng], refs[ng:2 * ng], refs[-1]
    out[...] = _mix_groups([o[...] for o in os_], [l[...] for l in ls_]).astype(out.dtype)


def _mix_groups(os_, ls):
    m = functools.reduce(jnp.maximum, ls)
    ws = [jnp.exp(l - m) for l in ls]
    den = functools.reduce(lambda a, b: a + b, ws)
    acc = functools.reduce(lambda a, b: a + b, [w * o for w, o in zip(ws, os_)])
    return acc / den


def _combine_prompt_kernel(*refs, dils):
    ng = len(dils)
    o_refs, l_refs, p_refs, out = refs[:ng], refs[ng:2 * ng], refs[2 * ng:3 * ng], refs[3 * ng]

    def token_order(ref, p_ref, d):
        x = jnp.concatenate([ref[0, r] for r in range(d)], axis=0)
        if d == 1:
            return x
        return sum(_dot(p_ref[...], piece) for piece in _split3(x))

    os_ = [token_order(o, p, d) for o, p, d in zip(o_refs, p_refs, dils)]
    ls = [token_order(l, p, d) for l, p, d in zip(l_refs, p_refs, dils)]
    out[...] = _mix_groups(os_, ls).astype(out.dtype)


def _combine_prompt(outs, lses, dils, name):
    b, _, _, ow = outs[0].shape
    t = outs[0].shape[1] * outs[0].shape[2]
    rows = _pick(t, (256, 128, 64, 32, 16))
    assert all(rows % (SROWS * d) == 0 for d in dils)
    nb = t // rows
    specs = [pl.BlockSpec((1, d, rows // d, ow), lambda bi, j: (bi, 0, j, 0)) for d in dils]
    perms = [jnp.asarray(_residue_perm(d, rows).T, BF16) for d in dils]
    pspec = pl.BlockSpec((rows, rows), lambda bi, j: (0, 0))
    return pl.pallas_call(
        functools.partial(_combine_prompt_kernel, dils=tuple(dils)),
        out_shape=jax.ShapeDtypeStruct((b * t, ow), BF16),
        grid=(b, nb),
        in_specs=specs + specs + [pspec] * len(dils),
        out_specs=pl.BlockSpec((rows, ow), lambda bi, j: (bi * nb + j, 0)),
        compiler_params=_cparams(("parallel", "parallel"), 40 << 20),
        name=name,
    )(*outs, *lses, *perms)


def _combine(outs, lses, name):
    n, ow = outs[0].shape
    tr = _pick(n, (512, 256, 128, 64, 8))
    spec = pl.BlockSpec((tr, ow), lambda i: (i, 0))
    return pl.pallas_call(
        _combine_kernel,
        out_shape=jax.ShapeDtypeStruct((n, ow), BF16),
        grid=(n // tr,),
        in_specs=[spec] * (2 * len(outs)),
        out_specs=spec,
        compiler_params=_cparams(("parallel",)),
        name=name,
    )(*outs, *lses)


def _merge_kernel(oa_ref, yb_ref, oc_ref, wa_ref, wb_ref, wc_ref, ga_ref, gb_ref, gc_ref, o_ref):
    acc = _sigmoid(ga_ref[...]) * _dot(oa_ref[...], wa_ref[...])
    acc = acc + _sigmoid(gb_ref[...]) * _dot(yb_ref[...], wb_ref[...])
    acc = acc + _sigmoid(gc_ref[...]) * _dot(oc_ref[...], wc_ref[...])
    o_ref[...] = acc.astype(o_ref.dtype)


def _merge(oa, yb, oc, wa, wb, wc, h, off_g, d, name):
    n = oa.shape[0]
    tm = _pick(n, (384, 256, 128, 88, 64, 8))
    tn = _pick(d, (512, 256, 128), off_g)
    ka, kb, kc = oa.shape[1], yb.shape[1], oc.shape[1]

    def gspec(k):
        return pl.BlockSpec((tm, tn), lambda j, i: (i, (off_g + k * d) // tn + j))

    return pl.pallas_call(
        _merge_kernel,
        out_shape=jax.ShapeDtypeStruct((n, d), BF16),
        grid=(d // tn, n // tm),
        in_specs=[pl.BlockSpec((tm, ka), lambda j, i: (i, 0)),
                  pl.BlockSpec((tm, kb), lambda j, i: (i, 0)),
                  pl.BlockSpec((tm, kc), lambda j, i: (i, 0)),
                  pl.BlockSpec((ka, tn), lambda j, i: (0, j)),
                  pl.BlockSpec((kb, tn), lambda j, i: (0, j)),
                  pl.BlockSpec((kc, tn), lambda j, i: (0, j)),
                  gspec(0), gspec(1), gspec(2)],
        out_specs=pl.BlockSpec((tm, tn), lambda j, i: (i, j)),
        compiler_params=_cparams(("parallel", "parallel"), 48 << 20),
        name=name,
    )(oa, yb, oc, wa, wb, wc, h, h, h)


def _layer_norm(y, g, b):
    mu = jnp.mean(y, axis=-1, keepdims=True)
    yc = y - mu
    var = jnp.mean(yc * yc, axis=-1, keepdims=True)
    return yc * lax.rsqrt(var + EPS) * g + b


def _outln_kernel(m_ref, w_ref, x_ref, g_ref, b_ref, o_ref, ob_ref, *, alpha):
    y = alpha * x_ref[...] + _dot(m_ref[...], w_ref[...])
    out = _layer_norm(y, g_ref[...], b_ref[...])
    o_ref[...] = out
    ob_ref[...] = out.astype(ob_ref.dtype)


def _outln(merged, w_out, x, g, b, alpha, name):
    n, d = x.shape
    tm = _pick(n, (384, 256, 128, 88, 64, 8))
    rspec = pl.BlockSpec((tm, d), lambda i: (i, 0))
    vspec = pl.BlockSpec((1, d), lambda i: (0, 0))
    return pl.pallas_call(
        functools.partial(_outln_kernel, alpha=alpha),
        out_shape=(jax.ShapeDtypeStruct((n, d), F32), jax.ShapeDtypeStruct((n, d), BF16)),
        grid=(n // tm,),
        in_specs=[rspec, pl.BlockSpec((d, d), lambda i: (0, 0)), rspec, vspec, vspec],
        out_specs=(rspec, rspec),
        compiler_params=_cparams(("parallel",), 48 << 20),
        name=name,
    )(merged, w_out, x, g, b)


def _resln_kernel(x_ref, f_ref, g_ref, b_ref, o_ref, ob_ref, *, alpha):
    out = _layer_norm(alpha * x_ref[...] + f_ref[...], g_ref[...], b_ref[...])
    o_ref[...] = out
    ob_ref[...] = out.astype(ob_ref.dtype)


def _resln(x, f, g, b, alpha, name):
    n, d = x.shape
    tm = _pick(n, (384, 256, 128, 88, 64, 8))
    rspec = pl.BlockSpec((tm, d), lambda i: (i, 0))
    vspec = pl.BlockSpec((1, d), lambda i: (0, 0))
    return pl.pallas_call(
        functools.partial(_resln_kernel, alpha=alpha),
        out_shape=(jax.ShapeDtypeStruct((n, d), F32), jax.ShapeDtypeStruct((n, d), BF16)),
        grid=(n // tm,),
        in_specs=[rspec, rspec, vspec, vspec],
        out_specs=(rspec, rspec),
        compiler_params=_cparams(("parallel",)),
        name=name,
    )(x, f, g, b)


def _ffn_a_kernel(te_ref, x_ref, w1_ref, w3_ref, o_ref):
    x = x_ref[...].astype(BF16)
    a = _dot(x, w1_ref[0])
    o_ref[...] = (_silu(a) * _dot(x, w3_ref[0])).astype(o_ref.dtype)


def _ffn_a(x, w1, w3, tile_expert, tm, name):
    n, d = x.shape
    f = w1.shape[2]
    tf = _pick(f, (1408, 1024, 512, 256, 128))
    xbytes = x.dtype.itemsize
    vm = 2 * (tm * d * xbytes + 2 * d * tf * 2 + tm * tf * 2) + 3 * tm * tf * 4 + (6 << 20)
    return pl.pallas_call(
        _ffn_a_kernel,
        out_shape=jax.ShapeDtypeStruct((n, f), BF16),
        grid_spec=pltpu.PrefetchScalarGridSpec(
            num_scalar_prefetch=1,
            grid=(f // tf, n // tm),
            in_specs=[pl.BlockSpec((tm, d), lambda j, i, te: (i, 0)),
                      pl.BlockSpec((1, d, tf), lambda j, i, te: (te[i], 0, j)),
                      pl.BlockSpec((1, d, tf), lambda j, i, te: (te[i], 0, j))],
            out_specs=pl.BlockSpec((tm, tf), lambda j, i, te: (i, j))),
        compiler_params=_cparams(("parallel", "arbitrary"), vm),
        name=name,
    )(tile_expert, x, w1, w3)


def _ffn_b_kernel(te_ref, h_ref, w2_ref, o_ref):
    o_ref[...] = _dot(h_ref[...], w2_ref[0])


def _ffn_b(hmid, w2, tile_expert, tm, name):
    n, f = hmid.shape
    d = w2.shape[2]
    tn = _pick(d, (512, 256, 128))
    vm = 2 * (tm * f * 2 + f * tn * 2 + tm * tn * 4) + (4 << 20)
    return pl.pallas_call(
        _ffn_b_kernel,
        out_shape=jax.ShapeDtypeStruct((n, d), F32),
        grid_spec=pltpu.PrefetchScalarGridSpec(
            num_scalar_prefetch=1,
            grid=(d // tn, n // tm),
            in_specs=[pl.BlockSpec((tm, f), lambda j, i, te: (i, 0)),
                      pl.BlockSpec((1, f, tn), lambda j, i, te: (te[i], 0, j))],
            out_specs=pl.BlockSpec((tm, tn), lambda j, i, te: (i, j))),
        compiler_params=_cparams(("parallel", "arbitrary"), vm),
        name=name,
    )(tile_expert, hmid, w2)


def _router_kernel(x_ref, w_ref, b_ref, e_ref, g_ref, *, n_exp):
    xs = _split3(x_ref[...])
    ws = _split3(w_ref[...])
    logits = b_ref[...]
    for i in range(3):
        for j in range(3 - i):
            logits = logits + _dot(xs[i], ws[j])
    lane = lax.broadcasted_iota(jnp.int32, logits.shape, 1)
    neg = jnp.float32(-jnp.inf)
    logits = jnp.where(lane < n_exp, logits, neg)
    m0 = jnp.max(logits, axis=-1, keepdims=True)
    i0 = jnp.min(jnp.where(logits == m0, lane, LANE), axis=-1, keepdims=True)
    rest = jnp.where(lane == i0, neg, logits)
    m1 = jnp.max(rest, axis=-1, keepdims=True)
    i1 = jnp.min(jnp.where(rest == m1, lane, LANE), axis=-1, keepdims=True)
    e1 = jnp.exp(m1 - m0)
    den = 1.0 + e1
    e_ref[...] = jnp.where(lane == 0, i0, jnp.where(lane == 1, i1, 0))
    g_ref[...] = jnp.where(lane == 0, 1.0 / den, jnp.where(lane == 1, e1 / den, 0.0))


def _router(x, w_router, b_router, name):
    n, d = x.shape
    n_exp = w_router.shape[1]
    wp = jnp.zeros((d, LANE), F32).at[:, :n_exp].set(w_router.astype(F32))
    bp = jnp.zeros((1, LANE), F32).at[0, :n_exp].set(b_router.astype(F32))
    tm = _pick(n, (384, 256, 128, 88, 64, 8))
    rspec = pl.BlockSpec((tm, LANE), lambda i: (i, 0))
    return pl.pallas_call(
        functools.partial(_router_kernel, n_exp=n_exp),
        out_shape=(jax.ShapeDtypeStruct((n, LANE), jnp.int32), jax.ShapeDtypeStruct((n, LANE), F32)),
        grid=(n // tm,),
        in_specs=[pl.BlockSpec((tm, d), lambda i: (i, 0)),
                  pl.BlockSpec((d, LANE), lambda i: (0, 0)),
                  pl.BlockSpec((1, LANE), lambda i: (0, 0))],
        out_specs=(rspec, rspec),
        compiler_params=_cparams(("parallel",)),
        name=name,
    )(x, wp, bp)


def _row_copy(src_hbm, row, dst_ref, dst_row, sem):
    return pltpu.make_async_copy(src_hbm.at[pl.ds(row, 1)], dst_ref.at[pl.ds(dst_row, 1)], sem)


def _gather_kernel(idx_ref, x_hbm, o_ref, sem, *, tm):
    base = pl.program_id(0) * tm

    def start(r, carry):
        _row_copy(x_hbm, idx_ref[base + r], o_ref, r, sem).start()
        return carry

    def wait(r, carry):
        _row_copy(x_hbm, 0, o_ref, r, sem).wait()
        return carry

    lax.fori_loop(0, tm, start, 0)
    lax.fori_loop(0, tm, wait, 0)


def _gather_rows(x, idx, tm, name):
    n_out = idx.shape[0]
    d = x.shape[1]
    return pl.pallas_call(
        functools.partial(_gather_kernel, tm=tm),
        out_shape=jax.ShapeDtypeStruct((n_out, d), x.dtype),
        grid_spec=pltpu.PrefetchScalarGridSpec(
            num_scalar_prefetch=1,
            grid=(n_out // tm,),
            in_specs=[pl.BlockSpec(memory_space=pl.ANY)],
            out_specs=pl.BlockSpec((tm, d), lambda i, idx_ref: (i, 0)),
            scratch_shapes=[pltpu.SemaphoreType.DMA(())]),
        compiler_params=_cparams(("arbitrary",)),
        name=name,
    )(idx, x)


def _moe_out_kernel(pos_ref, y_hbm, x_ref, gate_ref, g_ref, b_ref, o_ref, ob_ref, buf, sem, *, tm, alpha):
    base = pl.program_id(0) * tm

    def start(r, carry):
        for k in range(2):
            _row_copy(y_hbm, pos_ref[2 * (base + r) + k], buf.at[k], r, sem).start()
        return carry

    def wait(r, carry):
        for k in range(2):
            _row_copy(y_hbm, 0, buf.at[k], r, sem).wait()
        return carry

    lax.fori_loop(0, tm, start, 0)
    lax.fori_loop(0, tm, wait, 0)
    gt = gate_ref[...]
    f = buf[0] * gt[:, 0:1] + buf[1] * gt[:, 1:2]
    out = _layer_norm(alpha * x_ref[...] + f, g_ref[...], b_ref[...])
    o_ref[...] = out
    ob_ref[...] = out.astype(ob_ref.dtype)


def _moe_out(ys, pos, x, gates, g, b, alpha, name):
    n, d = x.shape
    tm = _pick(n, (256, 128, 88, 64, 8))
    rspec = pl.BlockSpec((tm, d), lambda i, p: (i, 0))
    vspec = pl.BlockSpec((1, d), lambda i, p: (0, 0))
    return pl.pallas_call(
        functools.partial(_moe_out_kernel, tm=tm, alpha=alpha),
        out_shape=(jax.ShapeDtypeStruct((n, d), F32), jax.ShapeDtypeStruct((n, d), BF16)),
        grid_spec=pltpu.PrefetchScalarGridSpec(
            num_scalar_prefetch=1,
            grid=(n // tm,),
            in_specs=[pl.BlockSpec(memory_space=pl.ANY), rspec,
                      pl.BlockSpec((tm, LANE), lambda i, p: (i, 0)), vspec, vspec],
            out_specs=(rspec, rspec),
            scratch_shapes=[pltpu.VMEM((2, tm, d), F32), pltpu.SemaphoreType.DMA(())]),
        compiler_params=_cparams(("arbitrary",)),
        name=name,
    )(pos, ys, x, gates, g, b)


def _moe_ffn(x, w_router, b_router, w1, w3, w2, g, b, alpha, tag):
    n, d = x.shape
    n_exp = w1.shape[0]
    tm = MOE_TILE
    e_pad, gates = _router(x, w_router, b_router, f"router_{tag}")
    e_flat = e_pad[:, :2].reshape(-1)
    n_slots = 2 * n
    onehot = (e_flat[:, None] == jnp.arange(n_exp, dtype=jnp.int32)[None, :]).astype(jnp.int32)
    csum = jnp.cumsum(onehot, axis=0)
    rank = jnp.take_along_axis(csum, e_flat[:, None], axis=1)[:, 0] - 1
    counts = csum[-1]
    padded = (counts + tm - 1) // tm * tm
    pad_end = jnp.cumsum(padded)
    pos = (pad_end - padded)[e_flat] + rank
    n_tiles = -(-(n_slots + n_exp * (tm - 1)) // tm)
    slot_tok = jnp.zeros((n_tiles * tm,), jnp.int32).at[pos].set(jnp.arange(n_slots, dtype=jnp.int32) // 2)
    tile_expert = jnp.minimum(
        jnp.searchsorted(pad_end, jnp.arange(n_tiles, dtype=jnp.int32) * tm, side="right"), n_exp - 1
    ).astype(jnp.int32)
    xs = _gather_rows(x, slot_tok, tm, f"moe_gather_{tag}")
    hmid = _ffn_a(xs, w1, w3, tile_expert, tm, f"moe_a_{tag}")
    ys = _ffn_b(hmid, w2, tile_expert, tm, f"moe_b_{tag}")
    return _moe_out(ys, pos.astype(jnp.int32), x, gates, g, b, alpha, f"moe_out_{tag}")


def kernel(x_prompt, x_sample, state_hgrn, state_ssm, state_conv, cache_k0, cache_v0, cache_k1, cache_v1,
           cache_k2, cache_v2, w_in, hg_lb, hg_norm_w, conv_w, conv_b, dt_bias, a_log, d_skip, ssm_norm_w,
           w_proj_a, w_proj_b, w_proj_c, w_out, ln1_g, ln1_b, ln2_g, ln2_b, ffn_w1, ffn_w3, ffn_w2,
           moe_router, moe_router_b, moe_w1, moe_w3, moe_w2):
    caches = ((cache_k0, cache_v0), (cache_k1, cache_v1), (cache_k2, cache_v2))
    depth = w_in.shape[0]
    b, t, d = x_prompt.shape
    bs, ts, _ = x_sample.shape
    _, _, hg_heads, hg_dk, hg_dv = state_hgrn.shape
    _, _, m_heads, pdim, dstate = state_ssm.shape
    conv_dim = state_conv.shape[-1]
    hpg, adh = cache_k0.shape[-2:]
    assert hg_dk == LANE and hg_dv == LANE and dstate == LANE and adh == LANE and ts <= SROWS
    hg_dim = hg_heads * LANE
    dinner = m_heads * pdim
    groups = (conv_dim - dinner) // (2 * dstate)
    m_hpg = m_heads // groups
    gw = m_hpg * pdim
    adim = len(A_GROUPS) * hpg * LANE
    assert gw % LANE == 0 and LANE % pdim == 0 and t % HG_CHUNK == 0 and t % SSD_CHUNK == 0
    ow = hpg * LANE
    alpha = (2 * depth) ** 0.25
    bt = b * t
    n = bt + bs * SROWS

    sizes = (("hq", hg_dim), ("hf", hg_dim), ("hi", hg_dim), ("hg", hg_dim), ("mz", dinner), ("xbc", conv_dim),
             ("mdt", m_heads), ("aq", adim), ("ak", adim), ("av", adim), ("gate", 3 * d))
    src, acc = {}, 0
    for name, sz in sizes:
        src[name] = acc
        acc += sz
    assert acc == w_in.shape[2]
    order = [k for k, _ in sizes if k != "mdt"] + ["mdt"]
    offs, acc = {}, 0
    for name in order:
        offs[name] = acc
        acc += dict(sizes)[name]
    mdt_pad = (-acc) % LANE
    assert offs["mz"] % gw == 0 and all(offs[k] % ow == 0 for k in ("aq", "ak", "av"))

    def pack_w_in(w):
        parts = [w[:, src[k]:src[k] + dict(sizes)[k]] for k in order]
        parts.append(jnp.zeros((d, mdt_pad), w.dtype))
        return jnp.concatenate(parts, axis=1).astype(BF16)

    xs_pad = jnp.zeros((bs, SROWS, d), F32).at[:, :ts].set(x_sample)
    x = jnp.concatenate([x_prompt.reshape(bt, d), xs_pad.reshape(bs * SROWS, d)], axis=0)
    xb = x.astype(BF16)

    state_ssm_r = state_ssm.reshape(depth, bs, m_heads * pdim, dstate)
    lb_sm = jax.nn.softmax(hg_lb.astype(F32), axis=0)
    lb_all = jnp.cumsum(lb_sm, axis=0) - lb_sm[0]

    out_hg_p, out_ssm_p, out_conv_p, out_hg_s, out_ssm_s, out_conv_s = [], [], [], [], [], []
    out_kv_p = [[] for _ in range(2 * len(A_GROUPS))]
    out_kv_s = [[] for _ in range(2 * len(A_GROUPS))]

    for l in range(depth):
        h = _matmul(xb, pack_w_in(w_in[l]), F32, f"in_proj_{l}")

        lb = lb_all[l]
        lbc = jnp.stack([jnp.log(jnp.maximum(lb, LB_FLOOR)), jnp.log1p(-lb), 1.0 - lb])
        nw_a = hg_norm_w[l][None, :]
        oa_p, hg_p = _hgrn_prompt(h, offs, lbc, nw_a, b, t, hg_heads, f"hgrn_prompt_{l}")
        oa_s, hg_s = _hgrn_sample(h, offs, lbc, nw_a, state_hgrn, l, bt, bs, ts, hg_heads, f"hgrn_sample_{l}")
        o_a = jnp.concatenate([oa_p, oa_s], axis=0)

        cw, cbias = conv_w[l], conv_b[l][None, :]
        conv_rows = _pick(t, (256, 128, 64, 32, 16, 8))
        xa_p = _conv_silu(h, offs["xbc"], conv_dim, cw, cbias, 0, bt, conv_rows, t // conv_rows, None,
                          f"conv_prompt_{l}")
        prev_s = jnp.zeros((bs, SROWS, conv_dim), F32).at[:, SROWS - (CONV_WIDTH - 1):].set(state_conv[l])
        xa_s = _conv_silu(h, offs["xbc"], conv_dim, cw, cbias, bt, bs * SROWS, SROWS, 1,
                          prev_s.reshape(bs * SROWS, conv_dim), f"conv_sample_{l}")
        mdt = h[:, offs["mdt"]:offs["mdt"] + m_heads]
        a_neg = -jnp.exp(a_log[l].astype(F32))
        prm = jnp.stack([dt_bias[l].astype(F32), a_neg, d_skip[l].astype(F32)])
        prm_r = prm.reshape(3, groups, m_hpg).transpose(1, 0, 2)
        mdt_g = mdt[:bt].reshape(bt, groups, m_hpg).transpose(1, 0, 2)
        nw_b = ssm_norm_w[l][None, :]
        yb_p, ssm_p = _ssd_prompt(xa_p, h, offs["mz"], mdt_g, mdt_g.transpose(0, 2, 1), prm_r,
                                  prm_r.transpose(0, 2, 1), nw_b, b, t, groups, m_hpg, pdim, dstate,
                                  f"ssd_prompt_{l}")
        yb_s, ssm_s = _ssd_sample(xa_s, h, offs["mz"], jnp.repeat(mdt[bt:], pdim, axis=1),
                                  jnp.repeat(prm, pdim, axis=1), nw_b,
                                  state_ssm_r, l, bt, bs, ts, groups, gw, dstate, f"ssd_sample_{l}")
        y_b = jnp.concatenate([yb_p, yb_s], axis=0)

        q_s, k_s = _rope_sample(h, offs, adim, bt, bs * SROWS, ts, f"rope_sample_{l}")
        v_s = h[bt:, offs["av"]:offs["av"] + adim]
        outs_p, lses_p, outs_s, lses_s, k_std = [], [], [], [], []
        for gi, (window, dil) in enumerate(A_GROUPS):
            qd, kd, vd, ks = _rope_prompt(h, offs, gi, dil, b, t, hpg, f"rope_prompt_{l}_{gi}")
            k_std.append(ks)
            o_g, l_g = _attn_prompt(qd, kd, vd, window, dil, f"attn_prompt_{l}_{gi}")
            outs_p.append(o_g)
            lses_p.append(l_g)
            g3 = lambda a: a[:, gi * ow:(gi + 1) * ow].reshape(bs * SROWS, hpg, LANE)
            o_g, l_g = _attn_sample(g3(q_s), g3(k_s), g3(v_s), caches[gi][0], caches[gi][1], l, window, dil,
                                    bs, ts, f"attn_sample_{l}_{gi}")
            outs_s.append(o_g.reshape(bs * SROWS, ow))
            lses_s.append(l_g.reshape(bs * SROWS, ow))
        o_c = jnp.concatenate([_combine_prompt(outs_p, lses_p, [dl for _, dl in A_GROUPS], f"attn_mix_prompt_{l}"),
                               _combine(outs_s, lses_s, f"attn_mix_sample_{l}")], axis=0)

        merged = _merge(o_a, y_b, o_c, w_proj_a[l].astype(BF16), w_proj_b[l].astype(BF16),
                        w_proj_c[l].astype(BF16), h, offs["gate"], d, f"merge_{l}")
        x, xb = _outln(merged, w_out[l].astype(BF16), x, ln1_g[l][None, :], ln1_b[l][None, :], alpha, f"out_ln1_{l}")

        if l % 2 == 0:
            i = l // 2
            tm = _pick(n, (768, 512, 384, 256, 128, 88, 64, 8))
            te = jnp.zeros((n // tm,), jnp.int32)
            hmid = _ffn_a(xb, ffn_w1[i:i + 1].astype(BF16), ffn_w3[i:i + 1].astype(BF16), te, tm, f"ffn_a_{l}")
            f = _ffn_b(hmid, ffn_w2[i:i + 1].astype(BF16), te, tm, f"ffn_b_{l}")
            x, xb = _resln(x, f, ln2_g[l][None, :], ln2_b[l][None, :], alpha, f"ln2_{l}")
        else:
            i = l // 2
            x, xb = _moe_ffn(x, moe_router[i], moe_router_b[i], moe_w1[i].astype(BF16), moe_w3[i].astype(BF16),
                             moe_w2[i].astype(BF16), ln2_g[l][None, :], ln2_b[l][None, :], alpha, str(l))

        keep_c = CONV_WIDTH - 1
        assert t >= keep_c

        def tail_rows(c0, c1, keep):
            return jnp.stack([h[(bi + 1) * t - keep:(bi + 1) * t, c0:c1] for bi in range(b)])

        out_conv_p.append(tail_rows(offs["xbc"], offs["xbc"] + conv_dim, keep_c))
        xbc_s = h[bt:, offs["xbc"]:offs["xbc"] + conv_dim].reshape(bs, SROWS, conv_dim)[:, :ts]
        out_conv_s.append(jnp.concatenate([state_conv[l], xbc_s], axis=1)[:, -keep_c:])
        out_hg_p.append(hg_p)
        out_hg_s.append(hg_s)
        out_ssm_p.append(ssm_p.reshape(b, m_heads, pdim, dstate))
        out_ssm_s.append(ssm_s.reshape(bs, m_heads, pdim, dstate))
        ksm = k_s.reshape(bs, SROWS, adim // LANE, LANE)[:, :ts]
        vsm = v_s.reshape(bs, SROWS, adim // LANE, LANE)[:, :ts]
        for gi, (window, _) in enumerate(A_GROUPS):
            hs = slice(gi * hpg, (gi + 1) * hpg)
            keep = min(window, t)
            v0 = offs["av"] + gi * ow
            out_kv_p[2 * gi].append(k_std[gi].reshape(b, t, hpg, LANE)[:, t - keep:])
            out_kv_p[2 * gi + 1].append(tail_rows(v0, v0 + ow, keep).reshape(b, keep, hpg, LANE))
            out_kv_s[2 * gi].append(ksm[:, :, hs])
            out_kv_s[2 * gi + 1].append(vsm[:, :, hs])

    y_prompt = x[:bt].reshape(b, t, d)
    y_sample = x[bt:].reshape(bs, SROWS, d)[:, :ts]
    return (y_prompt, y_sample, jnp.stack(out_hg_p), jnp.stack(out_ssm_p), jnp.stack(out_conv_p),
            *[jnp.stack(o) for o in out_kv_p],
            jnp.stack(out_hg_s), jnp.stack(out_ssm_s), jnp.stack(out_conv_s),
            *[jnp.stack(o) for o in out_kv_s])
```

```python
import functools
import math

import numpy as np
import jax
import jax.numpy as jnp
from jax import lax
from jax.experimental import pallas as pl
from jax.experimental.pallas import tpu as pltpu

F32 = jnp.float32
BF16 = jnp.bfloat16

LANE = 128
SROWS = 8
VMEM_CAP = 56 * 1024 * 1024

A_GROUPS = ((128, 1), (512, 4), (2048, 16))
PAST_LEN = 8192
ROPE_THETA = 10000.0
MASK_VALUE = -1e30
LB_FLOOR = 1e-30
EPS = 1e-5
LOG2E = 1.4426950408889634
CONV_WIDTH = 4
HG_CHUNK = 128
SSD_CHUNK = 128
MOE_TILE = 256


def _cparams(sem, vmem_bytes=None):
    kw = dict(dimension_semantics=sem)
    if vmem_bytes is not None:
        kw["vmem_limit_bytes"] = int(min(max(vmem_bytes, 16 * 1024 * 1024), VMEM_CAP))
    return pltpu.CompilerParams(**kw)


def _pick(n, cands, *offsets):
    for c in cands:
        if n % c == 0 and all(o % c == 0 for o in offsets):
            return c
    raise ValueError(f"no tile for {n} in {cands} (offsets {offsets})")


def _sigmoid(x):
    return 1.0 / (1.0 + jnp.exp(-x))


def _silu(x):
    return x * _sigmoid(x)


def _softplus(x):
    return jnp.maximum(x, 0.0) + jnp.log1p(jnp.exp(-jnp.abs(x)))


def _dot(a, b):
    return jnp.dot(a, b, preferred_element_type=F32)


def _dot_nt(a, b):
    return lax.dot_general(a, b, (((1,), (1,)), ((), ())), preferred_element_type=F32)


def _dot_tn(a, b):
    return lax.dot_general(a, b, (((0,), (0,)), ((), ())), preferred_element_type=F32)


def _split2(x):
    hi = x.astype(BF16)
    lo = (x - hi.astype(F32)).astype(BF16)
    return hi, lo


def _split3(x):
    p1 = x.astype(BF16)
    r1 = x - p1.astype(F32)
    p2 = r1.astype(BF16)
    p3 = (r1 - p2.astype(F32)).astype(BF16)
    return p1, p2, p3


def _mm_kernel(x_ref, w_ref, o_ref):
    o_ref[...] = _dot(x_ref[...], w_ref[...]).astype(o_ref.dtype)


def _matmul(x, w, out_dtype, name):
    m, k = x.shape
    n = w.shape[1]
    tm = _pick(m, (1408, 768, 512, 384, 256, 128, 88, 64, 8))
    tn = _pick(n, (1408, 1152, 1024, 512, 384, 256, 128))
    vm = 2 * (tm * k * 2 + k * tn * 2 + tm * tn * 4) + tm * tn * 4 + (4 << 20)
    return pl.pallas_call(
        _mm_kernel,
        out_shape=jax.ShapeDtypeStruct((m, n), out_dtype),
        grid=(m // tm, n // tn),
        in_specs=[pl.BlockSpec((tm, k), lambda i, j: (i, 0)),
                  pl.BlockSpec((k, tn), lambda i, j: (0, j))],
        out_specs=pl.BlockSpec((tm, tn), lambda i, j: (i, j)),
        compiler_params=_cparams(("parallel", "parallel"), vm),
        name=name,
    )(x, w)


def _conv_tile(u_ref, carry, w_ref, b_ref, rows):
    u = u_ref[...]
    full = jnp.concatenate([carry[...], u], axis=0)
    acc = b_ref[...] + u * w_ref[CONV_WIDTH - 1:CONV_WIDTH, :]
    for i in range(CONV_WIDTH - 1):
        off = SROWS - (CONV_WIDTH - 1) + i
        acc = acc + full[off:off + rows] * w_ref[i:i + 1, :]
    return u, _silu(acc)


def _conv_sample_kernel(u_ref, prev_ref, w_ref, b_ref, o_ref):
    o_ref[...] = _conv_tile(u_ref, prev_ref, w_ref, b_ref, SROWS)[1]


def _conv_sample(h, off_xbc, conv_dim, w, b, row0, prev_arr, name):
    nrows = prev_arr.shape[0]
    tc = _pick(conv_dim, (2048, 1536, 1024, 512, 256, 128), off_xbc)
    cb0 = off_xbc // tc
    rb0 = row0 // SROWS
    return pl.pallas_call(
        _conv_sample_kernel,
        out_shape=jax.ShapeDtypeStruct((nrows, conv_dim), F32),
        grid=(nrows // SROWS, conv_dim // tc),
        in_specs=[pl.BlockSpec((SROWS, tc), lambda i, j: (rb0 + i, cb0 + j)),
                  pl.BlockSpec((SROWS, tc), lambda i, j: (i, j)),
                  pl.BlockSpec((CONV_WIDTH, tc), lambda i, j: (0, j)),
                  pl.BlockSpec((1, tc), lambda i, j: (0, j))],
        out_specs=pl.BlockSpec((SROWS, tc), lambda i, j: (i, j)),
        compiler_params=_cparams(("parallel", "parallel")),
        name=name,
    )(h, prev_arr, w, b)


def _hgrn_consts(c):
    t = np.arange(c)[:, None]
    u = np.arange(c)[None, :]
    mats, masks = [], [np.eye(c)]
    h = 1
    while h < c:
        tb, ub = t // h, u // h
        mats.append(((tb % 2 == 1) & (ub == tb) & (u <= t)) | ((tb % 2 == 0) & (ub == tb) & (u > t)))
        masks.append((tb % 2 == 1) & (ub == tb - 1))
        h *= 2
    mats.append(u <= t)
    mall = np.concatenate([m.astype(np.float32) for m in mats], axis=0)
    mall = np.concatenate([mall, mall], axis=1)
    return jnp.asarray(mall, BF16), jnp.asarray(np.stack(masks).astype(np.float32))


def _hgrn_gates(fx, hq, lbc):
    la, l1, oml = lbc[0:1], lbc[1:2], lbc[2:3]
    ls = jnp.minimum(fx, 0.0) - jnp.log1p(jnp.exp(-jnp.abs(fx)))
    ct = l1 + ls
    g = jnp.maximum(la, ct) + jnp.log1p(jnp.exp(-jnp.abs(la - ct)))
    kk = oml * (1.0 / (1.0 + jnp.exp(fx)))
    return g, kk, _silu(hq)


def _hgrn_out(o, nw, hg):
    ms = jnp.mean(o * o, axis=-1, keepdims=True)
    return (o * lax.rsqrt(ms + EPS) * nw) * _silu(hg)


def _hgrn_prompt_kernel(hq_ref, hf_ref, hi_ref, hg_ref, lbc_ref, nw_ref, mall_ref, mask_ref,
                        o_ref, s_ref, st_scr, *, hb, c, nlev):
    ci = pl.program_id(2)

    @pl.when(ci == 0)
    def _():
        st_scr[...] = jnp.zeros_like(st_scr)

    for hh in range(hb):
        sl = slice(hh * LANE, (hh + 1) * LANE)
        g, kk, q = _hgrn_gates(hf_ref[:, sl], hq_ref[:, sl], lbc_ref[:, sl])
        v = hi_ref[:, sl]
        vb = v.astype(BF16)
        g_hi, g_lo = _split2(g * LOG2E)
        ex = _dot(mall_ref[...], jnp.concatenate([g_hi, g_lo], axis=0))
        a = _dot_nt(q.astype(BF16), kk.astype(BF16)) * mask_ref[0]
        for lv in range(nlev):
            w = jnp.exp2(ex[lv * c:(lv + 1) * c])
            a = a + _dot_nt((q * w).astype(BF16), (kk * w).astype(BF16)) * mask_ref[lv + 1]
        bcum = ex[nlev * c:(nlev + 1) * c]
        erev = bcum[c - 1:c, :] - bcum
        st = st_scr[hh]
        o = _dot(a.astype(BF16), vb) + _dot_nt((q * jnp.exp2(bcum)).astype(BF16), st.astype(BF16))
        st_new = st * jnp.exp2(bcum[c - 1:c, :]) + _dot_tn(vb, (kk * jnp.exp2(erev)).astype(BF16))
        st_scr[hh] = st_new
        o_ref[:, sl] = _hgrn_out(o, nw_ref[:, sl], hg_ref[:, sl]).astype(o_ref.dtype)

    @pl.when(ci == pl.num_programs(2) - 1)
    def _():
        for hh in range(hb):
            s_ref[0, hh] = st_scr[hh].T


def _hgrn_prompt(h, offs, lbc, nw, b, t, heads, name):
    c = HG_CHUNK
    hb = _pick(heads * LANE, (8 * LANE, 4 * LANE, 2 * LANE, LANE),
               *(offs[k] for k in ("hq", "hf", "hi", "hg"))) // LANE
    w = hb * LANE
    nlev = int(math.log2(c))
    mall, masks = _hgrn_consts(c)
    nc = t // c

    def hspec(off):
        return pl.BlockSpec((c, w), lambda bi, hi, ci: (bi * nc + ci, off // w + hi))

    return pl.pallas_call(
        functools.partial(_hgrn_prompt_kernel, hb=hb, c=c, nlev=nlev),
        out_shape=(jax.ShapeDtypeStruct((b * t, heads * LANE), BF16),
                   jax.ShapeDtypeStruct((b, heads, LANE, LANE), F32)),
        grid=(b, heads // hb, nc),
        in_specs=[hspec(offs["hq"]), hspec(offs["hf"]), hspec(offs["hi"]), hspec(offs["hg"]),
                  pl.BlockSpec((3, w), lambda bi, hi, ci: (0, hi)),
                  pl.BlockSpec((1, w), lambda bi, hi, ci: (0, hi)),
                  pl.BlockSpec(mall.shape, lambda bi, hi, ci: (0, 0)),
                  pl.BlockSpec(masks.shape, lambda bi, hi, ci: (0, 0, 0))],
        out_specs=(pl.BlockSpec((c, w), lambda bi, hi, ci: (bi * nc + ci, hi)),
                   pl.BlockSpec((1, hb, LANE, LANE), lambda bi, hi, ci: (bi, hi, 0, 0))),
        scratch_shapes=[pltpu.VMEM((hb, LANE, LANE), F32)],
        compiler_params=_cparams(("parallel", "parallel", "arbitrary"), 40 << 20),
        name=name,
    )(h, h, h, h, lbc, nw, mall, masks)


def _row_select(rows_list, n):
    width = rows_list[0].shape[1]
    ridx = lax.broadcasted_iota(jnp.int32, (n, width), 0)
    out = jnp.zeros((n, width), F32)
    for i, r in enumerate(rows_list):
        out = jnp.where(ridx == i, r, out)
    return out


def _hgrn_sample_kernel(hq_ref, hf_ref, hi_ref, hg_ref, lbc_ref, nw_ref, s0_ref, o_ref, s_ref, *, hb, ts):
    ridx = lax.broadcasted_iota(jnp.int32, (SROWS, LANE), 0)
    real = ridx < ts
    for hh in range(hb):
        sl = slice(hh * LANE, (hh + 1) * LANE)
        g, kk, q = _hgrn_gates(hf_ref[:, sl], hq_ref[:, sl], lbc_ref[:, sl])
        v = hi_ref[:, sl]
        g = jnp.where(real, g, 0.0)
        kk = jnp.where(real, kk, 0.0)
        brow = []
        for i in range(ts):
            brow.append(g[i:i + 1] if i == 0 else brow[-1] + g[i:i + 1])
        s0 = s0_ref[0, hh]
        bt = _row_select(brow, SROWS)
        o_inter = _dot(q * jnp.exp(bt), s0)
        orow = []
        for i in range(ts):
            acc = jnp.zeros((1, LANE), F32)
            for s in range(i + 1):
                wgt = jnp.sum(q[i:i + 1] * kk[s:s + 1] * jnp.exp(brow[i] - brow[s]), axis=-1, keepdims=True)
                acc = acc + wgt * v[s:s + 1]
            orow.append(acc)
        o = o_inter + _row_select(orow, SROWS)
        o_ref[:, sl] = _hgrn_out(o, nw_ref[:, sl], hg_ref[:, sl]).astype(o_ref.dtype)
        cols = [jnp.exp(brow[-1])] + [kk[s:s + 1] * jnp.exp(brow[-1] - brow[s]) for s in range(ts)]
        xt = jnp.concatenate([_row_select(cols, SROWS), jnp.zeros((LANE - SROWS, LANE), F32)], axis=0).T
        s_new = xt[:, 0:1] * s0
        for s in range(ts):
            s_new = s_new + xt[:, 1 + s:2 + s] * v[s:s + 1]
        s_ref[0, hh] = s_new


def _hgrn_sample(h, offs, lbc, nw, s0_all, layer, row0, bs, ts, heads, name):
    hb = _pick(heads * LANE, (8 * LANE, 4 * LANE, 2 * LANE, LANE),
               *(offs[k] for k in ("hq", "hf", "hi", "hg"))) // LANE
    w = hb * LANE
    rb0 = row0 // SROWS

    def hspec(off):
        return pl.BlockSpec((SROWS, w), lambda bi, hi: (rb0 + bi, off // w + hi))

    sspec = pl.BlockSpec((1, hb, LANE, LANE), lambda bi, hi: (bi, hi, 0, 0))
    s0spec = pl.BlockSpec((None, 1, hb, LANE, LANE), lambda bi, hi: (layer, bi, hi, 0, 0))
    return pl.pallas_call(
        functools.partial(_hgrn_sample_kernel, hb=hb, ts=ts),
        out_shape=(jax.ShapeDtypeStruct((bs * SROWS, heads * LANE), BF16),
                   jax.ShapeDtypeStruct((bs, heads, LANE, LANE), F32)),
        grid=(bs, heads // hb),
        in_specs=[hspec(offs["hq"]), hspec(offs["hf"]), hspec(offs["hi"]), hspec(offs["hg"]),
                  pl.BlockSpec((3, w), lambda bi, hi: (0, hi)),
                  pl.BlockSpec((1, w), lambda bi, hi: (0, hi)),
                  s0spec],
        out_specs=(pl.BlockSpec((SROWS, w), lambda bi, hi: (bi, hi)), sspec),
        compiler_params=_cparams(("parallel", "parallel")),
        name=name,
    )(h, h, h, h, lbc, nw, s0_all)


def _ssd_finish(ys, xs, zs, dsk, nw_ref, o_ref, width, col0=0):
    gated = [(y + d * x) * _silu(z) for y, x, z, d in zip(ys, xs, zs, dsk)]
    ssq = sum(jnp.sum(t * t, axis=-1, keepdims=True) for t in gated)
    r = lax.rsqrt(ssq / width + EPS)
    for p, t in enumerate(gated):
        sl = slice(col0 + p * LANE, col0 + (p + 1) * LANE)
        o_ref[:, sl] = (t * r * nw_ref[:, sl]).astype(o_ref.dtype)


def _conv_chunk(u_ref, carry, w_ref, b_ref, rows):
    u, act = _conv_tile(u_ref, carry, w_ref, b_ref, rows)
    carry[...] = u[rows - SROWS:rows]
    return act


def _ssd_prompt_kernel(xr_ref, br_ref, cr_ref, z_ref, dt_ref, dtt_ref, pr_ref, pc_ref, nw_ref, tri_ref, triu_ref,
                       wx_ref, wb_ref, wc_ref, bx_ref, bb_ref, bc_ref,
                       o_ref, s_ref, ht_scr, x_scr, cx_scr, cb_scr, cc_scr, *, c, hpg, pdim):
    ci = pl.program_id(2)
    npair = hpg * pdim // LANE
    hpp = LANE // pdim

    @pl.when(ci == 0)
    def _():
        ht_scr[...] = jnp.zeros_like(ht_scr)
        cx_scr[...] = jnp.zeros_like(cx_scr)
        cb_scr[...] = jnp.zeros_like(cb_scr)
        cc_scr[...] = jnp.zeros_like(cc_scr)

    x_scr[...] = _conv_chunk(xr_ref, cx_scr, wx_ref, bx_ref, c)
    x_ref = x_scr
    b_act = _conv_chunk(br_ref, cb_scr, wb_ref, bb_ref, c)
    c_act = _conv_chunk(cr_ref, cc_scr, wc_ref, bc_ref, c)

    pr = pr_ref[0]
    pc = pc_ref[0]
    dt_c = _softplus(dt_ref[0] + pr[0:1])
    da_c = dt_c * pr[1:2]
    da_r = _softplus(dtt_ref[0] + pc[:, 0:1]) * pc[:, 1:2]
    tri = tri_ref[...]
    triu = triu_ref[...]
    cum_c = sum(_dot(tri, p.astype(F32)) for p in _split3(da_c))
    cum_r = sum(_dot(p.astype(F32), triu) for p in _split3(da_r))
    bm = b_act.astype(BF16)
    cm = c_act.astype(BF16)
    cb = _dot_nt(cm, bm)
    trow = lax.broadcasted_iota(jnp.int32, (c, c), 0)
    scol = lax.broadcasted_iota(jnp.int32, (c, c), 1)
    causal = trow >= scol
    lane = lax.broadcasted_iota(jnp.int32, (1, LANE), 1)

    def per_head(vals):
        out = vals[-1]
        for k in range(hpp - 2, -1, -1):
            out = jnp.where(lane < (k + 1) * pdim, vals[k], out)
        return out

    ys, xs, zs, dsk = [], [], [], []
    for p in range(npair):
        sl = slice(p * LANE, (p + 1) * LANE)
        js = [p * hpp + k for k in range(hpp)]
        xp = x_ref[:, sl]
        xdt = xp * per_head([dt_c[:, j:j + 1] for j in js])
        xdtb = xdt.astype(BF16)
        yj = []
        for j in js:
            seg = jnp.exp(jnp.where(causal, cum_c[:, j:j + 1] - cum_r[j:j + 1, :], MASK_VALUE))
            yj.append(_dot((cb * seg).astype(BF16), xdtb))
        y = per_head(yj)
        htp = ht_scr[p]
        y = y + _dot(cm, htp.astype(BF16)) * per_head([jnp.exp(cum_c[:, j:j + 1]) for j in js])
        last = [cum_c[c - 1:c, j:j + 1] for j in js]
        wgt = per_head([jnp.exp(l - cum_c[:, j:j + 1]) for l, j in zip(last, js)])
        ht_new = htp * per_head([jnp.exp(l) for l in last]) + _dot_tn(bm, (xdt * wgt).astype(BF16))
        ht_scr[p] = ht_new
        ys.append(y)
        xs.append(xp)
        zs.append(z_ref[:, sl])
        dsk.append(per_head([pr[2:3, j:j + 1] for j in js]))

    _ssd_finish(ys, xs, zs, dsk, nw_ref, o_ref, hpg * pdim)

    @pl.when(ci == pl.num_programs(2) - 1)
    def _():
        for p in range(npair):
            s_ref[0, p * LANE:(p + 1) * LANE, :] = ht_scr[p].T


def _ssd_prompt(h, off_xbc, off_z, conv_w, conv_b, mdt_g, mdt_gt, prm_r, prm_c, nw, b, t, groups, hpg, pdim, dstate,
                name):
    c = SSD_CHUNK
    gw = hpg * pdim
    dinner = groups * gw
    nc = t // c
    npair = gw // LANE
    assert off_xbc % gw == 0 and (off_xbc + dinner) % dstate == 0 and c >= SROWS
    xb0 = off_xbc // gw
    nb0 = dinner // dstate
    hb0 = (off_xbc + dinner) // dstate
    tt = np.arange(c)
    tri = jnp.asarray((tt[None, :] <= tt[:, None]).astype(np.float32))
    rows = lambda bi, ci: bi * nc + ci
    const = lambda bi, gi, ci: (0, 0)
    return pl.pallas_call(
        functools.partial(_ssd_prompt_kernel, c=c, hpg=hpg, pdim=pdim),
        out_shape=(jax.ShapeDtypeStruct((b * t, dinner), BF16),
                   jax.ShapeDtypeStruct((b, groups * gw, dstate), F32)),
        grid=(b, groups, nc),
        in_specs=[pl.BlockSpec((c, gw), lambda bi, gi, ci: (rows(bi, ci), xb0 + gi)),
                  pl.BlockSpec((c, dstate), lambda bi, gi, ci: (rows(bi, ci), hb0 + gi)),
                  pl.BlockSpec((c, dstate), lambda bi, gi, ci: (rows(bi, ci), hb0 + groups + gi)),
                  pl.BlockSpec((c, gw), lambda bi, gi, ci: (rows(bi, ci), off_z // gw + gi)),
                  pl.BlockSpec((1, c, hpg), lambda bi, gi, ci: (gi, rows(bi, ci), 0)),
                  pl.BlockSpec((1, hpg, c), lambda bi, gi, ci: (gi, 0, rows(bi, ci))),
                  pl.BlockSpec((1, 3, hpg), lambda bi, gi, ci: (gi, 0, 0)),
                  pl.BlockSpec((1, hpg, 3), lambda bi, gi, ci: (gi, 0, 0)),
                  pl.BlockSpec((1, gw), lambda bi, gi, ci: (0, gi)),
                  pl.BlockSpec((c, c), const),
                  pl.BlockSpec((c, c), const),
                  pl.BlockSpec((CONV_WIDTH, gw), lambda bi, gi, ci: (0, gi)),
                  pl.BlockSpec((CONV_WIDTH, dstate), lambda bi, gi, ci: (0, nb0 + gi)),
                  pl.BlockSpec((CONV_WIDTH, dstate), lambda bi, gi, ci: (0, nb0 + groups + gi)),
                  pl.BlockSpec((1, gw), lambda bi, gi, ci: (0, gi)),
                  pl.BlockSpec((1, dstate), lambda bi, gi, ci: (0, nb0 + gi)),
                  pl.BlockSpec((1, dstate), lambda bi, gi, ci: (0, nb0 + groups + gi))],
        out_specs=(pl.BlockSpec((c, gw), lambda bi, gi, ci: (rows(bi, ci), gi)),
                   pl.BlockSpec((1, gw, dstate), lambda bi, gi, ci: (bi, gi, 0))),
        scratch_shapes=[pltpu.VMEM((npair, dstate, LANE), F32), pltpu.VMEM((c, gw), F32),
                        pltpu.VMEM((SROWS, gw), F32), pltpu.VMEM((SROWS, dstate), F32),
                        pltpu.VMEM((SROWS, dstate), F32)],
        compiler_params=_cparams(("parallel", "parallel", "arbitrary"), 40 << 20),
        name=name,
    )(h, h, h, h, mdt_g, mdt_gt, prm_r, prm_c, nw, tri, tri.T, conv_w, conv_w, conv_w, conv_b, conv_b, conv_b)


def _ssd_sample_kernel(x_ref, b_ref, c_ref, z_ref, dtx_ref, px_ref, nw_ref, s0_ref, o_ref, s_ref, *, ts, gw, ngb):
    npair = gw // LANE
    dstate = b_ref.shape[1] // ngb
    ridx = lax.broadcasted_iota(jnp.int32, (SROWS, ngb * gw), 0)
    dt = jnp.where(ridx < ts, _softplus(dtx_ref[...] + px_ref[0:1, :]), 0.0)
    dec = jnp.exp(dt * px_ref[1:2, :])
    x = x_ref[...]
    xdt = x * dt
    pad = jnp.zeros((LANE - SROWS, dstate), F32)
    for gs in range(ngb):
        nsl = slice(gs * dstate, (gs + 1) * dstate)
        bt = jnp.concatenate([b_ref[:, nsl], pad], axis=0).T
        ct = jnp.concatenate([c_ref[:, nsl], pad], axis=0).T
        ys, xs, zs, dsk = [], [], [], []
        for p in range(npair):
            sl = slice(gs * gw + p * LANE, gs * gw + (p + 1) * LANE)
            ht = s0_ref[0, sl, :].T
            yrow = []
            for i in range(ts):
                ht = ht * dec[i:i + 1, sl] + bt[:, i:i + 1] * xdt[i:i + 1, sl]
                yrow.append(jnp.sum(ht * ct[:, i:i + 1], axis=0, keepdims=True))
            s_ref[0, sl, :] = ht.T
            ys.append(_row_select(yrow, SROWS))
            xs.append(x[:, sl])
            zs.append(z_ref[:, sl])
            dsk.append(px_ref[2:3, sl])
        _ssd_finish(ys, xs, zs, dsk, nw_ref, o_ref, gw, col0=gs * gw)


def _ssd_sample(xa_s, h, off_z, mdt_x, prm_x, nw, s0_all, layer, row0, bs, ts, groups, gw, dstate, name):
    dinner = groups * gw
    rb0 = row0 // SROWS
    nb0 = dinner // dstate
    ngb = _pick(groups, (4, 2, 1), nb0, nb0 + groups, off_z // gw)
    w = ngb * gw
    sspec = pl.BlockSpec((1, w, dstate), lambda bi, gi: (bi, gi, 0))
    s0spec = pl.BlockSpec((None, 1, w, dstate), lambda bi, gi: (layer, bi, gi, 0))
    return pl.pallas_call(
        functools.partial(_ssd_sample_kernel, ts=ts, gw=gw, ngb=ngb),
        out_shape=(jax.ShapeDtypeStruct((bs * SROWS, dinner), BF16),
                   jax.ShapeDtypeStruct((bs, groups * gw, dstate), F32)),
        grid=(bs, groups // ngb),
        in_specs=[pl.BlockSpec((SROWS, w), lambda bi, gi: (bi, gi)),
                  pl.BlockSpec((SROWS, ngb * dstate), lambda bi, gi: (bi, nb0 // ngb + gi)),
                  pl.BlockSpec((SROWS, ngb * dstate), lambda bi, gi: (bi, (nb0 + groups) // ngb + gi)),
                  pl.BlockSpec((SROWS, w), lambda bi, gi: (rb0 + bi, off_z // w + gi)),
                  pl.BlockSpec((SROWS, w), lambda bi, gi: (bi, gi)),
                  pl.BlockSpec((3, w), lambda bi, gi: (0, gi)),
                  pl.BlockSpec((1, w), lambda bi, gi: (0, gi)),
                  s0spec],
        out_specs=(pl.BlockSpec((SROWS, w), lambda bi, gi: (bi, gi)), sspec),
        compiler_params=_cparams(("parallel", "parallel")),
        name=name,
    )(xa_s, xa_s, xa_s, h, mdt_x, prm_x, nw, s0_all)


def _rot(x, cos, sin):
    return x * cos + pltpu.roll(x, LANE // 2, 1) * sin


def _residue_perm(dil, rows):
    i = np.arange(rows)
    p = np.zeros((rows, rows), np.float32)
    p[(i % dil) * (rows // dil) + i // dil, i] = 1.0
    return p


def _rope_prompt_kernel(q_ref, k_ref, v_ref, cos_ref, sin_ref, p_ref, qd_ref, kd_ref, vd_ref, ks_ref,
                        *, nh, dil, rows):
    n = rows // dil
    cos = cos_ref[...]
    sin = sin_ref[...]

    def emit(xb, dst_ref, sl):
        y = _dot(p_ref[...], xb).astype(BF16) if dil > 1 else xb
        for r in range(dil):
            dst_ref[0, r, :, sl] = y[r * n:(r + 1) * n]

    for hh in range(nh):
        sl = slice(hh * LANE, (hh + 1) * LANE)
        emit(_rot(q_ref[:, sl], cos, sin).astype(BF16), qd_ref, sl)
        kr = _rot(k_ref[:, sl], cos, sin)
        ks_ref[:, sl] = kr
        emit(kr.astype(BF16), kd_ref, sl)
        emit(v_ref[:, sl].astype(BF16), vd_ref, sl)


def _rope_prompt(h, offs, gi, dil, b, t, hpg, name):
    ow = hpg * LANE
    rows = _pick(t, (256, 128, 64, 32, 16))
    assert rows % (2 * SROWS * dil) == 0 and all(offs[k] % ow == 0 for k in ("aq", "ak", "av"))
    l, nb, n = t // dil, t // rows, rows // dil
    half = LANE // 2
    inv = ROPE_THETA ** (-jnp.arange(half, dtype=F32) / half)
    ang = jnp.arange(t, dtype=F32)[:, None] * inv[None, :]
    cos2 = jnp.concatenate([jnp.cos(ang), jnp.cos(ang)], axis=-1)
    sin2 = jnp.concatenate([-jnp.sin(ang), jnp.sin(ang)], axis=-1)
    perm = jnp.asarray(_residue_perm(dil, rows), BF16)

    def hspec(key):
        cb = offs[key] // ow + gi
        return pl.BlockSpec((rows, ow), lambda bi, j: (bi * nb + j, cb))

    tspec = pl.BlockSpec((rows, LANE), lambda bi, j: (j, 0))
    dspec = pl.BlockSpec((1, dil, n, ow), lambda bi, j: (bi, 0, j, 0))
    dshape = jax.ShapeDtypeStruct((b, dil, l, ow), BF16)
    return pl.pallas_call(
        functools.partial(_rope_prompt_kernel, nh=hpg, dil=dil, rows=rows),
        out_shape=(dshape, dshape, dshape, jax.ShapeDtypeStruct((b * t, ow), F32)),
        grid=(b, nb),
        in_specs=[hspec("aq"), hspec("ak"), hspec("av"), tspec, tspec,
                  pl.BlockSpec((rows, rows), lambda bi, j: (0, 0))],
        out_specs=(dspec, dspec, dspec, pl.BlockSpec((rows, ow), lambda bi, j: (bi * nb + j, 0))),
        compiler_params=_cparams(("parallel", "parallel")),
        name=name,
    )(h, h, h, cos2, sin2, perm)


def _rope_sample_kernel(q_ref, k_ref, cos_ref, sin_ref, qo_ref, ko_ref, *, nh):
    cos = cos_ref[...]
    sin = sin_ref[...]
    for hh in range(nh):
        sl = slice(hh * LANE, (hh + 1) * LANE)
        qo_ref[:, sl] = _rot(q_ref[:, sl], cos, sin)
        ko_ref[:, sl] = _rot(k_ref[:, sl], cos, sin)


def _rope_sample(h, offs, adim, row0, nrows, ts, name):
    tr = _pick(nrows, (256, 128, 64, 32, 16, 8), row0)
    tc = _pick(adim, (1024, 512, 256, 128), offs["aq"], offs["ak"])
    half = LANE // 2
    inv = ROPE_THETA ** (-jnp.arange(half, dtype=F32) / half)
    srow = jnp.arange(SROWS)
    pos = jnp.tile(jnp.where(srow < ts, PAST_LEN + srow, 0), nrows // SROWS).astype(F32)
    ang = pos[:, None] * inv[None, :]
    cos2 = jnp.concatenate([jnp.cos(ang), jnp.cos(ang)], axis=1)
    sin2 = jnp.concatenate([-jnp.sin(ang), jnp.sin(ang)], axis=1)
    rb0 = row0 // tr

    def hspec(off):
        return pl.BlockSpec((tr, tc), lambda i, j: (rb0 + i, off // tc + j))

    ospec = pl.BlockSpec((tr, tc), lambda i, j: (i, j))
    tspec = pl.BlockSpec((tr, LANE), lambda i, j: (i, 0))
    return pl.pallas_call(
        functools.partial(_rope_sample_kernel, nh=tc // LANE),
        out_shape=(jax.ShapeDtypeStruct((nrows, adim), F32), jax.ShapeDtypeStruct((nrows, adim), F32)),
        grid=(nrows // tr, adim // tc),
        in_specs=[hspec(offs["aq"]), hspec(offs["ak"]), tspec, tspec],
        out_specs=(ospec, ospec),
        compiler_params=_cparams(("parallel", "parallel")),
        name=name,
    )(h, h, cos2, sin2)


def _attn_prompt_kernel(q_ref, kp_ref, kc_ref, vp_ref, vc_ref, o_ref, l_ref, s_scr, p_scr, *, qb, nh, scale):
    has_prev = pl.program_id(2) > 0
    row = lax.broadcasted_iota(jnp.int32, (qb, 2 * qb), 0)
    col = lax.broadcasted_iota(jnp.int32, (qb, 2 * qb), 1)
    in_prev = jnp.logical_and(jnp.logical_and(col < qb, col >= row), has_prev)
    valid = jnp.logical_or(in_prev, jnp.logical_and(col >= qb, col - qb <= row))
    for hh in range(nh):
        sl = slice(hh * LANE, (hh + 1) * LANE)
        kcat = jnp.concatenate([kp_ref[:, sl], kc_ref[:, sl]], axis=0)
        s_scr[hh] = _dot_nt(q_ref[:, sl], kcat)
    for hh in range(nh):
        sl = slice(hh * LANE, (hh + 1) * LANE)
        s = jnp.where(valid, s_scr[hh] * scale, MASK_VALUE)
        m = jnp.max(s, axis=-1, keepdims=True)
        p = jnp.exp(s - m)
        den = jnp.sum(p, axis=-1, keepdims=True)
        p_scr[hh] = (p * (1.0 / den)).astype(BF16)
        l_ref[:, sl] = jnp.broadcast_to(m + jnp.log(den), (qb, LANE))
    for hh in range(nh):
        sl = slice(hh * LANE, (hh + 1) * LANE)
        vcat = jnp.concatenate([vp_ref[:, sl], vc_ref[:, sl]], axis=0)
        o_ref[:, sl] = _dot(p_scr[hh], vcat)


def _attn_prompt(qd, kd, vd, window, dil, name):
    b, _, l, ow = qd.shape
    qb = window // dil
    nq = l // qb
    cur = pl.BlockSpec((None, None, qb, ow), lambda bi, r, i: (bi, r, i, 0))
    prev = pl.BlockSpec((None, None, qb, ow), lambda bi, r, i: (bi, r, jnp.maximum(i - 1, 0), 0))
    oshape = jax.ShapeDtypeStruct((b, dil, l, ow), F32)
    return pl.pallas_call(
        functools.partial(_attn_prompt_kernel, qb=qb, nh=ow // LANE, scale=LANE ** -0.5),
        out_shape=(oshape, oshape),
        grid=(b, dil, nq),
        in_specs=[cur, prev, cur, prev, cur],
        out_specs=(cur, cur),
        scratch_shapes=[pltpu.VMEM((ow // LANE, qb, 2 * qb), F32), pltpu.VMEM((ow // LANE, qb, 2 * qb), BF16)],
        compiler_params=_cparams(("parallel", "parallel", "arbitrary")),
        name=name,
    )(qd, kd, kd, vd, vd)


def _attn_sample_kernel(q_ref, kn_ref, vn_ref, kc_ref, vc_ref, o_ref, l_ref, *, ts, window, dil, scale):
    nk = window // dil
    o_ref[...] = jnp.zeros_like(o_ref)
    l_ref[...] = jnp.zeros_like(l_ref)
    arow = lax.broadcasted_iota(jnp.int32, (nk, 1, 1), 0)
    nrow = lax.broadcasted_iota(jnp.int32, (SROWS, 1, 1), 0)
    kn = kn_ref[...]
    vn = vn_ref[...]
    for i in range(ts):
        rho = (window + i) % dil
        base = (window + i - rho) // dil
        j0 = i // dil + 1
        a_lo, a_hi = max(base - nk, 0), min(base - j0, nk - 1)
        vcache = jnp.logical_and(arow >= a_lo, arow <= a_hi)
        new_rows = [i - j * dil for j in range(i // dil + 1)]
        vnew = functools.reduce(jnp.logical_or, [nrow == r for r in new_rows])
        q = q_ref[i][None]
        kc = kc_ref[:, rho]
        sc = jnp.where(vcache, jnp.sum(kc * q, axis=-1, keepdims=True) * scale, MASK_VALUE)
        sn = jnp.where(vnew, jnp.sum(kn * q, axis=-1, keepdims=True) * scale, MASK_VALUE)
        m = jnp.maximum(jnp.max(sc, axis=0, keepdims=True), jnp.max(sn, axis=0, keepdims=True))
        pc = jnp.where(vcache, jnp.exp(sc - m), 0.0)
        pn = jnp.where(vnew, jnp.exp(sn - m), 0.0)
        den = jnp.sum(pc, axis=0, keepdims=True) + jnp.sum(pn, axis=0, keepdims=True)
        acc = jnp.sum(pc * vc_ref[:, rho], axis=0) + jnp.sum(pn * vn, axis=0)
        o_ref[i] = acc / den[0]
        l_ref[i] = jnp.broadcast_to(m[0] + jnp.log(den[0]), acc.shape)


def _attn_sample(q3, k3, v3, ck_all, cv_all, layer, window, dil, bs, ts, name):
    hpg = q3.shape[1]
    nk = window // dil
    assert ck_all.shape[2] == window and window % dil == 0 and (dil == 1 or dil >= ts)
    nres = min(dil, ts)
    shape6 = ck_all.shape[:2] + (nk, dil, hpg, LANE)
    nspec = pl.BlockSpec((SROWS, hpg, LANE), lambda bi: (bi, 0, 0))
    cspec = pl.BlockSpec((None, None, nk, nres, hpg, LANE), lambda bi: (layer, bi, 0, 0, 0, 0))
    oshape = jax.ShapeDtypeStruct((bs * SROWS, hpg, LANE), F32)
    return pl.pallas_call(
        functools.partial(_attn_sample_kernel, ts=ts, window=window, dil=dil, scale=LANE ** -0.5),
        out_shape=(oshape, oshape),
        grid=(bs,),
        in_specs=[nspec, nspec, nspec, cspec, cspec],
        out_specs=(nspec, nspec),
        compiler_params=_cparams(("parallel",), 40 << 20),
        name=name,
    )(q3, k3, v3, ck_all.reshape(shape6), cv_all.reshape(shape6))


def _combine_kernel(*refs):
    ng = (len(refs) - 1) // 2
    os_, ls_, out = refs[:ng], refs[ng:2 * ng], refs[-1]
    out[...] = _mix_groups([o[...] for o in os_], [l[...] for l in ls_]).astype(out.dtype)


def _mix_groups(os_, ls):
    m = functools.reduce(jnp.maximum, ls)
    ws = [jnp.exp(l - m) for l in ls]
    den = functools.reduce(lambda a, b: a + b, ws)
    acc = functools.reduce(lambda a, b: a + b, [w * o for w, o in zip(ws, os_)])
    return acc / den


def _combine_prompt_kernel(*refs, dils):
    ng = len(dils)
    o_refs, l_refs, p_refs, out = refs[:ng], refs[ng:2 * ng], refs[2 * ng:3 * ng], refs[3 * ng]

    def token_order(ref, p_ref, d):
        x = jnp.concatenate([ref[0, r] for r in range(d)], axis=0)
        if d == 1:
            return x
        return sum(_dot(p_ref[...], piece) for piece in _split3(x))

    os_ = [token_order(o, p, d) for o, p, d in zip(o_refs, p_refs, dils)]
    ls = [token_order(l, p, d) for l, p, d in zip(l_refs, p_refs, dils)]
    out[...] = _mix_groups(os_, ls).astype(out.dtype)


def _combine_prompt(outs, lses, dils, name):
    b, _, _, ow = outs[0].shape
    t = outs[0].shape[1] * outs[0].shape[2]
    rows = _pick(t, (256, 128, 64, 32, 16))
    assert all(rows % (SROWS * d) == 0 for d in dils)
    nb = t // rows
    specs = [pl.BlockSpec((1, d, rows // d, ow), lambda bi, j: (bi, 0, j, 0)) for d in dils]
    perms = [jnp.asarray(_residue_perm(d, rows).T, BF16) for d in dils]
    pspec = pl.BlockSpec((rows, rows), lambda bi, j: (0, 0))
    return pl.pallas_call(
        functools.partial(_combine_prompt_kernel, dils=tuple(dils)),
        out_shape=jax.ShapeDtypeStruct((b * t, ow), BF16),
        grid=(b, nb),
        in_specs=specs + specs + [pspec] * len(dils),
        out_specs=pl.BlockSpec((rows, ow), lambda bi, j: (bi * nb + j, 0)),
        compiler_params=_cparams(("parallel", "parallel"), 40 << 20),
        name=name,
    )(*outs, *lses, *perms)


def _combine(outs, lses, name):
    n, ow = outs[0].shape
    tr = _pick(n, (512, 256, 128, 64, 8))
    spec = pl.BlockSpec((tr, ow), lambda i: (i, 0))
    return pl.pallas_call(
        _combine_kernel,
        out_shape=jax.ShapeDtypeStruct((n, ow), BF16),
        grid=(n // tr,),
        in_specs=[spec] * (2 * len(outs)),
        out_specs=spec,
        compiler_params=_cparams(("parallel",)),
        name=name,
    )(*outs, *lses)


def _merge_kernel(oa_ref, yb_ref, oc_ref, wa_ref, wb_ref, wc_ref, ga_ref, gb_ref, gc_ref, o_ref):
    acc = _sigmoid(ga_ref[...]) * _dot(oa_ref[...], wa_ref[...])
    acc = acc + _sigmoid(gb_ref[...]) * _dot(yb_ref[...], wb_ref[...])
    acc = acc + _sigmoid(gc_ref[...]) * _dot(oc_ref[...], wc_ref[...])
    o_ref[...] = acc.astype(o_ref.dtype)


def _merge(oa, yb, oc, wa, wb, wc, h, off_g, d, name):
    n = oa.shape[0]
    tm = _pick(n, (384, 256, 128, 88, 64, 8))
    tn = _pick(d, (512, 256, 128), off_g)
    ka, kb, kc = oa.shape[1], yb.shape[1], oc.shape[1]

    def gspec(k):
        return pl.BlockSpec((tm, tn), lambda j, i: (i, (off_g + k * d) // tn + j))

    return pl.pallas_call(
        _merge_kernel,
        out_shape=jax.ShapeDtypeStruct((n, d), BF16),
        grid=(d // tn, n // tm),
        in_specs=[pl.BlockSpec((tm, ka), lambda j, i: (i, 0)),
                  pl.BlockSpec((tm, kb), lambda j, i: (i, 0)),
                  pl.BlockSpec((tm, kc), lambda j, i: (i, 0)),
                  pl.BlockSpec((ka, tn), lambda j, i: (0, j)),
                  pl.BlockSpec((kb, tn), lambda j, i: (0, j)),
                  pl.BlockSpec((kc, tn), lambda j, i: (0, j)),
                  gspec(0), gspec(1), gspec(2)],
        out_specs=pl.BlockSpec((tm, tn), lambda j, i: (i, j)),
        compiler_params=_cparams(("parallel", "parallel"), 48 << 20),
        name=name,
    )(oa, yb, oc, wa, wb, wc, h, h, h)


def _layer_norm(y, g, b):
    mu = jnp.mean(y, axis=-1, keepdims=True)
    yc = y - mu
    var = jnp.mean(yc * yc, axis=-1, keepdims=True)
    return yc * lax.rsqrt(var + EPS) * g + b


def _outln_kernel(m_ref, w_ref, x_ref, g_ref, b_ref, o_ref, ob_ref, *, alpha):
    y = alpha * x_ref[...] + _dot(m_ref[...], w_ref[...])
    out = _layer_norm(y, g_ref[...], b_ref[...])
    o_ref[...] = out
    ob_ref[...] = out.astype(ob_ref.dtype)


def _outln(merged, w_out, x, g, b, alpha, name):
    n, d = x.shape
    tm = _pick(n, (384, 256, 128, 88, 64, 8))
    rspec = pl.BlockSpec((tm, d), lambda i: (i, 0))
    vspec = pl.BlockSpec((1, d), lambda i: (0, 0))
    return pl.pallas_call(
        functools.partial(_outln_kernel, alpha=alpha),
        out_shape=(jax.ShapeDtypeStruct((n, d), F32), jax.ShapeDtypeStruct((n, d), BF16)),
        grid=(n // tm,),
        in_specs=[rspec, pl.BlockSpec((d, d), lambda i: (0, 0)), rspec, vspec, vspec],
        out_specs=(rspec, rspec),
        compiler_params=_cparams(("parallel",), 48 << 20),
        name=name,
    )(merged, w_out, x, g, b)


def _resln_kernel(x_ref, f_ref, g_ref, b_ref, o_ref, ob_ref, *, alpha):
    out = _layer_norm(alpha * x_ref[...] + f_ref[...], g_ref[...], b_ref[...])
    o_ref[...] = out
    ob_ref[...] = out.astype(ob_ref.dtype)


def _resln(x, f, g, b, alpha, name):
    n, d = x.shape
    tm = _pick(n, (384, 256, 128, 88, 64, 8))
    rspec = pl.BlockSpec((tm, d), lambda i: (i, 0))
    vspec = pl.BlockSpec((1, d), lambda i: (0, 0))
    return pl.pallas_call(
        functools.partial(_resln_kernel, alpha=alpha),
        out_shape=(jax.ShapeDtypeStruct((n, d), F32), jax.ShapeDtypeStruct((n, d), BF16)),
        grid=(n // tm,),
        in_specs=[rspec, rspec, vspec, vspec],
        out_specs=(rspec, rspec),
        compiler_params=_cparams(("parallel",)),
        name=name,
    )(x, f, g, b)


def _ffn_a_kernel(te_ref, x_ref, w1_ref, w3_ref, o_ref):
    used = pl.program_id(1) < te_ref[pl.num_programs(1)]

    @pl.when(used)
    def _():
        x = x_ref[...].astype(BF16)
        a = _dot(x, w1_ref[0])
        o_ref[...] = (_silu(a) * _dot(x, w3_ref[0])).astype(o_ref.dtype)

    @pl.when(jnp.logical_not(used))
    def _():
        o_ref[...] = jnp.zeros_like(o_ref)


def _ffn_a(x, w1, w3, tile_expert, tm, name):
    n, d = x.shape
    f = w1.shape[2]
    tf = _pick(f, (1408, 1024, 512, 256, 128))
    xbytes = x.dtype.itemsize
    vm = 2 * (tm * d * xbytes + 2 * d * tf * 2 + tm * tf * 2) + 3 * tm * tf * 4 + (6 << 20)
    return pl.pallas_call(
        _ffn_a_kernel,
        out_shape=jax.ShapeDtypeStruct((n, f), BF16),
        grid_spec=pltpu.PrefetchScalarGridSpec(
            num_scalar_prefetch=1,
            grid=(f // tf, n // tm),
            in_specs=[pl.BlockSpec((tm, d), lambda j, i, te: (i, 0)),
                      pl.BlockSpec((1, d, tf), lambda j, i, te: (te[i], 0, j)),
                      pl.BlockSpec((1, d, tf), lambda j, i, te: (te[i], 0, j))],
            out_specs=pl.BlockSpec((tm, tf), lambda j, i, te: (i, j))),
        compiler_params=_cparams(("parallel", "arbitrary"), vm),
        name=name,
    )(tile_expert, x, w1, w3)


def _ffn_b_kernel(te_ref, h_ref, w2_ref, o_ref):
    used = pl.program_id(1) < te_ref[pl.num_programs(1)]

    @pl.when(used)
    def _():
        o_ref[...] = _dot(h_ref[...], w2_ref[0])

    @pl.when(jnp.logical_not(used))
    def _():
        o_ref[...] = jnp.zeros_like(o_ref)


def _ffn_b(hmid, w2, tile_expert, tm, name):
    n, f = hmid.shape
    d = w2.shape[2]
    tn = _pick(d, (1024, 512, 256, 128))
    vm = 2 * (tm * f * 2 + f * tn * 2 + tm * tn * 4) + (4 << 20)
    return pl.pallas_call(
        _ffn_b_kernel,
        out_shape=jax.ShapeDtypeStruct((n, d), F32),
        grid_spec=pltpu.PrefetchScalarGridSpec(
            num_scalar_prefetch=1,
            grid=(d // tn, n // tm),
            in_specs=[pl.BlockSpec((tm, f), lambda j, i, te: (i, 0)),
                      pl.BlockSpec((1, f, tn), lambda j, i, te: (te[i], 0, j))],
            out_specs=pl.BlockSpec((tm, tn), lambda j, i, te: (i, j))),
        compiler_params=_cparams(("parallel", "arbitrary"), vm),
        name=name,
    )(tile_expert, hmid, w2)


def _router_kernel(x_ref, w_ref, b_ref, e_ref, g_ref, *, n_exp):
    xs = _split3(x_ref[...])
    ws = _split3(w_ref[...])
    logits = b_ref[...]
    for i in range(3):
        for j in range(3 - i):
            logits = logits + _dot(xs[i], ws[j])
    lane = lax.broadcasted_iota(jnp.int32, logits.shape, 1)
    neg = jnp.float32(-jnp.inf)
    logits = jnp.where(lane < n_exp, logits, neg)
    m0 = jnp.max(logits, axis=-1, keepdims=True)
    i0 = jnp.min(jnp.where(logits == m0, lane, LANE), axis=-1, keepdims=True)
    rest = jnp.where(lane == i0, neg, logits)
    m1 = jnp.max(rest, axis=-1, keepdims=True)
    i1 = jnp.min(jnp.where(rest == m1, lane, LANE), axis=-1, keepdims=True)
    e1 = jnp.exp(m1 - m0)
    den = 1.0 + e1
    e_ref[...] = jnp.where(lane == 0, i0, jnp.where(lane == 1, i1, 0))
    g_ref[...] = jnp.where(lane == 0, 1.0 / den, jnp.where(lane == 1, e1 / den, 0.0))


def _router(x, w_router, b_router, name):
    n, d = x.shape
    n_exp = w_router.shape[1]
    wp = jnp.zeros((d, LANE), F32).at[:, :n_exp].set(w_router.astype(F32))
    bp = jnp.zeros((1, LANE), F32).at[0, :n_exp].set(b_router.astype(F32))
    tm = _pick(n, (384, 256, 128, 88, 64, 8))
    rspec = pl.BlockSpec((tm, LANE), lambda i: (i, 0))
    return pl.pallas_call(
        functools.partial(_router_kernel, n_exp=n_exp),
        out_shape=(jax.ShapeDtypeStruct((n, LANE), jnp.int32), jax.ShapeDtypeStruct((n, LANE), F32)),
        grid=(n // tm,),
        in_specs=[pl.BlockSpec((tm, d), lambda i: (i, 0)),
                  pl.BlockSpec((d, LANE), lambda i: (0, 0)),
                  pl.BlockSpec((1, LANE), lambda i: (0, 0))],
        out_specs=(rspec, rspec),
        compiler_params=_cparams(("parallel",)),
        name=name,
    )(x, wp, bp)


def _row_copy(src_hbm, row, dst_ref, dst_row, sem):
    return pltpu.make_async_copy(src_hbm.at[pl.ds(row, 1)], dst_ref.at[pl.ds(dst_row, 1)], sem)


def _gather_kernel(idx_ref, x_hbm, o_ref, sem, *, tm):
    base = pl.program_id(0) * tm

    def start(r, carry):
        _row_copy(x_hbm, idx_ref[base + r], o_ref, r, sem).start()
        return carry

    def wait(r, carry):
        _row_copy(x_hbm, 0, o_ref, r, sem).wait()
        return carry

    lax.fori_loop(0, tm, start, 0, unroll=8)
    lax.fori_loop(0, tm, wait, 0)


def _gather_rows(x, idx, tm, name):
    n_out = idx.shape[0]
    d = x.shape[1]
    return pl.pallas_call(
        functools.partial(_gather_kernel, tm=tm),
        out_shape=jax.ShapeDtypeStruct((n_out, d), x.dtype),
        grid_spec=pltpu.PrefetchScalarGridSpec(
            num_scalar_prefetch=1,
            grid=(n_out // tm,),
            in_specs=[pl.BlockSpec(memory_space=pl.ANY)],
            out_specs=pl.BlockSpec((tm, d), lambda i, idx_ref: (i, 0)),
            scratch_shapes=[pltpu.SemaphoreType.DMA(())]),
        compiler_params=_cparams(("arbitrary",)),
        name=name,
    )(idx, x)


def _moe_out_kernel(pos_ref, y_hbm, x_ref, gate_ref, g_ref, b_ref, o_ref, ob_ref, buf, sem, *, tm, alpha):
    base = pl.program_id(0) * tm

    def start(r, carry):
        for k in range(2):
            _row_copy(y_hbm, pos_ref[2 * (base + r) + k], buf.at[k], r, sem).start()
        return carry

    def wait(r, carry):
        for k in range(2):
            _row_copy(y_hbm, 0, buf.at[k], r, sem).wait()
        return carry

    lax.fori_loop(0, tm, start, 0, unroll=8)
    lax.fori_loop(0, tm, wait, 0)
    gt = gate_ref[...]
    f = buf[0] * gt[:, 0:1] + buf[1] * gt[:, 1:2]
    out = _layer_norm(alpha * x_ref[...] + f, g_ref[...], b_ref[...])
    o_ref[...] = out
    ob_ref[...] = out.astype(ob_ref.dtype)


def _moe_out(ys, pos, x, gates, g, b, alpha, name):
    n, d = x.shape
    tm = _pick(n, (256, 128, 88, 64, 8))
    rspec = pl.BlockSpec((tm, d), lambda i, p: (i, 0))
    vspec = pl.BlockSpec((1, d), lambda i, p: (0, 0))
    return pl.pallas_call(
        functools.partial(_moe_out_kernel, tm=tm, alpha=alpha),
        out_shape=(jax.ShapeDtypeStruct((n, d), F32), jax.ShapeDtypeStruct((n, d), BF16)),
        grid_spec=pltpu.PrefetchScalarGridSpec(
            num_scalar_prefetch=1,
            grid=(n // tm,),
            in_specs=[pl.BlockSpec(memory_space=pl.ANY), rspec,
                      pl.BlockSpec((tm, LANE), lambda i, p: (i, 0)), vspec, vspec],
            out_specs=(rspec, rspec),
            scratch_shapes=[pltpu.VMEM((2, tm, d), F32), pltpu.SemaphoreType.DMA(())]),
        compiler_params=_cparams(("arbitrary",)),
        name=name,
    )(pos, ys, x, gates, g, b)


def _moe_ffn(x, w_router, b_router, w1, w3, w2, g, b, alpha, tag):
    n, d = x.shape
    n_exp = w1.shape[0]
    tm = MOE_TILE
    e_pad, gates = _router(x, w_router, b_router, f"router_{tag}")
    e_flat = e_pad[:, :2].reshape(-1)
    n_slots = 2 * n
    onehot = (e_flat[:, None] == jnp.arange(n_exp, dtype=jnp.int32)[None, :]).astype(jnp.int32)
    csum = jnp.cumsum(onehot, axis=0)
    rank = jnp.take_along_axis(csum, e_flat[:, None], axis=1)[:, 0] - 1
    counts = csum[-1]
    padded = (counts + tm - 1) // tm * tm
    pad_end = jnp.cumsum(padded)
    pos = (pad_end - padded)[e_flat] + rank
    n_tiles = -(-(n_slots + n_exp * (tm - 1)) // tm)
    slot_tok = jnp.zeros((n_tiles * tm,), jnp.int32).at[pos].set(jnp.arange(n_slots, dtype=jnp.int32) // 2)
    tile_expert = jnp.minimum(
        jnp.searchsorted(pad_end, jnp.arange(n_tiles, dtype=jnp.int32) * tm, side="right"), n_exp - 1
    ).astype(jnp.int32)
    tile_expert = jnp.concatenate([tile_expert, (pad_end[-1:] // tm).astype(jnp.int32)])
    xs = _gather_rows(x, slot_tok, tm, f"moe_gather_{tag}")
    hmid = _ffn_a(xs, w1, w3, tile_expert, tm, f"moe_a_{tag}")
    ys = _ffn_b(hmid, w2, tile_expert, tm, f"moe_b_{tag}")
    return _moe_out(ys, pos.astype(jnp.int32), x, gates, g, b, alpha, f"moe_out_{tag}")


def kernel(x_prompt, x_sample, state_hgrn, state_ssm, state_conv, cache_k0, cache_v0, cache_k1, cache_v1,
           cache_k2, cache_v2, w_in, hg_lb, hg_norm_w, conv_w, conv_b, dt_bias, a_log, d_skip, ssm_norm_w,
           w_proj_a, w_proj_b, w_proj_c, w_out, ln1_g, ln1_b, ln2_g, ln2_b, ffn_w1, ffn_w3, ffn_w2,
           moe_router, moe_router_b, moe_w1, moe_w3, moe_w2):
    caches = ((cache_k0, cache_v0), (cache_k1, cache_v1), (cache_k2, cache_v2))
    depth = w_in.shape[0]
    b, t, d = x_prompt.shape
    bs, ts, _ = x_sample.shape
    _, _, hg_heads, hg_dk, hg_dv = state_hgrn.shape
    _, _, m_heads, pdim, dstate = state_ssm.shape
    conv_dim = state_conv.shape[-1]
    hpg, adh = cache_k0.shape[-2:]
    assert hg_dk == LANE and hg_dv == LANE and dstate == LANE and adh == LANE and ts <= SROWS
    hg_dim = hg_heads * LANE
    dinner = m_heads * pdim
    groups = (conv_dim - dinner) // (2 * dstate)
    m_hpg = m_heads // groups
    gw = m_hpg * pdim
    adim = len(A_GROUPS) * hpg * LANE
    assert gw % LANE == 0 and LANE % pdim == 0 and t % HG_CHUNK == 0 and t % SSD_CHUNK == 0
    ow = hpg * LANE
    alpha = (2 * depth) ** 0.25
    bt = b * t
    n = bt + bs * SROWS

    sizes = (("hq", hg_dim), ("hf", hg_dim), ("hi", hg_dim), ("hg", hg_dim), ("mz", dinner), ("xbc", conv_dim),
             ("mdt", m_heads), ("aq", adim), ("ak", adim), ("av", adim), ("gate", 3 * d))
    src, acc = {}, 0
    for name, sz in sizes:
        src[name] = acc
        acc += sz
    assert acc == w_in.shape[2]
    order = [k for k, _ in sizes if k != "mdt"] + ["mdt"]
    offs, acc = {}, 0
    for name in order:
        offs[name] = acc
        acc += dict(sizes)[name]
    mdt_pad = (-acc) % LANE
    assert offs["mz"] % gw == 0 and all(offs[k] % ow == 0 for k in ("aq", "ak", "av"))

    def pack_w_in(w):
        parts = [w[:, src[k]:src[k] + dict(sizes)[k]] for k in order]
        parts.append(jnp.zeros((d, mdt_pad), w.dtype))
        return jnp.concatenate(parts, axis=1).astype(BF16)

    xs_pad = jnp.zeros((bs, SROWS, d), F32).at[:, :ts].set(x_sample)
    x = jnp.concatenate([x_prompt.reshape(bt, d), xs_pad.reshape(bs * SROWS, d)], axis=0)
    xb = x.astype(BF16)

    state_ssm_r = state_ssm.reshape(depth, bs, m_heads * pdim, dstate)
    lb_sm = jax.nn.softmax(hg_lb.astype(F32), axis=0)
    lb_all = jnp.cumsum(lb_sm, axis=0) - lb_sm[0]

    out_hg_p, out_ssm_p, out_conv_p, out_hg_s, out_ssm_s, out_conv_s = [], [], [], [], [], []
    out_kv_p = [[] for _ in range(2 * len(A_GROUPS))]
    out_kv_s = [[] for _ in range(2 * len(A_GROUPS))]

    for l in range(depth):
        h = _matmul(xb, pack_w_in(w_in[l]), F32, f"in_proj_{l}")

        lb = lb_all[l]
        lbc = jnp.stack([jnp.log(jnp.maximum(lb, LB_FLOOR)), jnp.log1p(-lb), 1.0 - lb])
        nw_a = hg_norm_w[l][None, :]
        oa_p, hg_p = _hgrn_prompt(h, offs, lbc, nw_a, b, t, hg_heads, f"hgrn_prompt_{l}")
        oa_s, hg_s = _hgrn_sample(h, offs, lbc, nw_a, state_hgrn, l, bt, bs, ts, hg_heads, f"hgrn_sample_{l}")
        o_a = jnp.concatenate([oa_p, oa_s], axis=0)

        cw, cbias = conv_w[l], conv_b[l][None, :]
        prev_s = jnp.zeros((bs, SROWS, conv_dim), F32).at[:, SROWS - (CONV_WIDTH - 1):].set(state_conv[l])
        xa_s = _conv_sample(h, offs["xbc"], conv_dim, cw, cbias, bt, prev_s.reshape(bs * SROWS, conv_dim),
                            f"conv_sample_{l}")
        mdt = h[:, offs["mdt"]:offs["mdt"] + m_heads]
        a_neg = -jnp.exp(a_log[l].astype(F32))
        prm = jnp.stack([dt_bias[l].astype(F32), a_neg, d_skip[l].astype(F32)])
        prm_r = prm.reshape(3, groups, m_hpg).transpose(1, 0, 2)
        mdt_g = mdt[:bt].reshape(bt, groups, m_hpg).transpose(1, 0, 2)
        nw_b = ssm_norm_w[l][None, :]
        yb_p, ssm_p = _ssd_prompt(h, offs["xbc"], offs["mz"], cw, cbias, mdt_g, mdt_g.transpose(0, 2, 1), prm_r,
                                  prm_r.transpose(0, 2, 1), nw_b, b, t, groups, m_hpg, pdim, dstate,
                                  f"ssd_prompt_{l}")
        yb_s, ssm_s = _ssd_sample(xa_s, h, offs["mz"], jnp.repeat(mdt[bt:], pdim, axis=1),
                                  jnp.repeat(prm, pdim, axis=1), nw_b,
                                  state_ssm_r, l, bt, bs, ts, groups, gw, dstate, f"ssd_sample_{l}")
        y_b = jnp.concatenate([yb_p, yb_s], axis=0)

        q_s, k_s = _rope_sample(h, offs, adim, bt, bs * SROWS, ts, f"rope_sample_{l}")
        v_s = h[bt:, offs["av"]:offs["av"] + adim]
        outs_p, lses_p, outs_s, lses_s, k_std = [], [], [], [], []
        for gi, (window, dil) in enumerate(A_GROUPS):
            qd, kd, vd, ks = _rope_prompt(h, offs, gi, dil, b, t, hpg, f"rope_prompt_{l}_{gi}")
            k_std.append(ks)
            o_g, l_g = _attn_prompt(qd, kd, vd, window, dil, f"attn_prompt_{l}_{gi}")
            outs_p.append(o_g)
            lses_p.append(l_g)
            g3 = lambda a: a[:, gi * ow:(gi + 1) * ow].reshape(bs * SROWS, hpg, LANE)
            o_g, l_g = _attn_sample(g3(q_s), g3(k_s), g3(v_s), caches[gi][0], caches[gi][1], l, window, dil,
                                    bs, ts, f"attn_sample_{l}_{gi}")
            outs_s.append(o_g.reshape(bs * SROWS, ow))
            lses_s.append(l_g.reshape(bs * SROWS, ow))
        o_c = jnp.concatenate([_combine_prompt(outs_p, lses_p, [dl for _, dl in A_GROUPS], f"attn_mix_prompt_{l}"),
                               _combine(outs_s, lses_s, f"attn_mix_sample_{l}")], axis=0)

        merged = _merge(o_a, y_b, o_c, w_proj_a[l].astype(BF16), w_proj_b[l].astype(BF16),
                        w_proj_c[l].astype(BF16), h, offs["gate"], d, f"merge_{l}")
        x, xb = _outln(merged, w_out[l].astype(BF16), x, ln1_g[l][None, :], ln1_b[l][None, :], alpha, f"out_ln1_{l}")

        if l % 2 == 0:
            i = l // 2
            tm = _pick(n, (768, 512, 384, 256, 128, 88, 64, 8))
            te = jnp.zeros((n // tm + 1,), jnp.int32).at[-1].set(n // tm)
            hmid = _ffn_a(xb, ffn_w1[i:i + 1].astype(BF16), ffn_w3[i:i + 1].astype(BF16), te, tm, f"ffn_a_{l}")
            f = _ffn_b(hmid, ffn_w2[i:i + 1].astype(BF16), te, tm, f"ffn_b_{l}")
            x, xb = _resln(x, f, ln2_g[l][None, :], ln2_b[l][None, :], alpha, f"ln2_{l}")
        else:
            i = l // 2
            x, xb = _moe_ffn(x, moe_router[i], moe_router_b[i], moe_w1[i].astype(BF16), moe_w3[i].astype(BF16),
                             moe_w2[i].astype(BF16), ln2_g[l][None, :], ln2_b[l][None, :], alpha, str(l))

        keep_c = CONV_WIDTH - 1
        assert t >= keep_c

        def tail_rows(c0, c1, keep):
            return jnp.stack([h[(bi + 1) * t - keep:(bi + 1) * t, c0:c1] for bi in range(b)])

        out_conv_p.append(tail_rows(offs["xbc"], offs["xbc"] + conv_dim, keep_c))
        xbc_s = h[bt:, offs["xbc"]:offs["xbc"] + conv_dim].reshape(bs, SROWS, conv_dim)[:, :ts]
        out_conv_s.append(jnp.concatenate([state_conv[l], xbc_s], axis=1)[:, -keep_c:])
        out_hg_p.append(hg_p)
        out_hg_s.append(hg_s)
        out_ssm_p.append(ssm_p.reshape(b, m_heads, pdim, dstate))
        out_ssm_s.append(ssm_s.reshape(bs, m_heads, pdim, dstate))
        ksm = k_s.reshape(bs, SROWS, adim // LANE, LANE)[:, :ts]
        vsm = v_s.reshape(bs, SROWS, adim // LANE, LANE)[:, :ts]
        for gi, (window, _) in enumerate(A_GROUPS):
            hs = slice(gi * hpg, (gi + 1) * hpg)
            keep = min(window, t)
            v0 = offs["av"] + gi * ow
            out_kv_p[2 * gi].append(k_std[gi].reshape(b, t, hpg, LANE)[:, t - keep:])
            out_kv_p[2 * gi + 1].append(tail_rows(v0, v0 + ow, keep).reshape(b, keep, hpg, LANE))
            out_kv_s[2 * gi].append(ksm[:, :, hs])
            out_kv_s[2 * gi + 1].append(vsm[:, :, hs])

    y_prompt = x[:bt].reshape(b, t, d)
    y_sample = x[bt:].reshape(bs, SROWS, d)[:, :ts]
    return (y_prompt, y_sample, jnp.stack(out_hg_p), jnp.stack(out_ssm_p), jnp.stack(out_conv_p),
            *[jnp.stack(o) for o in out_kv_p],
            jnp.stack(out_hg_s), jnp.stack(out_ssm_s), jnp.stack(out_conv_s),
            *[jnp.stack(o) for o in out_kv_s])
```

```python
import functools
import math

import numpy as np
import jax
import jax.numpy as jnp
from jax import lax
from jax.experimental import pallas as pl
from jax.experimental.pallas import tpu as pltpu

F32 = jnp.float32
BF16 = jnp.bfloat16

LANE = 128
SROWS = 8
VMEM_CAP = 56 * 1024 * 1024

A_GROUPS = ((128, 1), (512, 4), (2048, 16))
PAST_LEN = 8192
ROPE_THETA = 10000.0
MASK_VALUE = -1e30
LB_FLOOR = 1e-30
EPS = 1e-5
LOG2E = 1.4426950408889634
CONV_WIDTH = 4
HG_CHUNK = 128
SSD_CHUNK = 128
MOE_TILE = 256


def _cparams(sem, vmem_bytes=None):
    kw = dict(dimension_semantics=sem)
    if vmem_bytes is not None:
        kw["vmem_limit_bytes"] = int(min(max(vmem_bytes, 16 * 1024 * 1024), VMEM_CAP))
    return pltpu.CompilerParams(**kw)


def _pick(n, cands, *offsets):
    for c in cands:
        if n % c == 0 and all(o % c == 0 for o in offsets):
            return c
    raise ValueError(f"no tile for {n} in {cands} (offsets {offsets})")


def _sigmoid(x):
    return 1.0 / (1.0 + jnp.exp(-x))


def _silu(x):
    return x * _sigmoid(x)


def _softplus(x):
    return jnp.maximum(x, 0.0) + jnp.log1p(jnp.exp(-jnp.abs(x)))


def _dot(a, b):
    return jnp.dot(a, b, preferred_element_type=F32)


def _dot_nt(a, b):
    return lax.dot_general(a, b, (((1,), (1,)), ((), ())), preferred_element_type=F32)


def _dot_tn(a, b):
    return lax.dot_general(a, b, (((0,), (0,)), ((), ())), preferred_element_type=F32)


def _split2(x):
    hi = x.astype(BF16)
    lo = (x - hi.astype(F32)).astype(BF16)
    return hi, lo


def _split3(x):
    p1 = x.astype(BF16)
    r1 = x - p1.astype(F32)
    p2 = r1.astype(BF16)
    p3 = (r1 - p2.astype(F32)).astype(BF16)
    return p1, p2, p3


def _mm_wcast_kernel(x_ref, w_ref, *rest, shift):
    o_ref, wb_scr = rest[-2:]

    @pl.when(pl.program_id(1) == 0)
    def _():
        w = w_ref[...]
        if shift:
            w = jnp.concatenate([w[shift:], rest[0][:shift]], axis=0)
        wb_scr[...] = w.astype(BF16)

    o_ref[...] = _dot_nt(x_ref[...], wb_scr[...]).astype(o_ref.dtype)


def _matmul_wcast(x, wt_all, layer, col0, ncols, shift, out_dtype, name):
    m, k = x.shape
    tm = _pick(m, (768, 512, 384, 256, 128, 88, 64, 8))
    tn = _pick(ncols, (1024, 512, 256, 128), col0)
    assert shift % SROWS == 0 and shift < LANE
    vm = 2 * (tm * k * 2 + k * (tn + LANE) * 4 + tm * tn * 4) + 2 * k * tn * 4 + tm * tn * 4 + (4 << 20)
    in_specs = [pl.BlockSpec((tm, k), lambda j, i: (i, 0)),
                pl.BlockSpec((None, tn, k), lambda j, i: (layer, col0 // tn + j, 0))]
    operands = [x, wt_all]
    if shift:
        in_specs.append(pl.BlockSpec((None, LANE, k), lambda j, i: (layer, (col0 + (j + 1) * tn) // LANE, 0)))
        operands.append(wt_all)
    return pl.pallas_call(
        functools.partial(_mm_wcast_kernel, shift=shift),
        out_shape=jax.ShapeDtypeStruct((m, ncols), out_dtype),
        grid=(ncols // tn, m // tm),
        in_specs=in_specs,
        out_specs=pl.BlockSpec((tm, tn), lambda j, i: (i, j)),
        scratch_shapes=[pltpu.VMEM((tn, k), BF16)],
        compiler_params=_cparams(("parallel", "arbitrary"), vm),
        name=name,
    )(*operands)


def _conv_tile(u_ref, carry, w_ref, b_ref, rows):
    u = u_ref[...]
    full = jnp.concatenate([carry[...], u], axis=0)
    acc = b_ref[...] + u * w_ref[CONV_WIDTH - 1:CONV_WIDTH, :]
    for i in range(CONV_WIDTH - 1):
        off = SROWS - (CONV_WIDTH - 1) + i
        acc = acc + full[off:off + rows] * w_ref[i:i + 1, :]
    return u, _silu(acc)


def _conv_sample_kernel(u_ref, prev_ref, w_ref, b_ref, o_ref):
    o_ref[...] = _conv_tile(u_ref, prev_ref, w_ref, b_ref, SROWS)[1]


def _conv_sample(h, off_xbc, conv_dim, w, b, row0, prev_arr, name):
    nrows = prev_arr.shape[0]
    tc = _pick(conv_dim, (2048, 1536, 1024, 512, 256, 128), off_xbc)
    cb0 = off_xbc // tc
    rb0 = row0 // SROWS
    return pl.pallas_call(
        _conv_sample_kernel,
        out_shape=jax.ShapeDtypeStruct((nrows, conv_dim), F32),
        grid=(nrows // SROWS, conv_dim // tc),
        in_specs=[pl.BlockSpec((SROWS, tc), lambda i, j: (rb0 + i, cb0 + j)),
                  pl.BlockSpec((SROWS, tc), lambda i, j: (i, j)),
                  pl.BlockSpec((CONV_WIDTH, tc), lambda i, j: (0, j)),
                  pl.BlockSpec((1, tc), lambda i, j: (0, j))],
        out_specs=pl.BlockSpec((SROWS, tc), lambda i, j: (i, j)),
        compiler_params=_cparams(("parallel", "parallel")),
        name=name,
    )(h, prev_arr, w, b)


def _hgrn_consts(c):
    t = np.arange(c)[:, None]
    u = np.arange(c)[None, :]
    mats, masks = [], [np.eye(c)]
    h = 1
    while h < c:
        tb, ub = t // h, u // h
        mats.append(((tb % 2 == 1) & (ub == tb) & (u <= t)) | ((tb % 2 == 0) & (ub == tb) & (u > t)))
        masks.append((tb % 2 == 1) & (ub == tb - 1))
        h *= 2
    mats.append(u <= t)
    mall = np.concatenate([m.astype(np.float32) for m in mats], axis=0)
    mall = np.concatenate([mall, mall], axis=1)
    return jnp.asarray(mall, BF16), jnp.asarray(np.stack(masks).astype(np.float32))


def _hgrn_gates(fx, hq, lbc):
    la, l1, oml = lbc[0:1], lbc[1:2], lbc[2:3]
    ls = jnp.minimum(fx, 0.0) - jnp.log1p(jnp.exp(-jnp.abs(fx)))
    ct = l1 + ls
    g = jnp.maximum(la, ct) + jnp.log1p(jnp.exp(-jnp.abs(la - ct)))
    kk = oml * (1.0 / (1.0 + jnp.exp(fx)))
    return g, kk, _silu(hq)


def _hgrn_out(o, nw, hg):
    ms = jnp.mean(o * o, axis=-1, keepdims=True)
    return (o * lax.rsqrt(ms + EPS) * nw) * _silu(hg)


def _hgrn_prompt_kernel(hq_ref, hf_ref, hi_ref, hg_ref, lbc_ref, nw_ref, mall_ref, mask_ref,
                        o_ref, s_ref, st_scr, *, hb, c, nlev):
    ci = pl.program_id(2)

    @pl.when(ci == 0)
    def _():
        st_scr[...] = jnp.zeros_like(st_scr)

    gates = {}
    ex_pair = None
    for hh in range(hb):
        sl = slice(hh * LANE, (hh + 1) * LANE)
        if hh % 2 == 0:
            pair = [h2 for h2 in (hh, hh + 1) if h2 < hb]
            for h2 in pair:
                s2 = slice(h2 * LANE, (h2 + 1) * LANE)
                gates[h2] = _hgrn_gates(hf_ref[:, s2], hq_ref[:, s2], lbc_ref[:, s2])
            halves = [_split2(gates[h2][0] * LOG2E) for h2 in pair]
            gmat = jnp.concatenate([jnp.concatenate([hl[0] for hl in halves], axis=1),
                                    jnp.concatenate([hl[1] for hl in halves], axis=1)], axis=0)
            ex_pair = _dot(mall_ref[...], gmat)
        _, kk, q = gates.pop(hh)
        v = hi_ref[:, sl]
        vb = v.astype(BF16)
        ex = ex_pair[:, (hh % 2) * LANE:(hh % 2 + 1) * LANE]
        a = _dot_nt(q.astype(BF16), kk.astype(BF16)) * mask_ref[0]
        for lv in range(nlev):
            w = jnp.exp2(ex[lv * c:(lv + 1) * c])
            a = a + _dot_nt((q * w).astype(BF16), (kk * w).astype(BF16)) * mask_ref[lv + 1]
        bcum = ex[nlev * c:(nlev + 1) * c]
        erev = bcum[c - 1:c, :] - bcum
        st = st_scr[hh]
        o = _dot(a.astype(BF16), vb) + _dot_nt((q * jnp.exp2(bcum)).astype(BF16), st.astype(BF16))
        st_new = st * jnp.exp2(bcum[c - 1:c, :]) + _dot_tn(vb, (kk * jnp.exp2(erev)).astype(BF16))
        st_scr[hh] = st_new
        o_ref[:, sl] = _hgrn_out(o, nw_ref[:, sl], hg_ref[:, sl]).astype(o_ref.dtype)

    @pl.when(ci == pl.num_programs(2) - 1)
    def _():
        for hh in range(hb):
            s_ref[0, hh] = st_scr[hh].T


def _hgrn_prompt(h, offs, lbc, nw, b, t, heads, name):
    c = HG_CHUNK
    hb = _pick(heads * LANE, (8 * LANE, 4 * LANE, 2 * LANE, LANE),
               *(offs[k] for k in ("hq", "hf", "hi", "hg"))) // LANE
    w = hb * LANE
    nlev = int(math.log2(c))
    mall, masks = _hgrn_consts(c)
    nc = t // c

    def hspec(off):
        return pl.BlockSpec((c, w), lambda bi, hi, ci: (bi * nc + ci, off // w + hi))

    return pl.pallas_call(
        functools.partial(_hgrn_prompt_kernel, hb=hb, c=c, nlev=nlev),
        out_shape=(jax.ShapeDtypeStruct((b * t, heads * LANE), BF16),
                   jax.ShapeDtypeStruct((b, heads, LANE, LANE), F32)),
        grid=(b, heads // hb, nc),
        in_specs=[hspec(offs["hq"]), hspec(offs["hf"]), hspec(offs["hi"]), hspec(offs["hg"]),
                  pl.BlockSpec((3, w), lambda bi, hi, ci: (0, hi)),
                  pl.BlockSpec((1, w), lambda bi, hi, ci: (0, hi)),
                  pl.BlockSpec(mall.shape, lambda bi, hi, ci: (0, 0)),
                  pl.BlockSpec(masks.shape, lambda bi, hi, ci: (0, 0, 0))],
        out_specs=(pl.BlockSpec((c, w), lambda bi, hi, ci: (bi * nc + ci, hi)),
                   pl.BlockSpec((1, hb, LANE, LANE), lambda bi, hi, ci: (bi, hi, 0, 0))),
        scratch_shapes=[pltpu.VMEM((hb, LANE, LANE), F32)],
        compiler_params=_cparams(("parallel", "parallel", "arbitrary"), 40 << 20),
        name=name,
    )(h, h, h, h, lbc, nw, mall, masks)


def _row_select(rows_list, n):
    width = rows_list[0].shape[1]
    ridx = lax.broadcasted_iota(jnp.int32, (n, width), 0)
    out = jnp.zeros((n, width), F32)
    for i, r in enumerate(rows_list):
        out = jnp.where(ridx == i, r, out)
    return out


def _hgrn_sample_kernel(hq_ref, hf_ref, hi_ref, hg_ref, lbc_ref, nw_ref, s0_ref, o_ref, s_ref, *, hb, ts):
    ridx = lax.broadcasted_iota(jnp.int32, (SROWS, LANE), 0)
    real = ridx < ts
    for hh in range(hb):
        sl = slice(hh * LANE, (hh + 1) * LANE)
        g, kk, q = _hgrn_gates(hf_ref[:, sl], hq_ref[:, sl], lbc_ref[:, sl])
        v = hi_ref[:, sl]
        g = jnp.where(real, g, 0.0)
        kk = jnp.where(real, kk, 0.0)
        brow = []
        for i in range(ts):
            brow.append(g[i:i + 1] if i == 0 else brow[-1] + g[i:i + 1])
        s0 = s0_ref[0, hh]
        bt = _row_select(brow, SROWS)
        o_inter = _dot(q * jnp.exp(bt), s0)
        orow = []
        for i in range(ts):
            acc = jnp.zeros((1, LANE), F32)
            for s in range(i + 1):
                wgt = jnp.sum(q[i:i + 1] * kk[s:s + 1] * jnp.exp(brow[i] - brow[s]), axis=-1, keepdims=True)
                acc = acc + wgt * v[s:s + 1]
            orow.append(acc)
        o = o_inter + _row_select(orow, SROWS)
        o_ref[:, sl] = _hgrn_out(o, nw_ref[:, sl], hg_ref[:, sl]).astype(o_ref.dtype)
        cols = [jnp.exp(brow[-1])] + [kk[s:s + 1] * jnp.exp(brow[-1] - brow[s]) for s in range(ts)]
        xt = jnp.concatenate([_row_select(cols, SROWS), jnp.zeros((LANE - SROWS, LANE), F32)], axis=0).T
        s_new = xt[:, 0:1] * s0
        for s in range(ts):
            s_new = s_new + xt[:, 1 + s:2 + s] * v[s:s + 1]
        s_ref[0, hh] = s_new


def _hgrn_sample(h, offs, lbc, nw, s0_all, layer, row0, bs, ts, heads, name):
    hb = _pick(heads * LANE, (8 * LANE, 4 * LANE, 2 * LANE, LANE),
               *(offs[k] for k in ("hq", "hf", "hi", "hg"))) // LANE
    w = hb * LANE
    rb0 = row0 // SROWS

    def hspec(off):
        return pl.BlockSpec((SROWS, w), lambda bi, hi: (rb0 + bi, off // w + hi))

    sspec = pl.BlockSpec((1, hb, LANE, LANE), lambda bi, hi: (bi, hi, 0, 0))
    s0spec = pl.BlockSpec((None, 1, hb, LANE, LANE), lambda bi, hi: (layer, bi, hi, 0, 0))
    return pl.pallas_call(
        functools.partial(_hgrn_sample_kernel, hb=hb, ts=ts),
        out_shape=(jax.ShapeDtypeStruct((bs * SROWS, heads * LANE), BF16),
                   jax.ShapeDtypeStruct((bs, heads, LANE, LANE), F32)),
        grid=(bs, heads // hb),
        in_specs=[hspec(offs["hq"]), hspec(offs["hf"]), hspec(offs["hi"]), hspec(offs["hg"]),
                  pl.BlockSpec((3, w), lambda bi, hi: (0, hi)),
                  pl.BlockSpec((1, w), lambda bi, hi: (0, hi)),
                  s0spec],
        out_specs=(pl.BlockSpec((SROWS, w), lambda bi, hi: (bi, hi)), sspec),
        compiler_params=_cparams(("parallel", "parallel")),
        name=name,
    )(h, h, h, h, lbc, nw, s0_all)


def _ssd_finish(ys, xs, zs, dsk, nw_ref, o_ref, width, col0=0):
    gated = [(y + d * x) * _silu(z) for y, x, z, d in zip(ys, xs, zs, dsk)]
    ssq = sum(jnp.sum(t * t, axis=-1, keepdims=True) for t in gated)
    r = lax.rsqrt(ssq / width + EPS)
    for p, t in enumerate(gated):
        sl = slice(col0 + p * LANE, col0 + (p + 1) * LANE)
        o_ref[:, sl] = (t * r * nw_ref[:, sl]).astype(o_ref.dtype)


def _conv_chunk(u_ref, carry, w_ref, b_ref, rows):
    u, act = _conv_tile(u_ref, carry, w_ref, b_ref, rows)
    carry[...] = u[rows - SROWS:rows]
    return act


def _ssd_prompt_kernel(xr_ref, br_ref, cr_ref, z_ref, dt_ref, dtt_ref, pr_ref, pc_ref, nw_ref, tri_ref, triu_ref,
                       wx_ref, wb_ref, wc_ref, bx_ref, bb_ref, bc_ref,
                       o_ref, s_ref, ht_scr, x_scr, cx_scr, cb_scr, cc_scr, *, c, hpg, pdim):
    ci = pl.program_id(2)
    npair = hpg * pdim // LANE
    hpp = LANE // pdim

    @pl.when(ci == 0)
    def _():
        ht_scr[...] = jnp.zeros_like(ht_scr)
        cx_scr[...] = jnp.zeros_like(cx_scr)
        cb_scr[...] = jnp.zeros_like(cb_scr)
        cc_scr[...] = jnp.zeros_like(cc_scr)

    x_scr[...] = _conv_chunk(xr_ref, cx_scr, wx_ref, bx_ref, c)
    x_ref = x_scr
    b_act = _conv_chunk(br_ref, cb_scr, wb_ref, bb_ref, c)
    c_act = _conv_chunk(cr_ref, cc_scr, wc_ref, bc_ref, c)

    pr = pr_ref[0]
    pc = pc_ref[0]
    dt_c = _softplus(dt_ref[0] + pr[0:1])
    da_c = dt_c * pr[1:2]
    da_r = _softplus(dtt_ref[0] + pc[:, 0:1]) * pc[:, 1:2]
    tri = tri_ref[...]
    triu = triu_ref[...]
    cum_c = sum(_dot(tri, p.astype(F32)) for p in _split3(da_c))
    cum_r = sum(_dot(p.astype(F32), triu) for p in _split3(da_r))
    bm = b_act.astype(BF16)
    cm = c_act.astype(BF16)
    cb = _dot_nt(cm, bm)
    trow = lax.broadcasted_iota(jnp.int32, (c, c), 0)
    scol = lax.broadcasted_iota(jnp.int32, (c, c), 1)
    causal = trow >= scol
    lane = lax.broadcasted_iota(jnp.int32, (1, LANE), 1)

    def per_head(vals):
        out = vals[-1]
        for k in range(hpp - 2, -1, -1):
            out = jnp.where(lane < (k + 1) * pdim, vals[k], out)
        return out

    ys, xs, zs, dsk = [], [], [], []
    for p in range(npair):
        sl = slice(p * LANE, (p + 1) * LANE)
        js = [p * hpp + k for k in range(hpp)]
        xp = x_ref[:, sl]
        xdt = xp * per_head([dt_c[:, j:j + 1] for j in js])
        xdtb = xdt.astype(BF16)
        yj = []
        for j in js:
            seg = jnp.exp(jnp.where(causal, cum_c[:, j:j + 1] - cum_r[j:j + 1, :], MASK_VALUE))
            yj.append(_dot((cb * seg).astype(BF16), xdtb))
        y = per_head(yj)
        htp = ht_scr[p]
        y = y + _dot(cm, htp.astype(BF16)) * per_head([jnp.exp(cum_c[:, j:j + 1]) for j in js])
        last = [cum_c[c - 1:c, j:j + 1] for j in js]
        wgt = per_head([jnp.exp(l - cum_c[:, j:j + 1]) for l, j in zip(last, js)])
        ht_new = htp * per_head([jnp.exp(l) for l in last]) + _dot_tn(bm, (xdt * wgt).astype(BF16))
        ht_scr[p] = ht_new
        ys.append(y)
        xs.append(xp)
        zs.append(z_ref[:, sl])
        dsk.append(per_head([pr[2:3, j:j + 1] for j in js]))

    _ssd_finish(ys, xs, zs, dsk, nw_ref, o_ref, hpg * pdim)

    @pl.when(ci == pl.num_programs(2) - 1)
    def _():
        for p in range(npair):
            s_ref[0, p * LANE:(p + 1) * LANE, :] = ht_scr[p].T


def _ssd_prompt(h, off_xbc, off_z, conv_w, conv_b, mdt_g, mdt_gt, prm_r, prm_c, nw, b, t, groups, hpg, pdim, dstate,
                name):
    c = SSD_CHUNK
    gw = hpg * pdim
    dinner = groups * gw
    nc = t // c
    npair = gw // LANE
    assert off_xbc % gw == 0 and (off_xbc + dinner) % dstate == 0 and c >= SROWS
    xb0 = off_xbc // gw
    nb0 = dinner // dstate
    hb0 = (off_xbc + dinner) // dstate
    tt = np.arange(c)
    tri = jnp.asarray((tt[None, :] <= tt[:, None]).astype(np.float32))
    rows = lambda bi, ci: bi * nc + ci
    const = lambda bi, gi, ci: (0, 0)
    return pl.pallas_call(
        functools.partial(_ssd_prompt_kernel, c=c, hpg=hpg, pdim=pdim),
        out_shape=(jax.ShapeDtypeStruct((b * t, dinner), BF16),
                   jax.ShapeDtypeStruct((b, groups * gw, dstate), F32)),
        grid=(b, groups, nc),
        in_specs=[pl.BlockSpec((c, gw), lambda bi, gi, ci: (rows(bi, ci), xb0 + gi)),
                  pl.BlockSpec((c, dstate), lambda bi, gi, ci: (rows(bi, ci), hb0 + gi)),
                  pl.BlockSpec((c, dstate), lambda bi, gi, ci: (rows(bi, ci), hb0 + groups + gi)),
                  pl.BlockSpec((c, gw), lambda bi, gi, ci: (rows(bi, ci), off_z // gw + gi)),
                  pl.BlockSpec((1, c, hpg), lambda bi, gi, ci: (gi, rows(bi, ci), 0)),
                  pl.BlockSpec((1, hpg, c), lambda bi, gi, ci: (gi, 0, rows(bi, ci))),
                  pl.BlockSpec((1, 3, hpg), lambda bi, gi, ci: (gi, 0, 0)),
                  pl.BlockSpec((1, hpg, 3), lambda bi, gi, ci: (gi, 0, 0)),
                  pl.BlockSpec((1, gw), lambda bi, gi, ci: (0, gi)),
                  pl.BlockSpec((c, c), const),
                  pl.BlockSpec((c, c), const),
                  pl.BlockSpec((CONV_WIDTH, gw), lambda bi, gi, ci: (0, gi)),
                  pl.BlockSpec((CONV_WIDTH, dstate), lambda bi, gi, ci: (0, nb0 + gi)),
                  pl.BlockSpec((CONV_WIDTH, dstate), lambda bi, gi, ci: (0, nb0 + groups + gi)),
                  pl.BlockSpec((1, gw), lambda bi, gi, ci: (0, gi)),
                  pl.BlockSpec((1, dstate), lambda bi, gi, ci: (0, nb0 + gi)),
                  pl.BlockSpec((1, dstate), lambda bi, gi, ci: (0, nb0 + groups + gi))],
        out_specs=(pl.BlockSpec((c, gw), lambda bi, gi, ci: (rows(bi, ci), gi)),
                   pl.BlockSpec((1, gw, dstate), lambda bi, gi, ci: (bi, gi, 0))),
        scratch_shapes=[pltpu.VMEM((npair, dstate, LANE), F32), pltpu.VMEM((c, gw), F32),
                        pltpu.VMEM((SROWS, gw), F32), pltpu.VMEM((SROWS, dstate), F32),
                        pltpu.VMEM((SROWS, dstate), F32)],
        compiler_params=_cparams(("parallel", "parallel", "arbitrary"), 40 << 20),
        name=name,
    )(h, h, h, h, mdt_g, mdt_gt, prm_r, prm_c, nw, tri, tri.T, conv_w, conv_w, conv_w, conv_b, conv_b, conv_b)


def _ssd_sample_kernel(x_ref, b_ref, c_ref, z_ref, dtx_ref, px_ref, nw_ref, s0_ref, o_ref, s_ref, *, ts, gw, ngb):
    npair = gw // LANE
    dstate = b_ref.shape[1] // ngb
    ridx = lax.broadcasted_iota(jnp.int32, (SROWS, ngb * gw), 0)
    dt = jnp.where(ridx < ts, _softplus(dtx_ref[...] + px_ref[0:1, :]), 0.0)
    dec = jnp.exp(dt * px_ref[1:2, :])
    x = x_ref[...]
    xdt = x * dt
    pad = jnp.zeros((LANE - SROWS, dstate), F32)
    for gs in range(ngb):
        nsl = slice(gs * dstate, (gs + 1) * dstate)
        bt = jnp.concatenate([b_ref[:, nsl], pad], axis=0).T
        ct = jnp.concatenate([c_ref[:, nsl], pad], axis=0).T
        ys, xs, zs, dsk = [], [], [], []
        for p in range(npair):
            sl = slice(gs * gw + p * LANE, gs * gw + (p + 1) * LANE)
            ht = s0_ref[0, sl, :].T
            yrow = []
            for i in range(ts):
                ht = ht * dec[i:i + 1, sl] + bt[:, i:i + 1] * xdt[i:i + 1, sl]
                yrow.append(jnp.sum(ht * ct[:, i:i + 1], axis=0, keepdims=True))
            s_ref[0, sl, :] = ht.T
            ys.append(_row_select(yrow, SROWS))
            xs.append(x[:, sl])
            zs.append(z_ref[:, sl])
            dsk.append(px_ref[2:3, sl])
        _ssd_finish(ys, xs, zs, dsk, nw_ref, o_ref, gw, col0=gs * gw)


def _ssd_sample(xa_s, h, off_z, mdt_x, prm_x, nw, s0_all, layer, row0, bs, ts, groups, gw, dstate, name):
    dinner = groups * gw
    rb0 = row0 // SROWS
    nb0 = dinner // dstate
    ngb = _pick(groups, (4, 2, 1), nb0, nb0 + groups, off_z // gw)
    w = ngb * gw
    sspec = pl.BlockSpec((1, w, dstate), lambda bi, gi: (bi, gi, 0))
    s0spec = pl.BlockSpec((None, 1, w, dstate), lambda bi, gi: (layer, bi, gi, 0))
    return pl.pallas_call(
        functools.partial(_ssd_sample_kernel, ts=ts, gw=gw, ngb=ngb),
        out_shape=(jax.ShapeDtypeStruct((bs * SROWS, dinner), BF16),
                   jax.ShapeDtypeStruct((bs, groups * gw, dstate), F32)),
        grid=(bs, groups // ngb),
        in_specs=[pl.BlockSpec((SROWS, w), lambda bi, gi: (bi, gi)),
                  pl.BlockSpec((SROWS, ngb * dstate), lambda bi, gi: (bi, nb0 // ngb + gi)),
                  pl.BlockSpec((SROWS, ngb * dstate), lambda bi, gi: (bi, (nb0 + groups) // ngb + gi)),
                  pl.BlockSpec((SROWS, w), lambda bi, gi: (rb0 + bi, off_z // w + gi)),
                  pl.BlockSpec((SROWS, w), lambda bi, gi: (bi, gi)),
                  pl.BlockSpec((3, w), lambda bi, gi: (0, gi)),
                  pl.BlockSpec((1, w), lambda bi, gi: (0, gi)),
                  s0spec],
        out_specs=(pl.BlockSpec((SROWS, w), lambda bi, gi: (bi, gi)), sspec),
        compiler_params=_cparams(("parallel", "parallel")),
        name=name,
    )(xa_s, xa_s, xa_s, h, mdt_x, prm_x, nw, s0_all)


def _rot(x, cos, sin):
    return x * cos + pltpu.roll(x, LANE // 2, 1) * sin


def _residue_perm(dil, rows):
    i = np.arange(rows)
    p = np.zeros((rows, rows), np.float32)
    p[(i % dil) * (rows // dil) + i // dil, i] = 1.0
    return p


def _rope_prompt_kernel(q_ref, k_ref, v_ref, cos_ref, sin_ref, p_ref, qd_ref, kd_ref, vd_ref, ks_ref,
                        *, nh, dil, rows):
    n = rows // dil
    cos = cos_ref[...]
    sin = sin_ref[...]

    def emit(xb, dst_ref, sl):
        y = _dot(p_ref[...], xb).astype(BF16) if dil > 1 else xb
        for r in range(dil):
            dst_ref[0, r, :, sl] = y[r * n:(r + 1) * n]

    for hh in range(nh):
        sl = slice(hh * LANE, (hh + 1) * LANE)
        emit(_rot(q_ref[:, sl], cos, sin).astype(BF16), qd_ref, sl)
        kr = _rot(k_ref[:, sl], cos, sin)
        ks_ref[:, sl] = kr
        emit(kr.astype(BF16), kd_ref, sl)
        emit(v_ref[:, sl].astype(BF16), vd_ref, sl)


def _rope_prompt(h, offs, gi, dil, b, t, hpg, name):
    ow = hpg * LANE
    rows = _pick(t, (256, 128, 64, 32, 16))
    assert rows % (2 * SROWS * dil) == 0 and all(offs[k] % ow == 0 for k in ("aq", "ak", "av"))
    l, nb, n = t // dil, t // rows, rows // dil
    half = LANE // 2
    inv = ROPE_THETA ** (-jnp.arange(half, dtype=F32) / half)
    ang = jnp.arange(t, dtype=F32)[:, None] * inv[None, :]
    cos2 = jnp.concatenate([jnp.cos(ang), jnp.cos(ang)], axis=-1)
    sin2 = jnp.concatenate([-jnp.sin(ang), jnp.sin(ang)], axis=-1)
    perm = jnp.asarray(_residue_perm(dil, rows), BF16)

    def hspec(key):
        cb = offs[key] // ow + gi
        return pl.BlockSpec((rows, ow), lambda bi, j: (bi * nb + j, cb))

    tspec = pl.BlockSpec((rows, LANE), lambda bi, j: (j, 0))
    dspec = pl.BlockSpec((1, dil, n, ow), lambda bi, j: (bi, 0, j, 0))
    dshape = jax.ShapeDtypeStruct((b, dil, l, ow), BF16)
    return pl.pallas_call(
        functools.partial(_rope_prompt_kernel, nh=hpg, dil=dil, rows=rows),
        out_shape=(dshape, dshape, dshape, jax.ShapeDtypeStruct((b * t, ow), F32)),
        grid=(b, nb),
        in_specs=[hspec("aq"), hspec("ak"), hspec("av"), tspec, tspec,
                  pl.BlockSpec((rows, rows), lambda bi, j: (0, 0))],
        out_specs=(dspec, dspec, dspec, pl.BlockSpec((rows, ow), lambda bi, j: (bi * nb + j, 0))),
        compiler_params=_cparams(("parallel", "parallel")),
        name=name,
    )(h, h, h, cos2, sin2, perm)


def _rope_sample_kernel(q_ref, k_ref, cos_ref, sin_ref, qo_ref, ko_ref, *, nh):
    cos = cos_ref[...]
    sin = sin_ref[...]
    for hh in range(nh):
        sl = slice(hh * LANE, (hh + 1) * LANE)
        qo_ref[:, sl] = _rot(q_ref[:, sl], cos, sin)
        ko_ref[:, sl] = _rot(k_ref[:, sl], cos, sin)


def _rope_sample(h, offs, adim, row0, nrows, ts, name):
    tr = _pick(nrows, (256, 128, 64, 32, 16, 8), row0)
    tc = _pick(adim, (1024, 512, 256, 128), offs["aq"], offs["ak"])
    half = LANE // 2
    inv = ROPE_THETA ** (-jnp.arange(half, dtype=F32) / half)
    srow = jnp.arange(SROWS)
    pos = jnp.tile(jnp.where(srow < ts, PAST_LEN + srow, 0), nrows // SROWS).astype(F32)
    ang = pos[:, None] * inv[None, :]
    cos2 = jnp.concatenate([jnp.cos(ang), jnp.cos(ang)], axis=1)
    sin2 = jnp.concatenate([-jnp.sin(ang), jnp.sin(ang)], axis=1)
    rb0 = row0 // tr

    def hspec(off):
        return pl.BlockSpec((tr, tc), lambda i, j: (rb0 + i, off // tc + j))

    ospec = pl.BlockSpec((tr, tc), lambda i, j: (i, j))
    tspec = pl.BlockSpec((tr, LANE), lambda i, j: (i, 0))
    return pl.pallas_call(
        functools.partial(_rope_sample_kernel, nh=tc // LANE),
        out_shape=(jax.ShapeDtypeStruct((nrows, adim), F32), jax.ShapeDtypeStruct((nrows, adim), F32)),
        grid=(nrows // tr, adim // tc),
        in_specs=[hspec(offs["aq"]), hspec(offs["ak"]), tspec, tspec],
        out_specs=(ospec, ospec),
        compiler_params=_cparams(("parallel", "parallel")),
        name=name,
    )(h, h, cos2, sin2)


def _attn_prompt_kernel(q_ref, kp_ref, kc_ref, vp_ref, vc_ref, o_ref, l_ref, s_scr, p_scr, *, qb, nh, scale):
    has_prev = pl.program_id(2) > 0
    row = lax.broadcasted_iota(jnp.int32, (qb, 2 * qb), 0)
    col = lax.broadcasted_iota(jnp.int32, (qb, 2 * qb), 1)
    in_prev = jnp.logical_and(jnp.logical_and(col < qb, col >= row), has_prev)
    valid = jnp.logical_or(in_prev, jnp.logical_and(col >= qb, col - qb <= row))
    for hh in range(nh):
        sl = slice(hh * LANE, (hh + 1) * LANE)
        kcat = jnp.concatenate([kp_ref[:, sl], kc_ref[:, sl]], axis=0)
        s_scr[hh] = _dot_nt(q_ref[:, sl], kcat)
    for hh in range(nh):
        sl = slice(hh * LANE, (hh + 1) * LANE)
        s = jnp.where(valid, s_scr[hh] * scale, MASK_VALUE)
        m = jnp.max(s, axis=-1, keepdims=True)
        p = jnp.exp(s - m)
        den = jnp.sum(p, axis=-1, keepdims=True)
        p_scr[hh] = (p * (1.0 / den)).astype(BF16)
        l_ref[:, sl] = jnp.broadcast_to(m + jnp.log(den), (qb, LANE))
    for hh in range(nh):
        sl = slice(hh * LANE, (hh + 1) * LANE)
        vcat = jnp.concatenate([vp_ref[:, sl], vc_ref[:, sl]], axis=0)
        o_ref[:, sl] = _dot(p_scr[hh], vcat)


def _attn_prompt(qd, kd, vd, window, dil, name):
    b, _, l, ow = qd.shape
    qb = window // dil
    nq = l // qb
    cur = pl.BlockSpec((None, None, qb, ow), lambda bi, r, i: (bi, r, i, 0))
    prev = pl.BlockSpec((None, None, qb, ow), lambda bi, r, i: (bi, r, jnp.maximum(i - 1, 0), 0))
    oshape = jax.ShapeDtypeStruct((b, dil, l, ow), F32)
    return pl.pallas_call(
        functools.partial(_attn_prompt_kernel, qb=qb, nh=ow // LANE, scale=LANE ** -0.5),
        out_shape=(oshape, oshape),
        grid=(b, dil, nq),
        in_specs=[cur, prev, cur, prev, cur],
        out_specs=(cur, cur),
        scratch_shapes=[pltpu.VMEM((ow // LANE, qb, 2 * qb), F32), pltpu.VMEM((ow // LANE, qb, 2 * qb), BF16)],
        compiler_params=_cparams(("parallel", "parallel", "arbitrary")),
        name=name,
    )(qd, kd, kd, vd, vd)


def _attn_sample_kernel(q_ref, kn_ref, vn_ref, kc_ref, vc_ref, o_ref, l_ref, *, ts, window, dil, scale):
    nk = window // dil
    o_ref[...] = jnp.zeros_like(o_ref)
    l_ref[...] = jnp.zeros_like(l_ref)
    arow = lax.broadcasted_iota(jnp.int32, (nk, 1, 1), 0)
    nrow = lax.broadcasted_iota(jnp.int32, (SROWS, 1, 1), 0)
    kn = kn_ref[...]
    vn = vn_ref[...]
    for i in range(ts):
        rho = (window + i) % dil
        base = (window + i - rho) // dil
        j0 = i // dil + 1
        a_lo, a_hi = max(base - nk, 0), min(base - j0, nk - 1)
        vcache = jnp.logical_and(arow >= a_lo, arow <= a_hi)
        new_rows = [i - j * dil for j in range(i // dil + 1)]
        vnew = functools.reduce(jnp.logical_or, [nrow == r for r in new_rows])
        q = q_ref[i][None]
        kc = kc_ref[:, rho]
        sc = jnp.where(vcache, jnp.sum(kc * q, axis=-1, keepdims=True) * scale, MASK_VALUE)
        sn = jnp.where(vnew, jnp.sum(kn * q, axis=-1, keepdims=True) * scale, MASK_VALUE)
        m = jnp.maximum(jnp.max(sc, axis=0, keepdims=True), jnp.max(sn, axis=0, keepdims=True))
        pc = jnp.where(vcache, jnp.exp(sc - m), 0.0)
        pn = jnp.where(vnew, jnp.exp(sn - m), 0.0)
        den = jnp.sum(pc, axis=0, keepdims=True) + jnp.sum(pn, axis=0, keepdims=True)
        acc = jnp.sum(pc * vc_ref[:, rho], axis=0) + jnp.sum(pn * vn, axis=0)
        o_ref[i] = acc / den[0]
        l_ref[i] = jnp.broadcast_to(m[0] + jnp.log(den[0]), acc.shape)


def _attn_sample(q3, k3, v3, ck_all, cv_all, layer, window, dil, bs, ts, name):
    hpg = q3.shape[1]
    nk = window // dil
    assert ck_all.shape[2] == window and window % dil == 0 and (dil == 1 or dil >= ts)
    nres = min(dil, ts)
    shape6 = ck_all.shape[:2] + (nk, dil, hpg, LANE)
    nspec = pl.BlockSpec((SROWS, hpg, LANE), lambda bi: (bi, 0, 0))
    cspec = pl.BlockSpec((None, None, nk, nres, hpg, LANE), lambda bi: (layer, bi, 0, 0, 0, 0))
    oshape = jax.ShapeDtypeStruct((bs * SROWS, hpg, LANE), F32)
    return pl.pallas_call(
        functools.partial(_attn_sample_kernel, ts=ts, window=window, dil=dil, scale=LANE ** -0.5),
        out_shape=(oshape, oshape),
        grid=(bs,),
        in_specs=[nspec, nspec, nspec, cspec, cspec],
        out_specs=(nspec, nspec),
        compiler_params=_cparams(("parallel",), 40 << 20),
        name=name,
    )(q3, k3, v3, ck_all.reshape(shape6), cv_all.reshape(shape6))


def _combine_kernel(*refs):
    ng = (len(refs) - 1) // 2
    os_, ls_, out = refs[:ng], refs[ng:2 * ng], refs[-1]
    out[...] = _mix_groups([o[...] for o in os_], [l[...] for l in ls_]).astype(out.dtype)


def _mix_groups(os_, ls):
    m = functools.reduce(jnp.maximum, ls)
    ws = [jnp.exp(l - m) for l in ls]
    den = functools.reduce(lambda a, b: a + b, ws)
    acc = functools.reduce(lambda a, b: a + b, [w * o for w, o in zip(ws, os_)])
    return acc / den


def _combine_prompt_kernel(*refs, dils):
    ng = len(dils)
    o_refs, l_refs, p_refs, out = refs[:ng], refs[ng:2 * ng], refs[2 * ng:3 * ng], refs[3 * ng]

    def token_order(ref, p_ref, d):
        x = jnp.concatenate([ref[0, r] for r in range(d)], axis=0)
        if d == 1:
            return x
        return sum(_dot(p_ref[...], piece) for piece in _split3(x))

    os_ = [token_order(o, p, d) for o, p, d in zip(o_refs, p_refs, dils)]
    ls = [token_order(l, p, d) for l, p, d in zip(l_refs, p_refs, dils)]
    out[...] = _mix_groups(os_, ls).astype(out.dtype)


def _combine_prompt(outs, lses, dils, name):
    b, _, _, ow = outs[0].shape
    t = outs[0].shape[1] * outs[0].shape[2]
    rows = _pick(t, (256, 128, 64, 32, 16))
    assert all(rows % (SROWS * d) == 0 for d in dils)
    nb = t // rows
    specs = [pl.BlockSpec((1, d, rows // d, ow), lambda bi, j: (bi, 0, j, 0)) for d in dils]
    perms = [jnp.asarray(_residue_perm(d, rows).T, BF16) for d in dils]
    pspec = pl.BlockSpec((rows, rows), lambda bi, j: (0, 0))
    return pl.pallas_call(
        functools.partial(_combine_prompt_kernel, dils=tuple(dils)),
        out_shape=jax.ShapeDtypeStruct((b * t, ow), BF16),
        grid=(b, nb),
        in_specs=specs + specs + [pspec] * len(dils),
        out_specs=pl.BlockSpec((rows, ow), lambda bi, j: (bi * nb + j, 0)),
        compiler_params=_cparams(("parallel", "parallel"), 40 << 20),
        name=name,
    )(*outs, *lses, *perms)


def _combine(outs, lses, name):
    n, ow = outs[0].shape
    tr = _pick(n, (512, 256, 128, 64, 8))
    spec = pl.BlockSpec((tr, ow), lambda i: (i, 0))
    return pl.pallas_call(
        _combine_kernel,
        out_shape=jax.ShapeDtypeStruct((n, ow), BF16),
        grid=(n // tr,),
        in_specs=[spec] * (2 * len(outs)),
        out_specs=spec,
        compiler_params=_cparams(("parallel",)),
        name=name,
    )(*outs, *lses)


def _merge_kernel(oa_ref, yb_ref, oc_ref, wa_ref, wb_ref, wc_ref, ga_ref, gb_ref, gc_ref, o_ref):
    acc = _sigmoid(ga_ref[...]) * _dot(oa_ref[...], wa_ref[...])
    acc = acc + _sigmoid(gb_ref[...]) * _dot(yb_ref[...], wb_ref[...])
    acc = acc + _sigmoid(gc_ref[...]) * _dot(oc_ref[...], wc_ref[...])
    o_ref[...] = acc.astype(o_ref.dtype)


def _merge(oa, yb, oc, wa, wb, wc, h, off_g, d, name):
    n = oa.shape[0]
    tm = _pick(n, (384, 256, 128, 88, 64, 8))
    tn = _pick(d, (512, 256, 128), off_g)
    ka, kb, kc = oa.shape[1], yb.shape[1], oc.shape[1]

    def gspec(k):
        return pl.BlockSpec((tm, tn), lambda j, i: (i, (off_g + k * d) // tn + j))

    return pl.pallas_call(
        _merge_kernel,
        out_shape=jax.ShapeDtypeStruct((n, d), BF16),
        grid=(d // tn, n // tm),
        in_specs=[pl.BlockSpec((tm, ka), lambda j, i: (i, 0)),
                  pl.BlockSpec((tm, kb), lambda j, i: (i, 0)),
                  pl.BlockSpec((tm, kc), lambda j, i: (i, 0)),
                  pl.BlockSpec((ka, tn), lambda j, i: (0, j)),
                  pl.BlockSpec((kb, tn), lambda j, i: (0, j)),
                  pl.BlockSpec((kc, tn), lambda j, i: (0, j)),
                  gspec(0), gspec(1), gspec(2)],
        out_specs=pl.BlockSpec((tm, tn), lambda j, i: (i, j)),
        compiler_params=_cparams(("parallel", "parallel"), 48 << 20),
        name=name,
    )(oa, yb, oc, wa, wb, wc, h, h, h)


def _layer_norm(y, g, b):
    mu = jnp.mean(y, axis=-1, keepdims=True)
    yc = y - mu
    var = jnp.mean(yc * yc, axis=-1, keepdims=True)
    return yc * lax.rsqrt(var + EPS) * g + b


def _outln_kernel(m_ref, w_ref, x_ref, g_ref, b_ref, o_ref, ob_ref, *, alpha):
    y = alpha * x_ref[...] + _dot(m_ref[...], w_ref[...])
    out = _layer_norm(y, g_ref[...], b_ref[...])
    o_ref[...] = out
    ob_ref[...] = out.astype(ob_ref.dtype)


def _outln(merged, w_out, x, g, b, alpha, name):
    n, d = x.shape
    tm = _pick(n, (384, 256, 128, 88, 64, 8))
    rspec = pl.BlockSpec((tm, d), lambda i: (i, 0))
    vspec = pl.BlockSpec((1, d), lambda i: (0, 0))
    return pl.pallas_call(
        functools.partial(_outln_kernel, alpha=alpha),
        out_shape=(jax.ShapeDtypeStruct((n, d), F32), jax.ShapeDtypeStruct((n, d), BF16)),
        grid=(n // tm,),
        in_specs=[rspec, pl.BlockSpec((d, d), lambda i: (0, 0)), rspec, vspec, vspec],
        out_specs=(rspec, rspec),
        compiler_params=_cparams(("parallel",), 48 << 20),
        name=name,
    )(merged, w_out, x, g, b)


def _resln_kernel(x_ref, f_ref, g_ref, b_ref, o_ref, ob_ref, *, alpha):
    out = _layer_norm(alpha * x_ref[...] + f_ref[...], g_ref[...], b_ref[...])
    o_ref[...] = out
    ob_ref[...] = out.astype(ob_ref.dtype)


def _resln(x, f, g, b, alpha, name):
    n, d = x.shape
    tm = _pick(n, (384, 256, 128, 88, 64, 8))
    rspec = pl.BlockSpec((tm, d), lambda i: (i, 0))
    vspec = pl.BlockSpec((1, d), lambda i: (0, 0))
    return pl.pallas_call(
        functools.partial(_resln_kernel, alpha=alpha),
        out_shape=(jax.ShapeDtypeStruct((n, d), F32), jax.ShapeDtypeStruct((n, d), BF16)),
        grid=(n // tm,),
        in_specs=[rspec, rspec, vspec, vspec],
        out_specs=(rspec, rspec),
        compiler_params=_cparams(("parallel",)),
        name=name,
    )(x, f, g, b)


def _ffn_a_kernel(te_ref, x_ref, w1_ref, w3_ref, o_ref):
    used = pl.program_id(1) < te_ref[pl.num_programs(1)]

    @pl.when(used)
    def _():
        x = x_ref[...].astype(BF16)
        a = _dot(x, w1_ref[0])
        o_ref[...] = (_silu(a) * _dot(x, w3_ref[0])).astype(o_ref.dtype)

    @pl.when(jnp.logical_not(used))
    def _():
        o_ref[...] = jnp.zeros_like(o_ref)


def _ffn_a(x, w1, w3, tile_expert, tm, name):
    n, d = x.shape
    f = w1.shape[2]
    tf = _pick(f, (1408, 1024, 512, 256, 128))
    xbytes = x.dtype.itemsize
    vm = 2 * (tm * d * xbytes + 2 * d * tf * 2 + tm * tf * 2) + 3 * tm * tf * 4 + (6 << 20)
    return pl.pallas_call(
        _ffn_a_kernel,
        out_shape=jax.ShapeDtypeStruct((n, f), BF16),
        grid_spec=pltpu.PrefetchScalarGridSpec(
            num_scalar_prefetch=1,
            grid=(f // tf, n // tm),
            in_specs=[pl.BlockSpec((tm, d), lambda j, i, te: (i, 0)),
                      pl.BlockSpec((1, d, tf), lambda j, i, te: (te[i], 0, j)),
                      pl.BlockSpec((1, d, tf), lambda j, i, te: (te[i], 0, j))],
            out_specs=pl.BlockSpec((tm, tf), lambda j, i, te: (i, j))),
        compiler_params=_cparams(("parallel", "arbitrary"), vm),
        name=name,
    )(tile_expert, x, w1, w3)


def _ffn_b_kernel(te_ref, h_ref, w2_ref, o_ref):
    used = pl.program_id(1) < te_ref[pl.num_programs(1)]

    @pl.when(used)
    def _():
        o_ref[...] = _dot(h_ref[...], w2_ref[0])

    @pl.when(jnp.logical_not(used))
    def _():
        o_ref[...] = jnp.zeros_like(o_ref)


def _ffn_b(hmid, w2, tile_expert, tm, name):
    n, f = hmid.shape
    d = w2.shape[2]
    tn = _pick(d, (1024, 512, 256, 128))
    vm = 2 * (tm * f * 2 + f * tn * 2 + tm * tn * 4) + (4 << 20)
    return pl.pallas_call(
        _ffn_b_kernel,
        out_shape=jax.ShapeDtypeStruct((n, d), F32),
        grid_spec=pltpu.PrefetchScalarGridSpec(
            num_scalar_prefetch=1,
            grid=(d // tn, n // tm),
            in_specs=[pl.BlockSpec((tm, f), lambda j, i, te: (i, 0)),
                      pl.BlockSpec((1, f, tn), lambda j, i, te: (te[i], 0, j))],
            out_specs=pl.BlockSpec((tm, tn), lambda j, i, te: (i, j))),
        compiler_params=_cparams(("parallel", "arbitrary"), vm),
        name=name,
    )(tile_expert, hmid, w2)


def _router_kernel(x_ref, w_ref, b_ref, e_ref, g_ref, *, n_exp):
    xs = _split3(x_ref[...])
    ws = _split3(w_ref[...])
    logits = b_ref[...]
    for i in range(3):
        for j in range(3 - i):
            logits = logits + _dot(xs[i], ws[j])
    lane = lax.broadcasted_iota(jnp.int32, logits.shape, 1)
    neg = jnp.float32(-jnp.inf)
    logits = jnp.where(lane < n_exp, logits, neg)
    m0 = jnp.max(logits, axis=-1, keepdims=True)
    i0 = jnp.min(jnp.where(logits == m0, lane, LANE), axis=-1, keepdims=True)
    rest = jnp.where(lane == i0, neg, logits)
    m1 = jnp.max(rest, axis=-1, keepdims=True)
    i1 = jnp.min(jnp.where(rest == m1, lane, LANE), axis=-1, keepdims=True)
    e1 = jnp.exp(m1 - m0)
    den = 1.0 + e1
    e_ref[...] = jnp.where(lane == 0, i0, jnp.where(lane == 1, i1, 0))
    g_ref[...] = jnp.where(lane == 0, 1.0 / den, jnp.where(lane == 1, e1 / den, 0.0))


def _router(x, w_router, b_router, name):
    n, d = x.shape
    n_exp = w_router.shape[1]
    wp = jnp.zeros((d, LANE), F32).at[:, :n_exp].set(w_router.astype(F32))
    bp = jnp.zeros((1, LANE), F32).at[0, :n_exp].set(b_router.astype(F32))
    tm = _pick(n, (384, 256, 128, 88, 64, 8))
    rspec = pl.BlockSpec((tm, LANE), lambda i: (i, 0))
    return pl.pallas_call(
        functools.partial(_router_kernel, n_exp=n_exp),
        out_shape=(jax.ShapeDtypeStruct((n, LANE), jnp.int32), jax.ShapeDtypeStruct((n, LANE), F32)),
        grid=(n // tm,),
        in_specs=[pl.BlockSpec((tm, d), lambda i: (i, 0)),
                  pl.BlockSpec((d, LANE), lambda i: (0, 0)),
                  pl.BlockSpec((1, LANE), lambda i: (0, 0))],
        out_specs=(rspec, rspec),
        compiler_params=_cparams(("parallel",)),
        name=name,
    )(x, wp, bp)


def _row_copy(src_hbm, row, dst_ref, dst_row, sem):
    return pltpu.make_async_copy(src_hbm.at[pl.ds(row, 1)], dst_ref.at[pl.ds(dst_row, 1)], sem)


def _gather_kernel(idx_ref, x_hbm, o_ref, sem, *, tm):
    base = pl.program_id(0) * tm

    def start(r, carry):
        _row_copy(x_hbm, idx_ref[base + r], o_ref, r, sem).start()
        return carry

    def wait(r, carry):
        _row_copy(x_hbm, 0, o_ref, r, sem).wait()
        return carry

    lax.fori_loop(0, tm, start, 0, unroll=8)
    lax.fori_loop(0, tm, wait, 0)


def _gather_rows(x, idx, tm, name):
    n_out = idx.shape[0]
    d = x.shape[1]
    return pl.pallas_call(
        functools.partial(_gather_kernel, tm=tm),
        out_shape=jax.ShapeDtypeStruct((n_out, d), x.dtype),
        grid_spec=pltpu.PrefetchScalarGridSpec(
            num_scalar_prefetch=1,
            grid=(n_out // tm,),
            in_specs=[pl.BlockSpec(memory_space=pl.ANY)],
            out_specs=pl.BlockSpec((tm, d), lambda i, idx_ref: (i, 0)),
            scratch_shapes=[pltpu.SemaphoreType.DMA(())]),
        compiler_params=_cparams(("arbitrary",)),
        name=name,
    )(idx, x)


def _moe_out_kernel(pos_ref, y_hbm, x_ref, gate_ref, g_ref, b_ref, o_ref, ob_ref, buf, sem, *, tm, alpha):
    base = pl.program_id(0) * tm

    def start(r, carry):
        for k in range(2):
            _row_copy(y_hbm, pos_ref[2 * (base + r) + k], buf.at[k], r, sem).start()
        return carry

    def wait(r, carry):
        for k in range(2):
            _row_copy(y_hbm, 0, buf.at[k], r, sem).wait()
        return carry

    lax.fori_loop(0, tm, start, 0, unroll=8)
    lax.fori_loop(0, tm, wait, 0)
    gt = gate_ref[...]
    f = buf[0] * gt[:, 0:1] + buf[1] * gt[:, 1:2]
    out = _layer_norm(alpha * x_ref[...] + f, g_ref[...], b_ref[...])
    o_ref[...] = out
    ob_ref[...] = out.astype(ob_ref.dtype)


def _moe_out(ys, pos, x, gates, g, b, alpha, name):
    n, d = x.shape
    tm = _pick(n, (256, 128, 88, 64, 8))
    rspec = pl.BlockSpec((tm, d), lambda i, p: (i, 0))
    vspec = pl.BlockSpec((1, d), lambda i, p: (0, 0))
    return pl.pallas_call(
        functools.partial(_moe_out_kernel, tm=tm, alpha=alpha),
        out_shape=(jax.ShapeDtypeStruct((n, d), F32), jax.ShapeDtypeStruct((n, d), BF16)),
        grid_spec=pltpu.PrefetchScalarGridSpec(
            num_scalar_prefetch=1,
            grid=(n // tm,),
            in_specs=[pl.BlockSpec(memory_space=pl.ANY), rspec,
                      pl.BlockSpec((tm, LANE), lambda i, p: (i, 0)), vspec, vspec],
            out_specs=(rspec, rspec),
            scratch_shapes=[pltpu.VMEM((2, tm, d), F32), pltpu.SemaphoreType.DMA(())]),
        compiler_params=_cparams(("arbitrary",)),
        name=name,
    )(pos, ys, x, gates, g, b)


def _moe_ffn(x, w_router, b_router, w1, w3, w2, g, b, alpha, tag):
    n, d = x.shape
    n_exp = w1.shape[0]
    tm = MOE_TILE
    e_pad, gates = _router(x, w_router, b_router, f"router_{tag}")
    e_flat = e_pad[:, :2].reshape(-1)
    n_slots = 2 * n
    onehot = (e_flat[:, None] == jnp.arange(n_exp, dtype=jnp.int32)[None, :]).astype(jnp.int32)
    csum = jnp.cumsum(onehot, axis=0)
    rank = jnp.take_along_axis(csum, e_flat[:, None], axis=1)[:, 0] - 1
    counts = csum[-1]
    padded = (counts + tm - 1) // tm * tm
    pad_end = jnp.cumsum(padded)
    pos = (pad_end - padded)[e_flat] + rank
    n_tiles = -(-(n_slots + n_exp * (tm - 1)) // tm)
    slot_tok = jnp.zeros((n_tiles * tm,), jnp.int32).at[pos].set(jnp.arange(n_slots, dtype=jnp.int32) // 2)
    tile_expert = jnp.minimum(
        jnp.searchsorted(pad_end, jnp.arange(n_tiles, dtype=jnp.int32) * tm, side="right"), n_exp - 1
    ).astype(jnp.int32)
    tile_expert = jnp.concatenate([tile_expert, (pad_end[-1:] // tm).astype(jnp.int32)])
    xs = _gather_rows(x, slot_tok, tm, f"moe_gather_{tag}")
    hmid = _ffn_a(xs, w1, w3, tile_expert, tm, f"moe_a_{tag}")
    ys = _ffn_b(hmid, w2, tile_expert, tm, f"moe_b_{tag}")
    return _moe_out(ys, pos.astype(jnp.int32), x, gates, g, b, alpha, f"moe_out_{tag}")


def kernel(x_prompt, x_sample, state_hgrn, state_ssm, state_conv, cache_k0, cache_v0, cache_k1, cache_v1,
           cache_k2, cache_v2, w_in, hg_lb, hg_norm_w, conv_w, conv_b, dt_bias, a_log, d_skip, ssm_norm_w,
           w_proj_a, w_proj_b, w_proj_c, w_out, ln1_g, ln1_b, ln2_g, ln2_b, ffn_w1, ffn_w3, ffn_w2,
           moe_router, moe_router_b, moe_w1, moe_w3, moe_w2):
    caches = ((cache_k0, cache_v0), (cache_k1, cache_v1), (cache_k2, cache_v2))
    depth = w_in.shape[0]
    b, t, d = x_prompt.shape
    bs, ts, _ = x_sample.shape
    _, _, hg_heads, hg_dk, hg_dv = state_hgrn.shape
    _, _, m_heads, pdim, dstate = state_ssm.shape
    conv_dim = state_conv.shape[-1]
    hpg, adh = cache_k0.shape[-2:]
    assert hg_dk == LANE and hg_dv == LANE and dstate == LANE and adh == LANE and ts <= SROWS
    hg_dim = hg_heads * LANE
    dinner = m_heads * pdim
    groups = (conv_dim - dinner) // (2 * dstate)
    m_hpg = m_heads // groups
    gw = m_hpg * pdim
    adim = len(A_GROUPS) * hpg * LANE
    assert gw % LANE == 0 and LANE % pdim == 0 and t % HG_CHUNK == 0 and t % SSD_CHUNK == 0
    ow = hpg * LANE
    alpha = (2 * depth) ** 0.25
    bt = b * t
    n = bt + bs * SROWS

    sizes = (("hq", hg_dim), ("hf", hg_dim), ("hi", hg_dim), ("hg", hg_dim), ("mz", dinner), ("xbc", conv_dim),
             ("mdt", m_heads), ("aq", adim), ("ak", adim), ("av", adim), ("gate", 3 * d))
    src, acc = {}, 0
    for name, sz in sizes:
        src[name] = acc
        acc += sz
    assert acc == w_in.shape[2]
    n_a = src["mdt"]
    n_b = 3 * adim + 3 * d
    offs = {k: src[k] for k in ("hq", "hf", "hi", "hg", "mz", "xbc")}
    offs.update({k: src[k] - src["aq"] for k in ("aq", "ak", "av", "gate")})
    assert n_a % LANE == 0 and m_heads < LANE and n_b % LANE == 0 and n_a + LANE <= w_in.shape[2]
    assert offs["mz"] % gw == 0 and all(offs[k] % ow == 0 for k in ("aq", "ak", "av"))

    xs_pad = jnp.zeros((bs, SROWS, d), F32).at[:, :ts].set(x_sample)
    x = jnp.concatenate([x_prompt.reshape(bt, d), xs_pad.reshape(bs * SROWS, d)], axis=0)
    xb = x.astype(BF16)

    state_ssm_r = state_ssm.reshape(depth, bs, m_heads * pdim, dstate)
    w_in_t = jnp.swapaxes(w_in, 1, 2)
    lb_sm = jax.nn.softmax(hg_lb.astype(F32), axis=0)
    lb_all = jnp.cumsum(lb_sm, axis=0) - lb_sm[0]

    out_hg_p, out_ssm_p, out_conv_p, out_hg_s, out_ssm_s, out_conv_s = [], [], [], [], [], []
    out_kv_p = [[] for _ in range(2 * len(A_GROUPS))]
    out_kv_s = [[] for _ in range(2 * len(A_GROUPS))]

    for l in range(depth):
        h = _matmul_wcast(xb, w_in_t, l, 0, n_a, 0, F32, f"in_proj_a_{l}")
        h_b = _matmul_wcast(xb, w_in_t, l, n_a, n_b, m_heads, F32, f"in_proj_b_{l}")
        mdt = _matmul_wcast(xb, w_in_t, l, n_a, LANE, 0, F32, f"in_proj_dt_{l}")[:, :m_heads]

        lb = lb_all[l]
        lbc = jnp.stack([jnp.log(jnp.maximum(lb, LB_FLOOR)), jnp.log1p(-lb), 1.0 - lb])
        nw_a = hg_norm_w[l][None, :]
        oa_p, hg_p = _hgrn_prompt(h, offs, lbc, nw_a, b, t, hg_heads, f"hgrn_prompt_{l}")
        oa_s, hg_s = _hgrn_sample(h, offs, lbc, nw_a, state_hgrn, l, bt, bs, ts, hg_heads, f"hgrn_sample_{l}")
        o_a = jnp.concatenate([oa_p, oa_s], axis=0)

        cw, cbias = conv_w[l], conv_b[l][None, :]
        prev_s = jnp.zeros((bs, SROWS, conv_dim), F32).at[:, SROWS - (CONV_WIDTH - 1):].set(state_conv[l])
        xa_s = _conv_sample(h, offs["xbc"], conv_dim, cw, cbias, bt, prev_s.reshape(bs * SROWS, conv_dim),
                            f"conv_sample_{l}")
        a_neg = -jnp.exp(a_log[l].astype(F32))
        prm = jnp.stack([dt_bias[l].astype(F32), a_neg, d_skip[l].astype(F32)])
        prm_r = prm.reshape(3, groups, m_hpg).transpose(1, 0, 2)
        mdt_g = mdt[:bt].reshape(bt, groups, m_hpg).transpose(1, 0, 2)
        nw_b = ssm_norm_w[l][None, :]
        yb_p, ssm_p = _ssd_prompt(h, offs["xbc"], offs["mz"], cw, cbias, mdt_g, mdt_g.transpose(0, 2, 1), prm_r,
                                  prm_r.transpose(0, 2, 1), nw_b, b, t, groups, m_hpg, pdim, dstate,
                                  f"ssd_prompt_{l}")
        yb_s, ssm_s = _ssd_sample(xa_s, h, offs["mz"], jnp.repeat(mdt[bt:], pdim, axis=1),
                                  jnp.repeat(prm, pdim, axis=1), nw_b,
                                  state_ssm_r, l, bt, bs, ts, groups, gw, dstate, f"ssd_sample_{l}")
        y_b = jnp.concatenate([yb_p, yb_s], axis=0)

        q_s, k_s = _rope_sample(h_b, offs, adim, bt, bs * SROWS, ts, f"rope_sample_{l}")
        v_s = h_b[bt:, offs["av"]:offs["av"] + adim]
        outs_p, lses_p, outs_s, lses_s, k_std = [], [], [], [], []
        for gi, (window, dil) in enumerate(A_GROUPS):
            qd, kd, vd, ks = _rope_prompt(h_b, offs, gi, dil, b, t, hpg, f"rope_prompt_{l}_{gi}")
            k_std.append(ks)
            o_g, l_g = _attn_prompt(qd, kd, vd, window, dil, f"attn_prompt_{l}_{gi}")
            outs_p.append(o_g)
            lses_p.append(l_g)
            g3 = lambda a: a[:, gi * ow:(gi + 1) * ow].reshape(bs * SROWS, hpg, LANE)
            o_g, l_g = _attn_sample(g3(q_s), g3(k_s), g3(v_s), caches[gi][0], caches[gi][1], l, window, dil,
                                    bs, ts, f"attn_sample_{l}_{gi}")
            outs_s.append(o_g.reshape(bs * SROWS, ow))
            lses_s.append(l_g.reshape(bs * SROWS, ow))
        o_c = jnp.concatenate([_combine_prompt(outs_p, lses_p, [dl for _, dl in A_GROUPS], f"attn_mix_prompt_{l}"),
                               _combine(outs_s, lses_s, f"attn_mix_sample_{l}")], axis=0)

        merged = _merge(o_a, y_b, o_c, w_proj_a[l].astype(BF16), w_proj_b[l].astype(BF16),
                        w_proj_c[l].astype(BF16), h_b, offs["gate"], d, f"merge_{l}")
        x, xb = _outln(merged, w_out[l].astype(BF16), x, ln1_g[l][None, :], ln1_b[l][None, :], alpha, f"out_ln1_{l}")

        if l % 2 == 0:
            i = l // 2
            tm = _pick(n, (768, 512, 384, 256, 128, 88, 64, 8))
            te = jnp.zeros((n // tm + 1,), jnp.int32).at[-1].set(n // tm)
            hmid = _ffn_a(xb, ffn_w1[i:i + 1].astype(BF16), ffn_w3[i:i + 1].astype(BF16), te, tm, f"ffn_a_{l}")
            f = _ffn_b(hmid, ffn_w2[i:i + 1].astype(BF16), te, tm, f"ffn_b_{l}")
            x, xb = _resln(x, f, ln2_g[l][None, :], ln2_b[l][None, :], alpha, f"ln2_{l}")
        else:
            i = l // 2
            x, xb = _moe_ffn(x, moe_router[i], moe_router_b[i], moe_w1[i].astype(BF16), moe_w3[i].astype(BF16),
                             moe_w2[i].astype(BF16), ln2_g[l][None, :], ln2_b[l][None, :], alpha, str(l))

        keep_c = CONV_WIDTH - 1
        assert t >= keep_c

        def tail_rows(arr, c0, c1, keep):
            return jnp.stack([arr[(bi + 1) * t - keep:(bi + 1) * t, c0:c1] for bi in range(b)])

        out_conv_p.append(tail_rows(h, offs["xbc"], offs["xbc"] + conv_dim, keep_c))
        xbc_s = h[bt:, offs["xbc"]:offs["xbc"] + conv_dim].reshape(bs, SROWS, conv_dim)[:, :ts]
        out_conv_s.append(jnp.concatenate([state_conv[l], xbc_s], axis=1)[:, -keep_c:])
        out_hg_p.append(hg_p)
        out_hg_s.append(hg_s)
        out_ssm_p.append(ssm_p.reshape(b, m_heads, pdim, dstate))
        out_ssm_s.append(ssm_s.reshape(bs, m_heads, pdim, dstate))
        ksm = k_s.reshape(bs, SROWS, adim // LANE, LANE)[:, :ts]
        vsm = v_s.reshape(bs, SROWS, adim // LANE, LANE)[:, :ts]
        for gi, (window, _) in enumerate(A_GROUPS):
            hs = slice(gi * hpg, (gi + 1) * hpg)
            keep = min(window, t)
            v0 = offs["av"] + gi * ow
            out_kv_p[2 * gi].append(k_std[gi].reshape(b, t, hpg, LANE)[:, t - keep:])
            out_kv_p[2 * gi + 1].append(tail_rows(h_b, v0, v0 + ow, keep).reshape(b, keep, hpg, LANE))
            out_kv_s[2 * gi].append(ksm[:, :, hs])
            out_kv_s[2 * gi + 1].append(vsm[:, :, hs])

    y_prompt = x[:bt].reshape(b, t, d)
    y_sample = x[bt:].reshape(bs, SROWS, d)[:, :ts]
    return (y_prompt, y_sample, jnp.stack(out_hg_p), jnp.stack(out_ssm_p), jnp.stack(out_conv_p),
            *[jnp.stack(o) for o in out_kv_p],
            jnp.stack(out_hg_s), jnp.stack(out_ssm_s), jnp.stack(out_conv_s),
            *[jnp.stack(o) for o in out_kv_s])
```

```python
import functools
import math

import numpy as np
import jax
import jax.numpy as jnp
from jax import lax
from jax.experimental import pallas as pl
from jax.experimental.pallas import tpu as pltpu

F32 = jnp.float32
BF16 = jnp.bfloat16

LANE = 128
SROWS = 8
VMEM_CAP = 56 * 1024 * 1024

A_GROUPS = ((128, 1), (512, 4), (2048, 16))
PAST_LEN = 8192
ROPE_THETA = 10000.0
MASK_VALUE = -1e30
LB_FLOOR = 1e-30
EPS = 1e-5
LOG2E = 1.4426950408889634
CONV_WIDTH = 4
HG_CHUNK = 128
SSD_CHUNK = 128
MOE_TILE = 256


def _cparams(sem, vmem_bytes=None):
    kw = dict(dimension_semantics=sem)
    if vmem_bytes is not None:
        kw["vmem_limit_bytes"] = int(min(max(vmem_bytes, 16 * 1024 * 1024), VMEM_CAP))
    return pltpu.CompilerParams(**kw)


def _pick(n, cands, *offsets):
    for c in cands:
        if n % c == 0 and all(o % c == 0 for o in offsets):
            return c
    raise ValueError(f"no tile for {n} in {cands} (offsets {offsets})")


def _sigmoid(x):
    return 1.0 / (1.0 + jnp.exp(-x))


def _silu(x):
    return x * _sigmoid(x)


def _softplus(x):
    return jnp.maximum(x, 0.0) + jnp.log1p(jnp.exp(-jnp.abs(x)))


def _dot(a, b):
    return jnp.dot(a, b, preferred_element_type=F32)


def _dot_nt(a, b):
    return lax.dot_general(a, b, (((1,), (1,)), ((), ())), preferred_element_type=F32)


def _dot_tn(a, b):
    return lax.dot_general(a, b, (((0,), (0,)), ((), ())), preferred_element_type=F32)


def _split2(x):
    hi = x.astype(BF16)
    lo = (x - hi.astype(F32)).astype(BF16)
    return hi, lo


def _split3(x):
    p1 = x.astype(BF16)
    r1 = x - p1.astype(F32)
    p2 = r1.astype(BF16)
    p3 = (r1 - p2.astype(F32)).astype(BF16)
    return p1, p2, p3


def _mm_wcast_kernel(x_ref, w_ref, *rest, shift):
    o_ref, wb_scr = rest[-2:]

    @pl.when(pl.program_id(1) == 0)
    def _():
        w = w_ref[...]
        if shift:
            w = jnp.concatenate([w[shift:], rest[0][:shift]], axis=0)
        wb_scr[...] = w.astype(BF16)

    o_ref[...] = _dot_nt(x_ref[...], wb_scr[...]).astype(o_ref.dtype)


def _matmul_wcast(x, wt_all, layer, col0, ncols, shift, out_dtype, name):
    m, k = x.shape
    tm = _pick(m, (768, 512, 384, 256, 128, 88, 64, 8))
    tn = _pick(ncols, (1024, 512, 256, 128), col0)
    assert shift % SROWS == 0 and shift < LANE
    vm = 2 * (tm * k * 2 + k * (tn + LANE) * 4 + tm * tn * 4) + 2 * k * tn * 4 + tm * tn * 4 + (4 << 20)
    in_specs = [pl.BlockSpec((tm, k), lambda j, i: (i, 0)),
                pl.BlockSpec((None, tn, k), lambda j, i: (layer, col0 // tn + j, 0))]
    operands = [x, wt_all]
    if shift:
        in_specs.append(pl.BlockSpec((None, LANE, k), lambda j, i: (layer, (col0 + (j + 1) * tn) // LANE, 0)))
        operands.append(wt_all)
    return pl.pallas_call(
        functools.partial(_mm_wcast_kernel, shift=shift),
        out_shape=jax.ShapeDtypeStruct((m, ncols), out_dtype),
        grid=(ncols // tn, m // tm),
        in_specs=in_specs,
        out_specs=pl.BlockSpec((tm, tn), lambda j, i: (i, j)),
        scratch_shapes=[pltpu.VMEM((tn, k), BF16)],
        compiler_params=_cparams(("parallel", "arbitrary"), vm),
        name=name,
    )(*operands)


def _conv_tile(u_ref, carry, w_ref, b_ref, rows):
    u = u_ref[...]
    full = jnp.concatenate([carry[...], u], axis=0)
    acc = b_ref[...] + u * w_ref[CONV_WIDTH - 1:CONV_WIDTH, :]
    for i in range(CONV_WIDTH - 1):
        off = SROWS - (CONV_WIDTH - 1) + i
        acc = acc + full[off:off + rows] * w_ref[i:i + 1, :]
    return u, _silu(acc)


def _conv_sample_kernel(u_ref, prev_ref, w_ref, b_ref, o_ref):
    o_ref[...] = _conv_tile(u_ref, prev_ref, w_ref, b_ref, SROWS)[1]


def _conv_sample(h, off_xbc, conv_dim, w, b, row0, prev_arr, name):
    nrows = prev_arr.shape[0]
    tc = _pick(conv_dim, (2048, 1536, 1024, 512, 256, 128), off_xbc)
    cb0 = off_xbc // tc
    rb0 = row0 // SROWS
    return pl.pallas_call(
        _conv_sample_kernel,
        out_shape=jax.ShapeDtypeStruct((nrows, conv_dim), F32),
        grid=(nrows // SROWS, conv_dim // tc),
        in_specs=[pl.BlockSpec((SROWS, tc), lambda i, j: (rb0 + i, cb0 + j)),
                  pl.BlockSpec((SROWS, tc), lambda i, j: (i, j)),
                  pl.BlockSpec((CONV_WIDTH, tc), lambda i, j: (0, j)),
                  pl.BlockSpec((1, tc), lambda i, j: (0, j))],
        out_specs=pl.BlockSpec((SROWS, tc), lambda i, j: (i, j)),
        compiler_params=_cparams(("parallel", "parallel")),
        name=name,
    )(h, prev_arr, w, b)


def _hgrn_consts(c):
    t = np.arange(c)[:, None]
    u = np.arange(c)[None, :]
    mats, masks = [], [np.eye(c)]
    h = 1
    while h < c:
        tb, ub = t // h, u // h
        mats.append(((tb % 2 == 1) & (ub == tb) & (u <= t)) | ((tb % 2 == 0) & (ub == tb) & (u > t)))
        masks.append((tb % 2 == 1) & (ub == tb - 1))
        h *= 2
    mats.append(u <= t)
    mall = np.concatenate([m.astype(np.float32) for m in mats], axis=0)
    mall = np.concatenate([mall, mall], axis=1)
    return jnp.asarray(mall, BF16), jnp.asarray(np.stack(masks).astype(np.float32))


def _hgrn_gates(fx, hq, lbc):
    la, l1, oml = lbc[0:1], lbc[1:2], lbc[2:3]
    ls = jnp.minimum(fx, 0.0) - jnp.log1p(jnp.exp(-jnp.abs(fx)))
    ct = l1 + ls
    g = jnp.maximum(la, ct) + jnp.log1p(jnp.exp(-jnp.abs(la - ct)))
    kk = oml * (1.0 / (1.0 + jnp.exp(fx)))
    return g, kk, _silu(hq)


def _hgrn_out(o, nw, hg):
    ms = jnp.mean(o * o, axis=-1, keepdims=True)
    return (o * lax.rsqrt(ms + EPS) * nw) * _silu(hg)


def _hgrn_prompt_kernel(hq_ref, hf_ref, hi_ref, hg_ref, lbc_ref, nw_ref, mall_ref, mask_ref,
                        o_ref, s_ref, st_scr, *, hb, c, nlev):
    ci = pl.program_id(2)

    @pl.when(ci == 0)
    def _():
        st_scr[...] = jnp.zeros_like(st_scr)

    for h0 in range(0, hb, 2):
        pair = [h2 for h2 in (h0, h0 + 1) if h2 < hb]
        sls = [slice(h2 * LANE, (h2 + 1) * LANE) for h2 in pair]
        gts = [_hgrn_gates(hf_ref[:, s2], hq_ref[:, s2], lbc_ref[:, s2]) for s2 in sls]
        qs, kks = [g[2] for g in gts], [g[1] for g in gts]
        halves = [_split2(g[0] * LOG2E) for g in gts]
        gmat = jnp.concatenate([jnp.concatenate([hl[0] for hl in halves], axis=1),
                                jnp.concatenate([hl[1] for hl in halves], axis=1)], axis=0)
        ex_pair = _dot(mall_ref[...], gmat)
        exs = [ex_pair[:, i * LANE:(i + 1) * LANE] for i in range(len(pair))]

        def pair_scores(qls, kls):
            ql = jnp.concatenate([x.astype(BF16) for x in qls], axis=0)
            kl = jnp.concatenate([x.astype(BF16) for x in kls], axis=0)
            sc = _dot_nt(ql, kl)
            return [sc[i * c:(i + 1) * c, i * c:(i + 1) * c] for i in range(len(qls))]

        accs = [s * mask_ref[0] for s in pair_scores(qs, kks)]
        for lv in range(nlev):
            ws = [jnp.exp2(ex[lv * c:(lv + 1) * c]) for ex in exs]
            scs = pair_scores([q * w for q, w in zip(qs, ws)], [kk * w for kk, w in zip(kks, ws)])
            accs = [a + s * mask_ref[lv + 1] for a, s in zip(accs, scs)]
        for hh, sl, q, kk, ex, a in zip(pair, sls, qs, kks, exs, accs):
            vb = hi_ref[:, sl].astype(BF16)
            bcum = ex[nlev * c:(nlev + 1) * c]
            erev = bcum[c - 1:c, :] - bcum
            st = st_scr[hh]
            o = _dot(a.astype(BF16), vb) + _dot_nt((q * jnp.exp2(bcum)).astype(BF16), st.astype(BF16))
            st_new = st * jnp.exp2(bcum[c - 1:c, :]) + _dot_tn(vb, (kk * jnp.exp2(erev)).astype(BF16))
            st_scr[hh] = st_new
            o_ref[:, sl] = _hgrn_out(o, nw_ref[:, sl], hg_ref[:, sl]).astype(o_ref.dtype)

    @pl.when(ci == pl.num_programs(2) - 1)
    def _():
        for hh in range(hb):
            s_ref[0, hh] = st_scr[hh].T


def _hgrn_prompt(h, offs, lbc, nw, b, t, heads, name):
    c = HG_CHUNK
    hb = _pick(heads * LANE, (8 * LANE, 4 * LANE, 2 * LANE, LANE),
               *(offs[k] for k in ("hq", "hf", "hi", "hg"))) // LANE
    w = hb * LANE
    nlev = int(math.log2(c))
    mall, masks = _hgrn_consts(c)
    nc = t // c

    def hspec(off):
        return pl.BlockSpec((c, w), lambda bi, hi, ci: (bi * nc + ci, off // w + hi))

    return pl.pallas_call(
        functools.partial(_hgrn_prompt_kernel, hb=hb, c=c, nlev=nlev),
        out_shape=(jax.ShapeDtypeStruct((b * t, heads * LANE), BF16),
                   jax.ShapeDtypeStruct((b, heads, LANE, LANE), F32)),
        grid=(b, heads // hb, nc),
        in_specs=[hspec(offs["hq"]), hspec(offs["hf"]), hspec(offs["hi"]), hspec(offs["hg"]),
                  pl.BlockSpec((3, w), lambda bi, hi, ci: (0, hi)),
                  pl.BlockSpec((1, w), lambda bi, hi, ci: (0, hi)),
                  pl.BlockSpec(mall.shape, lambda bi, hi, ci: (0, 0)),
                  pl.BlockSpec(masks.shape, lambda bi, hi, ci: (0, 0, 0))],
        out_specs=(pl.BlockSpec((c, w), lambda bi, hi, ci: (bi * nc + ci, hi)),
                   pl.BlockSpec((1, hb, LANE, LANE), lambda bi, hi, ci: (bi, hi, 0, 0))),
        scratch_shapes=[pltpu.VMEM((hb, LANE, LANE), F32)],
        compiler_params=_cparams(("parallel", "parallel", "arbitrary"), 40 << 20),
        name=name,
    )(h, h, h, h, lbc, nw, mall, masks)


def _row_select(rows_list, n):
    width = rows_list[0].shape[1]
    ridx = lax.broadcasted_iota(jnp.int32, (n, width), 0)
    out = jnp.zeros((n, width), F32)
    for i, r in enumerate(rows_list):
        out = jnp.where(ridx == i, r, out)
    return out


def _hgrn_sample_kernel(hq_ref, hf_ref, hi_ref, hg_ref, lbc_ref, nw_ref, s0_ref, o_ref, s_ref, *, hb, ts):
    ridx = lax.broadcasted_iota(jnp.int32, (SROWS, LANE), 0)
    real = ridx < ts
    for hh in range(hb):
        sl = slice(hh * LANE, (hh + 1) * LANE)
        g, kk, q = _hgrn_gates(hf_ref[:, sl], hq_ref[:, sl], lbc_ref[:, sl])
        v = hi_ref[:, sl]
        g = jnp.where(real, g, 0.0)
        kk = jnp.where(real, kk, 0.0)
        brow = []
        for i in range(ts):
            brow.append(g[i:i + 1] if i == 0 else brow[-1] + g[i:i + 1])
        s0 = s0_ref[0, hh]
        bt = _row_select(brow, SROWS)
        o_inter = _dot(q * jnp.exp(bt), s0)
        orow = []
        for i in range(ts):
            acc = jnp.zeros((1, LANE), F32)
            for s in range(i + 1):
                wgt = jnp.sum(q[i:i + 1] * kk[s:s + 1] * jnp.exp(brow[i] - brow[s]), axis=-1, keepdims=True)
                acc = acc + wgt * v[s:s + 1]
            orow.append(acc)
        o = o_inter + _row_select(orow, SROWS)
        o_ref[:, sl] = _hgrn_out(o, nw_ref[:, sl], hg_ref[:, sl]).astype(o_ref.dtype)
        cols = [jnp.exp(brow[-1])] + [kk[s:s + 1] * jnp.exp(brow[-1] - brow[s]) for s in range(ts)]
        xt = jnp.concatenate([_row_select(cols, SROWS), jnp.zeros((LANE - SROWS, LANE), F32)], axis=0).T
        s_new = xt[:, 0:1] * s0
        for s in range(ts):
            s_new = s_new + xt[:, 1 + s:2 + s] * v[s:s + 1]
        s_ref[0, hh] = s_new


def _hgrn_sample(h, offs, lbc, nw, s0_all, layer, row0, bs, ts, heads, name):
    hb = _pick(heads * LANE, (8 * LANE, 4 * LANE, 2 * LANE, LANE),
               *(offs[k] for k in ("hq", "hf", "hi", "hg"))) // LANE
    w = hb * LANE
    rb0 = row0 // SROWS

    def hspec(off):
        return pl.BlockSpec((SROWS, w), lambda bi, hi: (rb0 + bi, off // w + hi))

    sspec = pl.BlockSpec((1, hb, LANE, LANE), lambda bi, hi: (bi, hi, 0, 0))
    s0spec = pl.BlockSpec((None, 1, hb, LANE, LANE), lambda bi, hi: (layer, bi, hi, 0, 0))
    return pl.pallas_call(
        functools.partial(_hgrn_sample_kernel, hb=hb, ts=ts),
        out_shape=(jax.ShapeDtypeStruct((bs * SROWS, heads * LANE), BF16),
                   jax.ShapeDtypeStruct((bs, heads, LANE, LANE), F32)),
        grid=(bs, heads // hb),
        in_specs=[hspec(offs["hq"]), hspec(offs["hf"]), hspec(offs["hi"]), hspec(offs["hg"]),
                  pl.BlockSpec((3, w), lambda bi, hi: (0, hi)),
                  pl.BlockSpec((1, w), lambda bi, hi: (0, hi)),
                  s0spec],
        out_specs=(pl.BlockSpec((SROWS, w), lambda bi, hi: (bi, hi)), sspec),
        compiler_params=_cparams(("parallel", "parallel")),
        name=name,
    )(h, h, h, h, lbc, nw, s0_all)


def _ssd_finish(ys, xs, zs, dsk, nw_ref, o_ref, width, col0=0):
    gated = [(y + d * x) * _silu(z) for y, x, z, d in zip(ys, xs, zs, dsk)]
    ssq = sum(jnp.sum(t * t, axis=-1, keepdims=True) for t in gated)
    r = lax.rsqrt(ssq / width + EPS)
    for p, t in enumerate(gated):
        sl = slice(col0 + p * LANE, col0 + (p + 1) * LANE)
        o_ref[:, sl] = (t * r * nw_ref[:, sl]).astype(o_ref.dtype)


def _conv_chunk(u_ref, carry, w_ref, b_ref, rows):
    u, act = _conv_tile(u_ref, carry, w_ref, b_ref, rows)
    carry[...] = u[rows - SROWS:rows]
    return act


def _ssd_prompt_kernel(xr_ref, br_ref, cr_ref, z_ref, dt_ref, dtt_ref, pr_ref, pc_ref, nw_ref, tri_ref, triu_ref,
                       wx_ref, wb_ref, wc_ref, bx_ref, bb_ref, bc_ref,
                       o_ref, s_ref, ht_scr, x_scr, cx_scr, cb_scr, cc_scr, *, c, hpg, pdim):
    ci = pl.program_id(2)
    npair = hpg * pdim // LANE
    hpp = LANE // pdim

    @pl.when(ci == 0)
    def _():
        ht_scr[...] = jnp.zeros_like(ht_scr)
        cx_scr[...] = jnp.zeros_like(cx_scr)
        cb_scr[...] = jnp.zeros_like(cb_scr)
        cc_scr[...] = jnp.zeros_like(cc_scr)

    x_scr[...] = _conv_chunk(xr_ref, cx_scr, wx_ref, bx_ref, c)
    x_ref = x_scr
    b_act = _conv_chunk(br_ref, cb_scr, wb_ref, bb_ref, c)
    c_act = _conv_chunk(cr_ref, cc_scr, wc_ref, bc_ref, c)

    pr = pr_ref[0]
    pc = pc_ref[0]
    dt_c = _softplus(dt_ref[0] + pr[0:1])
    da_c = dt_c * pr[1:2]
    da_r = _softplus(dtt_ref[0] + pc[:, 0:1]) * pc[:, 1:2]
    tri = tri_ref[...]
    triu = triu_ref[...]
    cum_c = sum(_dot(tri, p.astype(F32)) for p in _split3(da_c))
    cum_r = sum(_dot(p.astype(F32), triu) for p in _split3(da_r))
    bm = b_act.astype(BF16)
    cm = c_act.astype(BF16)
    cb = _dot_nt(cm, bm)
    trow = lax.broadcasted_iota(jnp.int32, (c, c), 0)
    scol = lax.broadcasted_iota(jnp.int32, (c, c), 1)
    causal = trow >= scol
    lane = lax.broadcasted_iota(jnp.int32, (1, LANE), 1)

    def per_head(vals):
        out = vals[-1]
        for k in range(hpp - 2, -1, -1):
            out = jnp.where(lane < (k + 1) * pdim, vals[k], out)
        return out

    ch_all = _dot(cm, jnp.concatenate([ht_scr[p].astype(BF16) for p in range(npair)], axis=1))
    ys, xs, zs, dsk, upd, hdec = [], [], [], [], [], []
    for p in range(npair):
        sl = slice(p * LANE, (p + 1) * LANE)
        js = [p * hpp + k for k in range(hpp)]
        xp = x_ref[:, sl]
        xdt = xp * per_head([dt_c[:, j:j + 1] for j in js])
        xdtb = xdt.astype(BF16)
        mj = []
        for j in js:
            seg = jnp.exp(jnp.where(causal, cum_c[:, j:j + 1] - cum_r[j:j + 1, :], MASK_VALUE))
            mj.append((cb * seg).astype(BF16))
        yy = _dot(jnp.concatenate(mj, axis=0), xdtb)
        y = per_head([yy[k * c:(k + 1) * c] for k in range(hpp)])
        y = y + ch_all[:, sl] * per_head([jnp.exp(cum_c[:, j:j + 1]) for j in js])
        last = [cum_c[c - 1:c, j:j + 1] for j in js]
        wgt = per_head([jnp.exp(l - cum_c[:, j:j + 1]) for l, j in zip(last, js)])
        upd.append((xdt * wgt).astype(BF16))
        hdec.append(per_head([jnp.exp(l) for l in last]))
        ys.append(y)
        xs.append(xp)
        zs.append(z_ref[:, sl])
        dsk.append(per_head([pr[2:3, j:j + 1] for j in js]))
    ht_add = _dot_tn(bm, jnp.concatenate(upd, axis=1))
    for p in range(npair):
        ht_scr[p] = ht_scr[p] * hdec[p] + ht_add[:, p * LANE:(p + 1) * LANE]

    _ssd_finish(ys, xs, zs, dsk, nw_ref, o_ref, hpg * pdim)

    @pl.when(ci == pl.num_programs(2) - 1)
    def _():
        for p in range(npair):
            s_ref[0, p * LANE:(p + 1) * LANE, :] = ht_scr[p].T


def _ssd_prompt(h, off_xbc, off_z, conv_w, conv_b, mdt_g, mdt_gt, prm_r, prm_c, nw, b, t, groups, hpg, pdim, dstate,
                name):
    c = SSD_CHUNK
    gw = hpg * pdim
    dinner = groups * gw
    nc = t // c
    npair = gw // LANE
    assert off_xbc % gw == 0 and (off_xbc + dinner) % dstate == 0 and c >= SROWS
    xb0 = off_xbc // gw
    nb0 = dinner // dstate
    hb0 = (off_xbc + dinner) // dstate
    tt = np.arange(c)
    tri = jnp.asarray((tt[None, :] <= tt[:, None]).astype(np.float32))
    rows = lambda bi, ci: bi * nc + ci
    const = lambda bi, gi, ci: (0, 0)
    return pl.pallas_call(
        functools.partial(_ssd_prompt_kernel, c=c, hpg=hpg, pdim=pdim),
        out_shape=(jax.ShapeDtypeStruct((b * t, dinner), BF16),
                   jax.ShapeDtypeStruct((b, groups * gw, dstate), F32)),
        grid=(b, groups, nc),
        in_specs=[pl.BlockSpec((c, gw), lambda bi, gi, ci: (rows(bi, ci), xb0 + gi)),
                  pl.BlockSpec((c, dstate), lambda bi, gi, ci: (rows(bi, ci), hb0 + gi)),
                  pl.BlockSpec((c, dstate), lambda bi, gi, ci: (rows(bi, ci), hb0 + groups + gi)),
                  pl.BlockSpec((c, gw), lambda bi, gi, ci: (rows(bi, ci), off_z // gw + gi)),
                  pl.BlockSpec((1, c, hpg), lambda bi, gi, ci: (gi, rows(bi, ci), 0)),
                  pl.BlockSpec((1, hpg, c), lambda bi, gi, ci: (gi, 0, rows(bi, ci))),
                  pl.BlockSpec((1, 3, hpg), lambda bi, gi, ci: (gi, 0, 0)),
                  pl.BlockSpec((1, hpg, 3), lambda bi, gi, ci: (gi, 0, 0)),
                  pl.BlockSpec((1, gw), lambda bi, gi, ci: (0, gi)),
                  pl.BlockSpec((c, c), const),
                  pl.BlockSpec((c, c), const),
                  pl.BlockSpec((CONV_WIDTH, gw), lambda bi, gi, ci: (0, gi)),
                  pl.BlockSpec((CONV_WIDTH, dstate), lambda bi, gi, ci: (0, nb0 + gi)),
                  pl.BlockSpec((CONV_WIDTH, dstate), lambda bi, gi, ci: (0, nb0 + groups + gi)),
                  pl.BlockSpec((1, gw), lambda bi, gi, ci: (0, gi)),
                  pl.BlockSpec((1, dstate), lambda bi, gi, ci: (0, nb0 + gi)),
                  pl.BlockSpec((1, dstate), lambda bi, gi, ci: (0, nb0 + groups + gi))],
        out_specs=(pl.BlockSpec((c, gw), lambda bi, gi, ci: (rows(bi, ci), gi)),
                   pl.BlockSpec((1, gw, dstate), lambda bi, gi, ci: (bi, gi, 0))),
        scratch_shapes=[pltpu.VMEM((npair, dstate, LANE), F32), pltpu.VMEM((c, gw), F32),
                        pltpu.VMEM((SROWS, gw), F32), pltpu.VMEM((SROWS, dstate), F32),
                        pltpu.VMEM((SROWS, dstate), F32)],
        compiler_params=_cparams(("parallel", "parallel", "arbitrary"), 40 << 20),
        name=name,
    )(h, h, h, h, mdt_g, mdt_gt, prm_r, prm_c, nw, tri, tri.T, conv_w, conv_w, conv_w, conv_b, conv_b, conv_b)


def _ssd_sample_kernel(x_ref, b_ref, c_ref, z_ref, dtx_ref, px_ref, nw_ref, s0_ref, o_ref, s_ref, *, ts, gw, ngb):
    npair = gw // LANE
    dstate = b_ref.shape[1] // ngb
    ridx = lax.broadcasted_iota(jnp.int32, (SROWS, ngb * gw), 0)
    dt = jnp.where(ridx < ts, _softplus(dtx_ref[...] + px_ref[0:1, :]), 0.0)
    dec = jnp.exp(dt * px_ref[1:2, :])
    x = x_ref[...]
    xdt = x * dt
    pad = jnp.zeros((LANE - SROWS, dstate), F32)
    for gs in range(ngb):
        nsl = slice(gs * dstate, (gs + 1) * dstate)
        bt = jnp.concatenate([b_ref[:, nsl], pad], axis=0).T
        ct = jnp.concatenate([c_ref[:, nsl], pad], axis=0).T
        ys, xs, zs, dsk = [], [], [], []
        for p in range(npair):
            sl = slice(gs * gw + p * LANE, gs * gw + (p + 1) * LANE)
            ht = s0_ref[0, sl, :].T
            yrow = []
            for i in range(ts):
                ht = ht * dec[i:i + 1, sl] + bt[:, i:i + 1] * xdt[i:i + 1, sl]
                yrow.append(jnp.sum(ht * ct[:, i:i + 1], axis=0, keepdims=True))
            s_ref[0, sl, :] = ht.T
            ys.append(_row_select(yrow, SROWS))
            xs.append(x[:, sl])
            zs.append(z_ref[:, sl])
            dsk.append(px_ref[2:3, sl])
        _ssd_finish(ys, xs, zs, dsk, nw_ref, o_ref, gw, col0=gs * gw)


def _ssd_sample(xa_s, h, off_z, mdt_x, prm_x, nw, s0_all, layer, row0, bs, ts, groups, gw, dstate, name):
    dinner = groups * gw
    rb0 = row0 // SROWS
    nb0 = dinner // dstate
    ngb = _pick(groups, (4, 2, 1), nb0, nb0 + groups, off_z // gw)
    w = ngb * gw
    sspec = pl.BlockSpec((1, w, dstate), lambda bi, gi: (bi, gi, 0))
    s0spec = pl.BlockSpec((None, 1, w, dstate), lambda bi, gi: (layer, bi, gi, 0))
    return pl.pallas_call(
        functools.partial(_ssd_sample_kernel, ts=ts, gw=gw, ngb=ngb),
        out_shape=(jax.ShapeDtypeStruct((bs * SROWS, dinner), BF16),
                   jax.ShapeDtypeStruct((bs, groups * gw, dstate), F32)),
        grid=(bs, groups // ngb),
        in_specs=[pl.BlockSpec((SROWS, w), lambda bi, gi: (bi, gi)),
                  pl.BlockSpec((SROWS, ngb * dstate), lambda bi, gi: (bi, nb0 // ngb + gi)),
                  pl.BlockSpec((SROWS, ngb * dstate), lambda bi, gi: (bi, (nb0 + groups) // ngb + gi)),
                  pl.BlockSpec((SROWS, w), lambda bi, gi: (rb0 + bi, off_z // w + gi)),
                  pl.BlockSpec((SROWS, w), lambda bi, gi: (bi, gi)),
                  pl.BlockSpec((3, w), lambda bi, gi: (0, gi)),
                  pl.BlockSpec((1, w), lambda bi, gi: (0, gi)),
                  s0spec],
        out_specs=(pl.BlockSpec((SROWS, w), lambda bi, gi: (bi, gi)), sspec),
        compiler_params=_cparams(("parallel", "parallel")),
        name=name,
    )(xa_s, xa_s, xa_s, h, mdt_x, prm_x, nw, s0_all)


def _rot(x, cos, sin):
    return x * cos + pltpu.roll(x, LANE // 2, 1) * sin


def _residue_perm(dil, rows):
    i = np.arange(rows)
    p = np.zeros((rows, rows), np.float32)
    p[(i % dil) * (rows // dil) + i // dil, i] = 1.0
    return p


def _rope_prompt_kernel(q_ref, k_ref, v_ref, cos_ref, sin_ref, p_ref, qd_ref, kd_ref, vd_ref, ks_ref,
                        *, nh, dil, rows):
    n = rows // dil
    cos = cos_ref[...]
    sin = sin_ref[...]

    def emit(xb, dst_ref, sl):
        y = _dot(p_ref[...], xb).astype(BF16) if dil > 1 else xb
        for r in range(dil):
            dst_ref[0, r, :, sl] = y[r * n:(r + 1) * n]

    for hh in range(nh):
        sl = slice(hh * LANE, (hh + 1) * LANE)
        emit(_rot(q_ref[:, sl], cos, sin).astype(BF16), qd_ref, sl)
        kr = _rot(k_ref[:, sl], cos, sin)
        ks_ref[:, sl] = kr
        emit(kr.astype(BF16), kd_ref, sl)
        emit(v_ref[:, sl].astype(BF16), vd_ref, sl)


def _rope_prompt(h, offs, gi, dil, b, t, hpg, name):
    ow = hpg * LANE
    rows = _pick(t, (256, 128, 64, 32, 16))
    assert rows % (2 * SROWS * dil) == 0 and all(offs[k] % ow == 0 for k in ("aq", "ak", "av"))
    l, nb, n = t // dil, t // rows, rows // dil
    half = LANE // 2
    inv = ROPE_THETA ** (-jnp.arange(half, dtype=F32) / half)
    ang = jnp.arange(t, dtype=F32)[:, None] * inv[None, :]
    cos2 = jnp.concatenate([jnp.cos(ang), jnp.cos(ang)], axis=-1)
    sin2 = jnp.concatenate([-jnp.sin(ang), jnp.sin(ang)], axis=-1)
    perm = jnp.asarray(_residue_perm(dil, rows), BF16)

    def hspec(key):
        cb = offs[key] // ow + gi
        return pl.BlockSpec((rows, ow), lambda bi, j: (bi * nb + j, cb))

    tspec = pl.BlockSpec((rows, LANE), lambda bi, j: (j, 0))
    dspec = pl.BlockSpec((1, dil, n, ow), lambda bi, j: (bi, 0, j, 0))
    dshape = jax.ShapeDtypeStruct((b, dil, l, ow), BF16)
    return pl.pallas_call(
        functools.partial(_rope_prompt_kernel, nh=hpg, dil=dil, rows=rows),
        out_shape=(dshape, dshape, dshape, jax.ShapeDtypeStruct((b * t, ow), F32)),
        grid=(b, nb),
        in_specs=[hspec("aq"), hspec("ak"), hspec("av"), tspec, tspec,
                  pl.BlockSpec((rows, rows), lambda bi, j: (0, 0))],
        out_specs=(dspec, dspec, dspec, pl.BlockSpec((rows, ow), lambda bi, j: (bi * nb + j, 0))),
        compiler_params=_cparams(("parallel", "parallel")),
        name=name,
    )(h, h, h, cos2, sin2, perm)


def _rope_sample_kernel(q_ref, k_ref, cos_ref, sin_ref, qo_ref, ko_ref, *, nh):
    cos = cos_ref[...]
    sin = sin_ref[...]
    for hh in range(nh):
        sl = slice(hh * LANE, (hh + 1) * LANE)
        qo_ref[:, sl] = _rot(q_ref[:, sl], cos, sin)
        ko_ref[:, sl] = _rot(k_ref[:, sl], cos, sin)


def _rope_sample(h, offs, adim, row0, nrows, ts, name):
    tr = _pick(nrows, (256, 128, 64, 32, 16, 8), row0)
    tc = _pick(adim, (1024, 512, 256, 128), offs["aq"], offs["ak"])
    half = LANE // 2
    inv = ROPE_THETA ** (-jnp.arange(half, dtype=F32) / half)
    srow = jnp.arange(SROWS)
    pos = jnp.tile(jnp.where(srow < ts, PAST_LEN + srow, 0), nrows // SROWS).astype(F32)
    ang = pos[:, None] * inv[None, :]
    cos2 = jnp.concatenate([jnp.cos(ang), jnp.cos(ang)], axis=1)
    sin2 = jnp.concatenate([-jnp.sin(ang), jnp.sin(ang)], axis=1)
    rb0 = row0 // tr

    def hspec(off):
        return pl.BlockSpec((tr, tc), lambda i, j: (rb0 + i, off // tc + j))

    ospec = pl.BlockSpec((tr, tc), lambda i, j: (i, j))
    tspec = pl.BlockSpec((tr, LANE), lambda i, j: (i, 0))
    return pl.pallas_call(
        functools.partial(_rope_sample_kernel, nh=tc // LANE),
        out_shape=(jax.ShapeDtypeStruct((nrows, adim), F32), jax.ShapeDtypeStruct((nrows, adim), F32)),
        grid=(nrows // tr, adim // tc),
        in_specs=[hspec(offs["aq"]), hspec(offs["ak"]), tspec, tspec],
        out_specs=(ospec, ospec),
        compiler_params=_cparams(("parallel", "parallel")),
        name=name,
    )(h, h, cos2, sin2)


def _attn_prompt_kernel(q_ref, kp_ref, kc_ref, vp_ref, vc_ref, o_ref, l_ref, s_scr, p_scr, *, qb, nh, scale):
    has_prev = pl.program_id(2) > 0
    row = lax.broadcasted_iota(jnp.int32, (qb, 2 * qb), 0)
    col = lax.broadcasted_iota(jnp.int32, (qb, 2 * qb), 1)
    in_prev = jnp.logical_and(jnp.logical_and(col < qb, col >= row), has_prev)
    valid = jnp.logical_or(in_prev, jnp.logical_and(col >= qb, col - qb <= row))
    for hh in range(nh):
        sl = slice(hh * LANE, (hh + 1) * LANE)
        kcat = jnp.concatenate([kp_ref[:, sl], kc_ref[:, sl]], axis=0)
        s_scr[hh] = _dot_nt(q_ref[:, sl], kcat)
    for hh in range(nh):
        sl = slice(hh * LANE, (hh + 1) * LANE)
        s = jnp.where(valid, s_scr[hh] * scale, MASK_VALUE)
        m = jnp.max(s, axis=-1, keepdims=True)
        p = jnp.exp(s - m)
        den = jnp.sum(p, axis=-1, keepdims=True)
        p_scr[hh] = (p * (1.0 / den)).astype(BF16)
        l_ref[:, sl] = jnp.broadcast_to(m + jnp.log(den), (qb, LANE))
    for hh in range(nh):
        sl = slice(hh * LANE, (hh + 1) * LANE)
        vcat = jnp.concatenate([vp_ref[:, sl], vc_ref[:, sl]], axis=0)
        o_ref[:, sl] = _dot(p_scr[hh], vcat)


def _attn_prompt(qd, kd, vd, window, dil, name):
    b, _, l, ow = qd.shape
    qb = window // dil
    nq = l // qb
    cur = pl.BlockSpec((None, None, qb, ow), lambda bi, r, i: (bi, r, i, 0))
    prev = pl.BlockSpec((None, None, qb, ow), lambda bi, r, i: (bi, r, jnp.maximum(i - 1, 0), 0))
    oshape = jax.ShapeDtypeStruct((b, dil, l, ow), F32)
    return pl.pallas_call(
        functools.partial(_attn_prompt_kernel, qb=qb, nh=ow // LANE, scale=LANE ** -0.5),
        out_shape=(oshape, oshape),
        grid=(b, dil, nq),
        in_specs=[cur, prev, cur, prev, cur],
        out_specs=(cur, cur),
        scratch_shapes=[pltpu.VMEM((ow // LANE, qb, 2 * qb), F32), pltpu.VMEM((ow // LANE, qb, 2 * qb), BF16)],
        compiler_params=_cparams(("parallel", "parallel", "arbitrary")),
        name=name,
    )(qd, kd, kd, vd, vd)


def _attn_sample_kernel(q_ref, kn_ref, vn_ref, kc_ref, vc_ref, o_ref, l_ref, *, ts, window, dil, scale):
    nk = window // dil
    o_ref[...] = jnp.zeros_like(o_ref)
    l_ref[...] = jnp.zeros_like(l_ref)
    arow = lax.broadcasted_iota(jnp.int32, (nk, 1, 1), 0)
    nrow = lax.broadcasted_iota(jnp.int32, (SROWS, 1, 1), 0)
    kn = kn_ref[...]
    vn = vn_ref[...]
    for i in range(ts):
        rho = (window + i) % dil
        base = (window + i - rho) // dil
        j0 = i // dil + 1
        a_lo, a_hi = max(base - nk, 0), min(base - j0, nk - 1)
        vcache = jnp.logical_and(arow >= a_lo, arow <= a_hi)
        new_rows = [i - j * dil for j in range(i // dil + 1)]
        vnew = functools.reduce(jnp.logical_or, [nrow == r for r in new_rows])
        q = q_ref[i][None]
        kc = kc_ref[:, rho]
        sc = jnp.where(vcache, jnp.sum(kc * q, axis=-1, keepdims=True) * scale, MASK_VALUE)
        sn = jnp.where(vnew, jnp.sum(kn * q, axis=-1, keepdims=True) * scale, MASK_VALUE)
        m = jnp.maximum(jnp.max(sc, axis=0, keepdims=True), jnp.max(sn, axis=0, keepdims=True))
        pc = jnp.where(vcache, jnp.exp(sc - m), 0.0)
        pn = jnp.where(vnew, jnp.exp(sn - m), 0.0)
        den = jnp.sum(pc, axis=0, keepdims=True) + jnp.sum(pn, axis=0, keepdims=True)
        acc = jnp.sum(pc * vc_ref[:, rho], axis=0) + jnp.sum(pn * vn, axis=0)
        o_ref[i] = acc / den[0]
        l_ref[i] = jnp.broadcast_to(m[0] + jnp.log(den[0]), acc.shape)


def _attn_sample(q3, k3, v3, ck_all, cv_all, layer, window, dil, bs, ts, name):
    hpg = q3.shape[1]
    nk = window // dil
    assert ck_all.shape[2] == window and window % dil == 0 and (dil == 1 or dil >= ts)
    nres = min(dil, ts)
    shape6 = ck_all.shape[:2] + (nk, dil, hpg, LANE)
    nspec = pl.BlockSpec((SROWS, hpg, LANE), lambda bi: (bi, 0, 0))
    cspec = pl.BlockSpec((None, None, nk, nres, hpg, LANE), lambda bi: (layer, bi, 0, 0, 0, 0))
    oshape = jax.ShapeDtypeStruct((bs * SROWS, hpg, LANE), F32)
    return pl.pallas_call(
        functools.partial(_attn_sample_kernel, ts=ts, window=window, dil=dil, scale=LANE ** -0.5),
        out_shape=(oshape, oshape),
        grid=(bs,),
        in_specs=[nspec, nspec, nspec, cspec, cspec],
        out_specs=(nspec, nspec),
        compiler_params=_cparams(("parallel",), 40 << 20),
        name=name,
    )(q3, k3, v3, ck_all.reshape(shape6), cv_all.reshape(shape6))


def _combine_kernel(*refs):
    ng = (len(refs) - 1) // 2
    os_, ls_, out = refs[:ng], refs[ng:2 * ng], refs[-1]
    out[...] = _mix_groups([o[...] for o in os_], [l[...] for l in ls_]).astype(out.dtype)


def _mix_groups(os_, ls):
    m = functools.reduce(jnp.maximum, ls)
    ws = [jnp.exp(l - m) for l in ls]
    den = functools.reduce(lambda a, b: a + b, ws)
    acc = functools.reduce(lambda a, b: a + b, [w * o for w, o in zip(ws, os_)])
    return acc / den


def _combine_prompt_kernel(*refs, dils):
    ng = len(dils)
    o_refs, l_refs, p_refs, out = refs[:ng], refs[ng:2 * ng], refs[2 * ng:3 * ng], refs[3 * ng]

    def token_order(ref, p_ref, d):
        x = jnp.concatenate([ref[0, r] for r in range(d)], axis=0)
        if d == 1:
            return x
        return sum(_dot(p_ref[...], piece) for piece in _split3(x))

    os_ = [token_order(o, p, d) for o, p, d in zip(o_refs, p_refs, dils)]
    ls = [token_order(l, p, d) for l, p, d in zip(l_refs, p_refs, dils)]
    out[...] = _mix_groups(os_, ls).astype(out.dtype)


def _combine_prompt(outs, lses, dils, name):
    b, _, _, ow = outs[0].shape
    t = outs[0].shape[1] * outs[0].shape[2]
    rows = _pick(t, (256, 128, 64, 32, 16))
    assert all(rows % (SROWS * d) == 0 for d in dils)
    nb = t // rows
    specs = [pl.BlockSpec((1, d, rows // d, ow), lambda bi, j: (bi, 0, j, 0)) for d in dils]
    perms = [jnp.asarray(_residue_perm(d, rows).T, BF16) for d in dils]
    pspec = pl.BlockSpec((rows, rows), lambda bi, j: (0, 0))
    return pl.pallas_call(
        functools.partial(_combine_prompt_kernel, dils=tuple(dils)),
        out_shape=jax.ShapeDtypeStruct((b * t, ow), BF16),
        grid=(b, nb),
        in_specs=specs + specs + [pspec] * len(dils),
        out_specs=pl.BlockSpec((rows, ow), lambda bi, j: (bi * nb + j, 0)),
        compiler_params=_cparams(("parallel", "parallel"), 40 << 20),
        name=name,
    )(*outs, *lses, *perms)


def _combine(outs, lses, name):
    n, ow = outs[0].shape
    tr = _pick(n, (512, 256, 128, 64, 8))
    spec = pl.BlockSpec((tr, ow), lambda i: (i, 0))
    return pl.pallas_call(
        _combine_kernel,
        out_shape=jax.ShapeDtypeStruct((n, ow), BF16),
        grid=(n // tr,),
        in_specs=[spec] * (2 * len(outs)),
        out_specs=spec,
        compiler_params=_cparams(("parallel",)),
        name=name,
    )(*outs, *lses)


def _merge_kernel(oa_ref, yb_ref, oc_ref, wa_ref, wb_ref, wc_ref, ga_ref, gb_ref, gc_ref, o_ref):
    acc = _sigmoid(ga_ref[...]) * _dot(oa_ref[...], wa_ref[...])
    acc = acc + _sigmoid(gb_ref[...]) * _dot(yb_ref[...], wb_ref[...])
    acc = acc + _sigmoid(gc_ref[...]) * _dot(oc_ref[...], wc_ref[...])
    o_ref[...] = acc.astype(o_ref.dtype)


def _merge(oa, yb, oc, wa, wb, wc, h, off_g, d, name):
    n = oa.shape[0]
    tm = _pick(n, (384, 256, 128, 88, 64, 8))
    tn = _pick(d, (512, 256, 128), off_g)
    ka, kb, kc = oa.shape[1], yb.shape[1], oc.shape[1]

    def gspec(k):
        return pl.BlockSpec((tm, tn), lambda j, i: (i, (off_g + k * d) // tn + j))

    return pl.pallas_call(
        _merge_kernel,
        out_shape=jax.ShapeDtypeStruct((n, d), BF16),
        grid=(d // tn, n // tm),
        in_specs=[pl.BlockSpec((tm, ka), lambda j, i: (i, 0)),
                  pl.BlockSpec((tm, kb), lambda j, i: (i, 0)),
                  pl.BlockSpec((tm, kc), lambda j, i: (i, 0)),
                  pl.BlockSpec((ka, tn), lambda j, i: (0, j)),
                  pl.BlockSpec((kb, tn), lambda j, i: (0, j)),
                  pl.BlockSpec((kc, tn), lambda j, i: (0, j)),
                  gspec(0), gspec(1), gspec(2)],
        out_specs=pl.BlockSpec((tm, tn), lambda j, i: (i, j)),
        compiler_params=_cparams(("parallel", "parallel"), 48 << 20),
        name=name,
    )(oa, yb, oc, wa, wb, wc, h, h, h)


def _layer_norm(y, g, b):
    mu = jnp.mean(y, axis=-1, keepdims=True)
    yc = y - mu
    var = jnp.mean(yc * yc, axis=-1, keepdims=True)
    return yc * lax.rsqrt(var + EPS) * g + b


def _outln_kernel(m_ref, w_ref, x_ref, g_ref, b_ref, o_ref, ob_ref, *, alpha):
    y = alpha * x_ref[...] + _dot(m_ref[...], w_ref[...])
    out = _layer_norm(y, g_ref[...], b_ref[...])
    o_ref[...] = out
    ob_ref[...] = out.astype(ob_ref.dtype)


def _outln(merged, w_out, x, g, b, alpha, name):
    n, d = x.shape
    tm = _pick(n, (384, 256, 128, 88, 64, 8))
    rspec = pl.BlockSpec((tm, d), lambda i: (i, 0))
    vspec = pl.BlockSpec((1, d), lambda i: (0, 0))
    return pl.pallas_call(
        functools.partial(_outln_kernel, alpha=alpha),
        out_shape=(jax.ShapeDtypeStruct((n, d), F32), jax.ShapeDtypeStruct((n, d), BF16)),
        grid=(n // tm,),
        in_specs=[rspec, pl.BlockSpec((d, d), lambda i: (0, 0)), rspec, vspec, vspec],
        out_specs=(rspec, rspec),
        compiler_params=_cparams(("parallel",), 48 << 20),
        name=name,
    )(merged, w_out, x, g, b)


def _resln_kernel(x_ref, f_ref, g_ref, b_ref, o_ref, ob_ref, *, alpha):
    out = _layer_norm(alpha * x_ref[...] + f_ref[...], g_ref[...], b_ref[...])
    o_ref[...] = out
    ob_ref[...] = out.astype(ob_ref.dtype)


def _resln(x, f, g, b, alpha, name):
    n, d = x.shape
    tm = _pick(n, (384, 256, 128, 88, 64, 8))
    rspec = pl.BlockSpec((tm, d), lambda i: (i, 0))
    vspec = pl.BlockSpec((1, d), lambda i: (0, 0))
    return pl.pallas_call(
        functools.partial(_resln_kernel, alpha=alpha),
        out_shape=(jax.ShapeDtypeStruct((n, d), F32), jax.ShapeDtypeStruct((n, d), BF16)),
        grid=(n // tm,),
        in_specs=[rspec, rspec, vspec, vspec],
        out_specs=(rspec, rspec),
        compiler_params=_cparams(("parallel",)),
        name=name,
    )(x, f, g, b)


def _ffn_a_kernel(te_ref, x_ref, w1_ref, w3_ref, o_ref):
    used = pl.program_id(1) < te_ref[pl.num_programs(1)]

    @pl.when(used)
    def _():
        x = x_ref[...].astype(BF16)
        a = _dot(x, w1_ref[0])
        o_ref[...] = (_silu(a) * _dot(x, w3_ref[0])).astype(o_ref.dtype)

    @pl.when(jnp.logical_not(used))
    def _():
        o_ref[...] = jnp.zeros_like(o_ref)


def _ffn_a(x, w1, w3, tile_expert, tm, name):
    n, d = x.shape
    f = w1.shape[2]
    tf = _pick(f, (1408, 1024, 512, 256, 128))
    xbytes = x.dtype.itemsize
    vm = 2 * (tm * d * xbytes + 2 * d * tf * 2 + tm * tf * 2) + 3 * tm * tf * 4 + (6 << 20)
    return pl.pallas_call(
        _ffn_a_kernel,
        out_shape=jax.ShapeDtypeStruct((n, f), BF16),
        grid_spec=pltpu.PrefetchScalarGridSpec(
            num_scalar_prefetch=1,
            grid=(f // tf, n // tm),
            in_specs=[pl.BlockSpec((tm, d), lambda j, i, te: (i, 0)),
                      pl.BlockSpec((1, d, tf), lambda j, i, te: (te[i], 0, j)),
                      pl.BlockSpec((1, d, tf), lambda j, i, te: (te[i], 0, j))],
            out_specs=pl.BlockSpec((tm, tf), lambda j, i, te: (i, j))),
        compiler_params=_cparams(("parallel", "arbitrary"), vm),
        name=name,
    )(tile_expert, x, w1, w3)


def _ffn_b_kernel(te_ref, h_ref, w2_ref, o_ref):
    used = pl.program_id(1) < te_ref[pl.num_programs(1)]

    @pl.when(used)
    def _():
        o_ref[...] = _dot(h_ref[...], w2_ref[0])

    @pl.when(jnp.logical_not(used))
    def _():
        o_ref[...] = jnp.zeros_like(o_ref)


def _ffn_b(hmid, w2, tile_expert, tm, name):
    n, f = hmid.shape
    d = w2.shape[2]
    tn = _pick(d, (1024, 512, 256, 128))
    vm = 2 * (tm * f * 2 + f * tn * 2 + tm * tn * 4) + (4 << 20)
    return pl.pallas_call(
        _ffn_b_kernel,
        out_shape=jax.ShapeDtypeStruct((n, d), F32),
        grid_spec=pltpu.PrefetchScalarGridSpec(
            num_scalar_prefetch=1,
            grid=(d // tn, n // tm),
            in_specs=[pl.BlockSpec((tm, f), lambda j, i, te: (i, 0)),
                      pl.BlockSpec((1, f, tn), lambda j, i, te: (te[i], 0, j))],
            out_specs=pl.BlockSpec((tm, tn), lambda j, i, te: (i, j))),
        compiler_params=_cparams(("parallel", "arbitrary"), vm),
        name=name,
    )(tile_expert, hmid, w2)


def _router_kernel(x_ref, w_ref, b_ref, e_ref, g_ref, *, n_exp):
    xs = _split3(x_ref[...])
    ws = _split3(w_ref[...])
    logits = b_ref[...]
    for i in range(3):
        for j in range(3 - i):
            logits = logits + _dot(xs[i], ws[j])
    lane = lax.broadcasted_iota(jnp.int32, logits.shape, 1)
    neg = jnp.float32(-jnp.inf)
    logits = jnp.where(lane < n_exp, logits, neg)
    m0 = jnp.max(logits, axis=-1, keepdims=True)
    i0 = jnp.min(jnp.where(logits == m0, lane, LANE), axis=-1, keepdims=True)
    rest = jnp.where(lane == i0, neg, logits)
    m1 = jnp.max(rest, axis=-1, keepdims=True)
    i1 = jnp.min(jnp.where(rest == m1, lane, LANE), axis=-1, keepdims=True)
    e1 = jnp.exp(m1 - m0)
    den = 1.0 + e1
    e_ref[...] = jnp.where(lane == 0, i0, jnp.where(lane == 1, i1, 0))
    g_ref[...] = jnp.where(lane == 0, 1.0 / den, jnp.where(lane == 1, e1 / den, 0.0))


def _router(x, w_router, b_router, name):
    n, d = x.shape
    n_exp = w_router.shape[1]
    wp = jnp.zeros((d, LANE), F32).at[:, :n_exp].set(w_router.astype(F32))
    bp = jnp.zeros((1, LANE), F32).at[0, :n_exp].set(b_router.astype(F32))
    tm = _pick(n, (384, 256, 128, 88, 64, 8))
    rspec = pl.BlockSpec((tm, LANE), lambda i: (i, 0))
    return pl.pallas_call(
        functools.partial(_router_kernel, n_exp=n_exp),
        out_shape=(jax.ShapeDtypeStruct((n, LANE), jnp.int32), jax.ShapeDtypeStruct((n, LANE), F32)),
        grid=(n // tm,),
        in_specs=[pl.BlockSpec((tm, d), lambda i: (i, 0)),
                  pl.BlockSpec((d, LANE), lambda i: (0, 0)),
                  pl.BlockSpec((1, LANE), lambda i: (0, 0))],
        out_specs=(rspec, rspec),
        compiler_params=_cparams(("parallel",)),
        name=name,
    )(x, wp, bp)


def _row_copy(src_hbm, row, dst_ref, dst_row, sem):
    return pltpu.make_async_copy(src_hbm.at[pl.ds(row, 1)], dst_ref.at[pl.ds(dst_row, 1)], sem)


def _gather_kernel(idx_ref, x_hbm, o_ref, sem, *, tm):
    base = pl.program_id(0) * tm

    def start(r, carry):
        _row_copy(x_hbm, idx_ref[base + r], o_ref, r, sem).start()
        return carry

    def wait(r, carry):
        _row_copy(x_hbm, 0, o_ref, r, sem).wait()
        return carry

    lax.fori_loop(0, tm, start, 0, unroll=8)
    lax.fori_loop(0, tm, wait, 0)


def _gather_rows(x, idx, tm, name):
    n_out = idx.shape[0]
    d = x.shape[1]
    return pl.pallas_call(
        functools.partial(_gather_kernel, tm=tm),
        out_shape=jax.ShapeDtypeStruct((n_out, d), x.dtype),
        grid_spec=pltpu.PrefetchScalarGridSpec(
            num_scalar_prefetch=1,
            grid=(n_out // tm,),
            in_specs=[pl.BlockSpec(memory_space=pl.ANY)],
            out_specs=pl.BlockSpec((tm, d), lambda i, idx_ref: (i, 0)),
            scratch_shapes=[pltpu.SemaphoreType.DMA(())]),
        compiler_params=_cparams(("arbitrary",)),
        name=name,
    )(idx, x)


def _moe_out_kernel(pos_ref, y_hbm, x_ref, gate_ref, g_ref, b_ref, o_ref, ob_ref, buf, sem, *, tm, alpha):
    base = pl.program_id(0) * tm

    def start(r, carry):
        for k in range(2):
            _row_copy(y_hbm, pos_ref[2 * (base + r) + k], buf.at[k], r, sem).start()
        return carry

    def wait(r, carry):
        for k in range(2):
            _row_copy(y_hbm, 0, buf.at[k], r, sem).wait()
        return carry

    lax.fori_loop(0, tm, start, 0, unroll=8)
    lax.fori_loop(0, tm, wait, 0)
    gt = gate_ref[...]
    f = buf[0] * gt[:, 0:1] + buf[1] * gt[:, 1:2]
    out = _layer_norm(alpha * x_ref[...] + f, g_ref[...], b_ref[...])
    o_ref[...] = out
    ob_ref[...] = out.astype(ob_ref.dtype)


def _moe_out(ys, pos, x, gates, g, b, alpha, name):
    n, d = x.shape
    tm = _pick(n, (256, 128, 88, 64, 8))
    rspec = pl.BlockSpec((tm, d), lambda i, p: (i, 0))
    vspec = pl.BlockSpec((1, d), lambda i, p: (0, 0))
    return pl.pallas_call(
        functools.partial(_moe_out_kernel, tm=tm, alpha=alpha),
        out_shape=(jax.ShapeDtypeStruct((n, d), F32), jax.ShapeDtypeStruct((n, d), BF16)),
        grid_spec=pltpu.PrefetchScalarGridSpec(
            num_scalar_prefetch=1,
            grid=(n // tm,),
            in_specs=[pl.BlockSpec(memory_space=pl.ANY), rspec,
                      pl.BlockSpec((tm, LANE), lambda i, p: (i, 0)), vspec, vspec],
            out_specs=(rspec, rspec),
            scratch_shapes=[pltpu.VMEM((2, tm, d), F32), pltpu.SemaphoreType.DMA(())]),
        compiler_params=_cparams(("arbitrary",)),
        name=name,
    )(pos, ys, x, gates, g, b)


def _moe_ffn(x, w_router, b_router, w1, w3, w2, g, b, alpha, tag):
    n, d = x.shape
    n_exp = w1.shape[0]
    tm = MOE_TILE
    e_pad, gates = _router(x, w_router, b_router, f"router_{tag}")
    e_flat = e_pad[:, :2].reshape(-1)
    n_slots = 2 * n
    onehot = (e_flat[:, None] == jnp.arange(n_exp, dtype=jnp.int32)[None, :]).astype(jnp.int32)
    csum = jnp.cumsum(onehot, axis=0)
    rank = jnp.take_along_axis(csum, e_flat[:, None], axis=1)[:, 0] - 1
    counts = csum[-1]
    padded = (counts + tm - 1) // tm * tm
    pad_end = jnp.cumsum(padded)
    pos = (pad_end - padded)[e_flat] + rank
    n_tiles = -(-(n_slots + n_exp * (tm - 1)) // tm)
    slot_tok = jnp.zeros((n_tiles * tm,), jnp.int32).at[pos].set(jnp.arange(n_slots, dtype=jnp.int32) // 2)
    tile_expert = jnp.minimum(
        jnp.searchsorted(pad_end, jnp.arange(n_tiles, dtype=jnp.int32) * tm, side="right"), n_exp - 1
    ).astype(jnp.int32)
    tile_expert = jnp.concatenate([tile_expert, (pad_end[-1:] // tm).astype(jnp.int32)])
    xs = _gather_rows(x, slot_tok, tm, f"moe_gather_{tag}")
    hmid = _ffn_a(xs, w1, w3, tile_expert, tm, f"moe_a_{tag}")
    ys = _ffn_b(hmid, w2, tile_expert, tm, f"moe_b_{tag}")
    return _moe_out(ys, pos.astype(jnp.int32), x, gates, g, b, alpha, f"moe_out_{tag}")


def kernel(x_prompt, x_sample, state_hgrn, state_ssm, state_conv, cache_k0, cache_v0, cache_k1, cache_v1,
           cache_k2, cache_v2, w_in, hg_lb, hg_norm_w, conv_w, conv_b, dt_bias, a_log, d_skip, ssm_norm_w,
           w_proj_a, w_proj_b, w_proj_c, w_out, ln1_g, ln1_b, ln2_g, ln2_b, ffn_w1, ffn_w3, ffn_w2,
           moe_router, moe_router_b, moe_w1, moe_w3, moe_w2):
    caches = ((cache_k0, cache_v0), (cache_k1, cache_v1), (cache_k2, cache_v2))
    depth = w_in.shape[0]
    b, t, d = x_prompt.shape
    bs, ts, _ = x_sample.shape
    _, _, hg_heads, hg_dk, hg_dv = state_hgrn.shape
    _, _, m_heads, pdim, dstate = state_ssm.shape
    conv_dim = state_conv.shape[-1]
    hpg, adh = cache_k0.shape[-2:]
    assert hg_dk == LANE and hg_dv == LANE and dstate == LANE and adh == LANE and ts <= SROWS
    hg_dim = hg_heads * LANE
    dinner = m_heads * pdim
    groups = (conv_dim - dinner) // (2 * dstate)
    m_hpg = m_heads // groups
    gw = m_hpg * pdim
    adim = len(A_GROUPS) * hpg * LANE
    assert gw % LANE == 0 and LANE % pdim == 0 and t % HG_CHUNK == 0 and t % SSD_CHUNK == 0
    ow = hpg * LANE
    alpha = (2 * depth) ** 0.25
    bt = b * t
    n = bt + bs * SROWS

    sizes = (("hq", hg_dim), ("hf", hg_dim), ("hi", hg_dim), ("hg", hg_dim), ("mz", dinner), ("xbc", conv_dim),
             ("mdt", m_heads), ("aq", adim), ("ak", adim), ("av", adim), ("gate", 3 * d))
    src, acc = {}, 0
    for name, sz in sizes:
        src[name] = acc
        acc += sz
    assert acc == w_in.shape[2]
    n_a = src["mdt"]
    n_b = 3 * adim + 3 * d
    offs = {k: src[k] for k in ("hq", "hf", "hi", "hg", "mz", "xbc")}
    offs.update({k: src[k] - src["aq"] for k in ("aq", "ak", "av", "gate")})
    assert n_a % LANE == 0 and m_heads < LANE and n_b % LANE == 0 and n_a + LANE <= w_in.shape[2]
    assert offs["mz"] % gw == 0 and all(offs[k] % ow == 0 for k in ("aq", "ak", "av"))

    xs_pad = jnp.zeros((bs, SROWS, d), F32).at[:, :ts].set(x_sample)
    x = jnp.concatenate([x_prompt.reshape(bt, d), xs_pad.reshape(bs * SROWS, d)], axis=0)
    xb = x.astype(BF16)

    state_ssm_r = state_ssm.reshape(depth, bs, m_heads * pdim, dstate)
    w_in_t = jnp.swapaxes(w_in, 1, 2)
    lb_sm = jax.nn.softmax(hg_lb.astype(F32), axis=0)
    lb_all = jnp.cumsum(lb_sm, axis=0) - lb_sm[0]

    out_hg_p, out_ssm_p, out_conv_p, out_hg_s, out_ssm_s, out_conv_s = [], [], [], [], [], []
    out_kv_p = [[] for _ in range(2 * len(A_GROUPS))]
    out_kv_s = [[] for _ in range(2 * len(A_GROUPS))]

    for l in range(depth):
        h = _matmul_wcast(xb, w_in_t, l, 0, n_a, 0, F32, f"in_proj_a_{l}")
        h_b = _matmul_wcast(xb, w_in_t, l, n_a, n_b, m_heads, F32, f"in_proj_b_{l}")
        mdt = _matmul_wcast(xb, w_in_t, l, n_a, LANE, 0, F32, f"in_proj_dt_{l}")[:, :m_heads]

        lb = lb_all[l]
        lbc = jnp.stack([jnp.log(jnp.maximum(lb, LB_FLOOR)), jnp.log1p(-lb), 1.0 - lb])
        nw_a = hg_norm_w[l][None, :]
        oa_p, hg_p = _hgrn_prompt(h, offs, lbc, nw_a, b, t, hg_heads, f"hgrn_prompt_{l}")
        oa_s, hg_s = _hgrn_sample(h, offs, lbc, nw_a, state_hgrn, l, bt, bs, ts, hg_heads, f"hgrn_sample_{l}")
        o_a = jnp.concatenate([oa_p, oa_s], axis=0)

        cw, cbias = conv_w[l], conv_b[l][None, :]
        prev_s = jnp.zeros((bs, SROWS, conv_dim), F32).at[:, SROWS - (CONV_WIDTH - 1):].set(state_conv[l])
        xa_s = _conv_sample(h, offs["xbc"], conv_dim, cw, cbias, bt, prev_s.reshape(bs * SROWS, conv_dim),
                            f"conv_sample_{l}")
        a_neg = -jnp.exp(a_log[l].astype(F32))
        prm = jnp.stack([dt_bias[l].astype(F32), a_neg, d_skip[l].astype(F32)])
        prm_r = prm.reshape(3, groups, m_hpg).transpose(1, 0, 2)
        mdt_g = mdt[:bt].reshape(bt, groups, m_hpg).transpose(1, 0, 2)
        nw_b = ssm_norm_w[l][None, :]
        yb_p, ssm_p = _ssd_prompt(h, offs["xbc"], offs["mz"], cw, cbias, mdt_g, mdt_g.transpose(0, 2, 1), prm_r,
                                  prm_r.transpose(0, 2, 1), nw_b, b, t, groups, m_hpg, pdim, dstate,
                                  f"ssd_prompt_{l}")
        yb_s, ssm_s = _ssd_sample(xa_s, h, offs["mz"], jnp.repeat(mdt[bt:], pdim, axis=1),
                                  jnp.repeat(prm, pdim, axis=1), nw_b,
                                  state_ssm_r, l, bt, bs, ts, groups, gw, dstate, f"ssd_sample_{l}")
        y_b = jnp.concatenate([yb_p, yb_s], axis=0)

        q_s, k_s = _rope_sample(h_b, offs, adim, bt, bs * SROWS, ts, f"rope_sample_{l}")
        v_s = h_b[bt:, offs["av"]:offs["av"] + adim]
        outs_p, lses_p, outs_s, lses_s, k_std = [], [], [], [], []
        for gi, (window, dil) in enumerate(A_GROUPS):
            qd, kd, vd, ks = _rope_prompt(h_b, offs, gi, dil, b, t, hpg, f"rope_prompt_{l}_{gi}")
            k_std.append(ks)
            o_g, l_g = _attn_prompt(qd, kd, vd, window, dil, f"attn_prompt_{l}_{gi}")
            outs_p.append(o_g)
            lses_p.append(l_g)
            g3 = lambda a: a[:, gi * ow:(gi + 1) * ow].reshape(bs * SROWS, hpg, LANE)
            o_g, l_g = _attn_sample(g3(q_s), g3(k_s), g3(v_s), caches[gi][0], caches[gi][1], l, window, dil,
                                    bs, ts, f"attn_sample_{l}_{gi}")
            outs_s.append(o_g.reshape(bs * SROWS, ow))
            lses_s.append(l_g.reshape(bs * SROWS, ow))
        o_c = jnp.concatenate([_combine_prompt(outs_p, lses_p, [dl for _, dl in A_GROUPS], f"attn_mix_prompt_{l}"),
                               _combine(outs_s, lses_s, f"attn_mix_sample_{l}")], axis=0)

        merged = _merge(o_a, y_b, o_c, w_proj_a[l].astype(BF16), w_proj_b[l].astype(BF16),
                        w_proj_c[l].astype(BF16), h_b, offs["gate"], d, f"merge_{l}")
        x, xb = _outln(merged, w_out[l].astype(BF16), x, ln1_g[l][None, :], ln1_b[l][None, :], alpha, f"out_ln1_{l}")

        if l % 2 == 0:
            i = l // 2
            tm = _pick(n, (768, 512, 384, 256, 128, 88, 64, 8))
            te = jnp.zeros((n // tm + 1,), jnp.int32).at[-1].set(n // tm)
            hmid = _ffn_a(xb, ffn_w1[i:i + 1].astype(BF16), ffn_w3[i:i + 1].astype(BF16), te, tm, f"ffn_a_{l}")
            f = _ffn_b(hmid, ffn_w2[i:i + 1].astype(BF16), te, tm, f"ffn_b_{l}")
            x, xb = _resln(x, f, ln2_g[l][None, :], ln2_b[l][None, :], alpha, f"ln2_{l}")
        else:
            i = l // 2
            x, xb = _moe_ffn(x, moe_router[i], moe_router_b[i], moe_w1[i].astype(BF16), moe_w3[i].astype(BF16),
                             moe_w2[i].astype(BF16), ln2_g[l][None, :], ln2_b[l][None, :], alpha, str(l))

        keep_c = CONV_WIDTH - 1
        assert t >= keep_c

        def tail_rows(arr, c0, c1, keep):
            return jnp.stack([arr[(bi + 1) * t - keep:(bi + 1) * t, c0:c1] for bi in range(b)])

        out_conv_p.append(tail_rows(h, offs["xbc"], offs["xbc"] + conv_dim, keep_c))
        xbc_s = h[bt:, offs["xbc"]:offs["xbc"] + conv_dim].reshape(bs, SROWS, conv_dim)[:, :ts]
        out_conv_s.append(jnp.concatenate([state_conv[l], xbc_s], axis=1)[:, -keep_c:])
        out_hg_p.append(hg_p)
        out_hg_s.append(hg_s)
        out_ssm_p.append(ssm_p.reshape(b, m_heads, pdim, dstate))
        out_ssm_s.append(ssm_s.reshape(bs, m_heads, pdim, dstate))
        ksm = k_s.reshape(bs, SROWS, adim // LANE, LANE)[:, :ts]
        vsm = v_s.reshape(bs, SROWS, adim // LANE, LANE)[:, :ts]
        for gi, (window, _) in enumerate(A_GROUPS):
            hs = slice(gi * hpg, (gi + 1) * hpg)
            keep = min(window, t)
            v0 = offs["av"] + gi * ow
            out_kv_p[2 * gi].append(k_std[gi].reshape(b, t, hpg, LANE)[:, t - keep:])
            out_kv_p[2 * gi + 1].append(tail_rows(h_b, v0, v0 + ow, keep).reshape(b, keep, hpg, LANE))
            out_kv_s[2 * gi].append(ksm[:, :, hs])
            out_kv_s[2 * gi + 1].append(vsm[:, :, hs])

    y_prompt = x[:bt].reshape(b, t, d)
    y_sample = x[bt:].reshape(bs, SROWS, d)[:, :ts]
    return (y_prompt, y_sample, jnp.stack(out_hg_p), jnp.stack(out_ssm_p), jnp.stack(out_conv_p),
            *[jnp.stack(o) for o in out_kv_p],
            jnp.stack(out_hg_s), jnp.stack(out_ssm_s), jnp.stack(out_conv_s),
            *[jnp.stack(o) for o in out_kv_s])
```

```python
import functools
import math

import numpy as np
import jax
import jax.numpy as jnp
from jax import lax
from jax.experimental import pallas as pl
from jax.experimental.pallas import tpu as pltpu

F32 = jnp.float32
BF16 = jnp.bfloat16

LANE = 128
SROWS = 8
VMEM_CAP = 56 * 1024 * 1024

A_GROUPS = ((128, 1), (512, 4), (2048, 16))
PAST_LEN = 8192
ROPE_THETA = 10000.0
MASK_VALUE = -1e30
LB_FLOOR = 1e-30
EPS = 1e-5
LOG2E = 1.4426950408889634
CONV_WIDTH = 4
HG_CHUNK = 128
SSD_CHUNK = 128
MOE_TILE = 256


def _cparams(sem, vmem_bytes=None):
    kw = dict(dimension_semantics=sem)
    if vmem_bytes is not None:
        kw["vmem_limit_bytes"] = int(min(max(vmem_bytes, 16 * 1024 * 1024), VMEM_CAP))
    return pltpu.CompilerParams(**kw)


def _pick(n, cands, *offsets):
    for c in cands:
        if n % c == 0 and all(o % c == 0 for o in offsets):
            return c
    raise ValueError(f"no tile for {n} in {cands} (offsets {offsets})")


def _sigmoid(x):
    return 1.0 / (1.0 + jnp.exp(-x))


def _silu(x):
    return x * _sigmoid(x)


def _softplus(x):
    return jnp.maximum(x, 0.0) + jnp.log1p(jnp.exp(-jnp.abs(x)))


def _dot(a, b):
    return jnp.dot(a, b, preferred_element_type=F32)


def _dot_nt(a, b):
    return lax.dot_general(a, b, (((1,), (1,)), ((), ())), preferred_element_type=F32)


def _dot_tn(a, b):
    return lax.dot_general(a, b, (((0,), (0,)), ((), ())), preferred_element_type=F32)


def _split2(x):
    hi = x.astype(BF16)
    lo = (x - hi.astype(F32)).astype(BF16)
    return hi, lo


def _split3(x):
    p1 = x.astype(BF16)
    r1 = x - p1.astype(F32)
    p2 = r1.astype(BF16)
    p3 = (r1 - p2.astype(F32)).astype(BF16)
    return p1, p2, p3


def _mm_wcast_kernel(x_ref, w_ref, *rest, shift):
    o_ref, wb_scr = rest[-2:]

    @pl.when(pl.program_id(1) == 0)
    def _():
        w = w_ref[...]
        if shift:
            w = jnp.concatenate([w[shift:], rest[0][:shift]], axis=0)
        wb_scr[...] = w.astype(BF16)

    o_ref[...] = _dot_nt(x_ref[...], wb_scr[...]).astype(o_ref.dtype)


def _matmul_wcast(x, wt_all, layer, col0, ncols, shift, out_dtype, name):
    m, k = x.shape
    tm = _pick(m, ((1408,) if shift == 0 and ncols > LANE else ()) + (768, 512, 384, 256, 128, 88, 64, 8))
    tn = _pick(ncols, (1024, 512, 256, 128), col0)
    assert shift % SROWS == 0 and shift < LANE
    vm = 2 * (tm * k * 2 + k * (tn + LANE) * 4 + tm * tn * 4) + 2 * k * tn * 4 + tm * tn * 4 + (4 << 20)
    in_specs = [pl.BlockSpec((tm, k), lambda j, i: (i, 0)),
                pl.BlockSpec((None, tn, k), lambda j, i: (layer, col0 // tn + j, 0))]
    operands = [x, wt_all]
    if shift:
        in_specs.append(pl.BlockSpec((None, LANE, k), lambda j, i: (layer, (col0 + (j + 1) * tn) // LANE, 0)))
        operands.append(wt_all)
    return pl.pallas_call(
        functools.partial(_mm_wcast_kernel, shift=shift),
        out_shape=jax.ShapeDtypeStruct((m, ncols), out_dtype),
        grid=(ncols // tn, m // tm),
        in_specs=in_specs,
        out_specs=pl.BlockSpec((tm, tn), lambda j, i: (i, j)),
        scratch_shapes=[pltpu.VMEM((tn, k), BF16)],
        compiler_params=_cparams(("parallel", "arbitrary"), vm),
        name=name,
    )(*operands)


def _conv_tile(u_ref, carry, w_ref, b_ref, rows):
    u = u_ref[...]
    full = jnp.concatenate([carry[...], u], axis=0)
    acc = b_ref[...] + u * w_ref[CONV_WIDTH - 1:CONV_WIDTH, :]
    for i in range(CONV_WIDTH - 1):
        off = SROWS - (CONV_WIDTH - 1) + i
        acc = acc + full[off:off + rows] * w_ref[i:i + 1, :]
    return u, _silu(acc)


def _conv_sample_kernel(u_ref, prev_ref, w_ref, b_ref, o_ref):
    o_ref[...] = _conv_tile(u_ref, prev_ref, w_ref, b_ref, SROWS)[1]


def _conv_sample(h, off_xbc, conv_dim, w, b, row0, prev_arr, name):
    nrows = prev_arr.shape[0]
    tc = _pick(conv_dim, (2048, 1536, 1024, 512, 256, 128), off_xbc)
    cb0 = off_xbc // tc
    rb0 = row0 // SROWS
    return pl.pallas_call(
        _conv_sample_kernel,
        out_shape=jax.ShapeDtypeStruct((nrows, conv_dim), F32),
        grid=(nrows // SROWS, conv_dim // tc),
        in_specs=[pl.BlockSpec((SROWS, tc), lambda i, j: (rb0 + i, cb0 + j)),
                  pl.BlockSpec((SROWS, tc), lambda i, j: (i, j)),
                  pl.BlockSpec((CONV_WIDTH, tc), lambda i, j: (0, j)),
                  pl.BlockSpec((1, tc), lambda i, j: (0, j))],
        out_specs=pl.BlockSpec((SROWS, tc), lambda i, j: (i, j)),
        compiler_params=_cparams(("parallel", "parallel")),
        name=name,
    )(h, prev_arr, w, b)


def _hgrn_consts(c):
    t = np.arange(c)[:, None]
    u = np.arange(c)[None, :]
    mats, masks = [], [np.eye(c)]
    h = 1
    while h < c:
        tb, ub = t // h, u // h
        mats.append(((tb % 2 == 1) & (ub == tb) & (u <= t)) | ((tb % 2 == 0) & (ub == tb) & (u > t)))
        masks.append((tb % 2 == 1) & (ub == tb - 1))
        h *= 2
    mats.append(u <= t)
    mall = np.concatenate([m.astype(np.float32) for m in mats], axis=0)
    mall = np.concatenate([mall, mall], axis=1)
    return jnp.asarray(mall, BF16), jnp.asarray(np.stack(masks).astype(np.float32))


def _hgrn_gates(fx, hq, lbc):
    la, l1, oml = lbc[0:1], lbc[1:2], lbc[2:3]
    ls = jnp.minimum(fx, 0.0) - jnp.log1p(jnp.exp(-jnp.abs(fx)))
    ct = l1 + ls
    g = jnp.maximum(la, ct) + jnp.log1p(jnp.exp(-jnp.abs(la - ct)))
    kk = oml * (1.0 / (1.0 + jnp.exp(fx)))
    return g, kk, _silu(hq)


def _hgrn_out(o, nw, hg):
    ms = jnp.mean(o * o, axis=-1, keepdims=True)
    return (o * lax.rsqrt(ms + EPS) * nw) * _silu(hg)


def _hgrn_prompt_kernel(hq_ref, hf_ref, hi_ref, hg_ref, lbc_ref, nw_ref, mall_ref, mask_ref,
                        o_ref, s_ref, st_scr, *, hb, c, nlev):
    ci = pl.program_id(2)

    @pl.when(ci == 0)
    def _():
        st_scr[...] = jnp.zeros_like(st_scr)

    for h0 in range(0, hb, 2):
        pair = [h2 for h2 in (h0, h0 + 1) if h2 < hb]
        sls = [slice(h2 * LANE, (h2 + 1) * LANE) for h2 in pair]
        gts = [_hgrn_gates(hf_ref[:, s2], hq_ref[:, s2], lbc_ref[:, s2]) for s2 in sls]
        qs, kks = [g[2] for g in gts], [g[1] for g in gts]
        halves = [_split2(g[0] * LOG2E) for g in gts]
        gmat = jnp.concatenate([jnp.concatenate([hl[0] for hl in halves], axis=1),
                                jnp.concatenate([hl[1] for hl in halves], axis=1)], axis=0)
        ex_pair = _dot(mall_ref[...], gmat)
        exs = [ex_pair[:, i * LANE:(i + 1) * LANE] for i in range(len(pair))]

        def pair_scores(qls, kls):
            ql = jnp.concatenate([x.astype(BF16) for x in qls], axis=0)
            kl = jnp.concatenate([x.astype(BF16) for x in kls], axis=0)
            sc = _dot_nt(ql, kl)
            return [sc[i * c:(i + 1) * c, i * c:(i + 1) * c] for i in range(len(qls))]

        accs = [s * mask_ref[0] for s in pair_scores(qs, kks)]
        for lv in range(nlev):
            ws = [jnp.exp2(ex[lv * c:(lv + 1) * c]) for ex in exs]
            scs = pair_scores([q * w for q, w in zip(qs, ws)], [kk * w for kk, w in zip(kks, ws)])
            accs = [a + s * mask_ref[lv + 1] for a, s in zip(accs, scs)]
        for hh, sl, q, kk, ex, a in zip(pair, sls, qs, kks, exs, accs):
            vb = hi_ref[:, sl].astype(BF16)
            bcum = ex[nlev * c:(nlev + 1) * c]
            erev = bcum[c - 1:c, :] - bcum
            st = st_scr[hh]
            o = _dot(a.astype(BF16), vb) + _dot_nt((q * jnp.exp2(bcum)).astype(BF16), st.astype(BF16))
            st_new = st * jnp.exp2(bcum[c - 1:c, :]) + _dot_tn(vb, (kk * jnp.exp2(erev)).astype(BF16))
            st_scr[hh] = st_new
            o_ref[:, sl] = _hgrn_out(o, nw_ref[:, sl], hg_ref[:, sl]).astype(o_ref.dtype)

    @pl.when(ci == pl.num_programs(2) - 1)
    def _():
        for hh in range(hb):
            s_ref[0, hh] = st_scr[hh].T


def _hgrn_prompt(h, offs, lbc, nw, b, t, heads, name):
    c = HG_CHUNK
    hb = _pick(heads * LANE, (8 * LANE, 4 * LANE, 2 * LANE, LANE),
               *(offs[k] for k in ("hq", "hf", "hi", "hg"))) // LANE
    w = hb * LANE
    nlev = int(math.log2(c))
    mall, masks = _hgrn_consts(c)
    nc = t // c

    def hspec(off):
        return pl.BlockSpec((c, w), lambda bi, hi, ci: (bi * nc + ci, off // w + hi))

    return pl.pallas_call(
        functools.partial(_hgrn_prompt_kernel, hb=hb, c=c, nlev=nlev),
        out_shape=(jax.ShapeDtypeStruct((b * t, heads * LANE), BF16),
                   jax.ShapeDtypeStruct((b, heads, LANE, LANE), F32)),
        grid=(b, heads // hb, nc),
        in_specs=[hspec(offs["hq"]), hspec(offs["hf"]), hspec(offs["hi"]), hspec(offs["hg"]),
                  pl.BlockSpec((3, w), lambda bi, hi, ci: (0, hi)),
                  pl.BlockSpec((1, w), lambda bi, hi, ci: (0, hi)),
                  pl.BlockSpec(mall.shape, lambda bi, hi, ci: (0, 0)),
                  pl.BlockSpec(masks.shape, lambda bi, hi, ci: (0, 0, 0))],
        out_specs=(pl.BlockSpec((c, w), lambda bi, hi, ci: (bi * nc + ci, hi)),
                   pl.BlockSpec((1, hb, LANE, LANE), lambda bi, hi, ci: (bi, hi, 0, 0))),
        scratch_shapes=[pltpu.VMEM((hb, LANE, LANE), F32)],
        compiler_params=_cparams(("parallel", "parallel", "arbitrary"), 40 << 20),
        name=name,
    )(h, h, h, h, lbc, nw, mall, masks)


def _row_select(rows_list, n):
    width = rows_list[0].shape[1]
    ridx = lax.broadcasted_iota(jnp.int32, (n, width), 0)
    out = jnp.zeros((n, width), F32)
    for i, r in enumerate(rows_list):
        out = jnp.where(ridx == i, r, out)
    return out


def _hgrn_sample_kernel(hq_ref, hf_ref, hi_ref, hg_ref, lbc_ref, nw_ref, s0_ref, o_ref, s_ref, *, hb, ts):
    ridx = lax.broadcasted_iota(jnp.int32, (SROWS, LANE), 0)
    real = ridx < ts
    for hh in range(hb):
        sl = slice(hh * LANE, (hh + 1) * LANE)
        g, kk, q = _hgrn_gates(hf_ref[:, sl], hq_ref[:, sl], lbc_ref[:, sl])
        v = hi_ref[:, sl]
        g = jnp.where(real, g, 0.0)
        kk = jnp.where(real, kk, 0.0)
        brow = []
        for i in range(ts):
            brow.append(g[i:i + 1] if i == 0 else brow[-1] + g[i:i + 1])
        s0 = s0_ref[0, hh]
        bt = _row_select(brow, SROWS)
        o_inter = _dot(q * jnp.exp(bt), s0)
        orow = []
        for i in range(ts):
            acc = jnp.zeros((1, LANE), F32)
            for s in range(i + 1):
                wgt = jnp.sum(q[i:i + 1] * kk[s:s + 1] * jnp.exp(brow[i] - brow[s]), axis=-1, keepdims=True)
                acc = acc + wgt * v[s:s + 1]
            orow.append(acc)
        o = o_inter + _row_select(orow, SROWS)
        o_ref[:, sl] = _hgrn_out(o, nw_ref[:, sl], hg_ref[:, sl]).astype(o_ref.dtype)
        cols = [jnp.exp(brow[-1])] + [kk[s:s + 1] * jnp.exp(brow[-1] - brow[s]) for s in range(ts)]
        xt = jnp.concatenate([_row_select(cols, SROWS), jnp.zeros((LANE - SROWS, LANE), F32)], axis=0).T
        s_new = xt[:, 0:1] * s0
        for s in range(ts):
            s_new = s_new + xt[:, 1 + s:2 + s] * v[s:s + 1]
        s_ref[0, hh] = s_new


def _hgrn_sample(h, offs, lbc, nw, s0_all, layer, row0, bs, ts, heads, name):
    hb = _pick(heads * LANE, (8 * LANE, 4 * LANE, 2 * LANE, LANE),
               *(offs[k] for k in ("hq", "hf", "hi", "hg"))) // LANE
    w = hb * LANE
    rb0 = row0 // SROWS

    def hspec(off):
        return pl.BlockSpec((SROWS, w), lambda bi, hi: (rb0 + bi, off // w + hi))

    sspec = pl.BlockSpec((1, hb, LANE, LANE), lambda bi, hi: (bi, hi, 0, 0))
    s0spec = pl.BlockSpec((None, 1, hb, LANE, LANE), lambda bi, hi: (layer, bi, hi, 0, 0))
    return pl.pallas_call(
        functools.partial(_hgrn_sample_kernel, hb=hb, ts=ts),
        out_shape=(jax.ShapeDtypeStruct((bs * SROWS, heads * LANE), BF16),
                   jax.ShapeDtypeStruct((bs, heads, LANE, LANE), F32)),
        grid=(bs, heads // hb),
        in_specs=[hspec(offs["hq"]), hspec(offs["hf"]), hspec(offs["hi"]), hspec(offs["hg"]),
                  pl.BlockSpec((3, w), lambda bi, hi: (0, hi)),
                  pl.BlockSpec((1, w), lambda bi, hi: (0, hi)),
                  s0spec],
        out_specs=(pl.BlockSpec((SROWS, w), lambda bi, hi: (bi, hi)), sspec),
        compiler_params=_cparams(("parallel", "parallel")),
        name=name,
    )(h, h, h, h, lbc, nw, s0_all)


def _ssd_finish(ys, xs, zs, dsk, nw_ref, o_ref, width, col0=0):
    gated = [(y + d * x) * _silu(z) for y, x, z, d in zip(ys, xs, zs, dsk)]
    ssq = sum(jnp.sum(t * t, axis=-1, keepdims=True) for t in gated)
    r = lax.rsqrt(ssq / width + EPS)
    for p, t in enumerate(gated):
        sl = slice(col0 + p * LANE, col0 + (p + 1) * LANE)
        o_ref[:, sl] = (t * r * nw_ref[:, sl]).astype(o_ref.dtype)


def _conv_chunk(u_ref, carry, w_ref, b_ref, rows):
    u, act = _conv_tile(u_ref, carry, w_ref, b_ref, rows)
    carry[...] = u[rows - SROWS:rows]
    return act


def _ssd_prompt_kernel(xr_ref, br_ref, cr_ref, z_ref, dt_ref, dtt_ref, pr_ref, pc_ref, nw_ref, tri_ref, triu_ref,
                       wx_ref, wb_ref, wc_ref, bx_ref, bb_ref, bc_ref,
                       o_ref, s_ref, ht_scr, x_scr, cx_scr, cb_scr, cc_scr, *, c, hpg, pdim):
    ci = pl.program_id(2)
    npair = hpg * pdim // LANE
    hpp = LANE // pdim

    @pl.when(ci == 0)
    def _():
        ht_scr[...] = jnp.zeros_like(ht_scr)
        cx_scr[...] = jnp.zeros_like(cx_scr)
        cb_scr[...] = jnp.zeros_like(cb_scr)
        cc_scr[...] = jnp.zeros_like(cc_scr)

    x_scr[...] = _conv_chunk(xr_ref, cx_scr, wx_ref, bx_ref, c)
    x_ref = x_scr
    b_act = _conv_chunk(br_ref, cb_scr, wb_ref, bb_ref, c)
    c_act = _conv_chunk(cr_ref, cc_scr, wc_ref, bc_ref, c)

    pr = pr_ref[0]
    pc = pc_ref[0]
    dt_c = _softplus(dt_ref[0] + pr[0:1])
    da_c = dt_c * pr[1:2]
    da_r = _softplus(dtt_ref[0] + pc[:, 0:1]) * pc[:, 1:2]
    tri = tri_ref[...]
    triu = triu_ref[...]
    cum_c = sum(_dot(tri, p.astype(F32)) for p in _split3(da_c))
    cum_r = sum(_dot(p.astype(F32), triu) for p in _split3(da_r))
    bm = b_act.astype(BF16)
    cm = c_act.astype(BF16)
    cb = _dot_nt(cm, bm)
    trow = lax.broadcasted_iota(jnp.int32, (c, c), 0)
    scol = lax.broadcasted_iota(jnp.int32, (c, c), 1)
    causal = trow >= scol
    lane = lax.broadcasted_iota(jnp.int32, (1, LANE), 1)

    def per_head(vals):
        out = vals[-1]
        for k in range(hpp - 2, -1, -1):
            out = jnp.where(lane < (k + 1) * pdim, vals[k], out)
        return out

    ch_all = _dot(cm, jnp.concatenate([ht_scr[p].astype(BF16) for p in range(npair)], axis=1))
    ys, xs, zs, dsk, upd, hdec = [], [], [], [], [], []
    for p in range(npair):
        sl = slice(p * LANE, (p + 1) * LANE)
        js = [p * hpp + k for k in range(hpp)]
        xp = x_ref[:, sl]
        xdt = xp * per_head([dt_c[:, j:j + 1] for j in js])
        xdtb = xdt.astype(BF16)
        mj = []
        for j in js:
            seg = jnp.exp(jnp.where(causal, cum_c[:, j:j + 1] - cum_r[j:j + 1, :], MASK_VALUE))
            mj.append((cb * seg).astype(BF16))
        yy = _dot(jnp.concatenate(mj, axis=0), xdtb)
        y = per_head([yy[k * c:(k + 1) * c] for k in range(hpp)])
        y = y + ch_all[:, sl] * per_head([jnp.exp(cum_c[:, j:j + 1]) for j in js])
        last = [cum_c[c - 1:c, j:j + 1] for j in js]
        wgt = per_head([jnp.exp(l - cum_c[:, j:j + 1]) for l, j in zip(last, js)])
        upd.append((xdt * wgt).astype(BF16))
        hdec.append(per_head([jnp.exp(l) for l in last]))
        ys.append(y)
        xs.append(xp)
        zs.append(z_ref[:, sl])
        dsk.append(per_head([pr[2:3, j:j + 1] for j in js]))
    ht_add = _dot_tn(bm, jnp.concatenate(upd, axis=1))
    for p in range(npair):
        ht_scr[p] = ht_scr[p] * hdec[p] + ht_add[:, p * LANE:(p + 1) * LANE]

    _ssd_finish(ys, xs, zs, dsk, nw_ref, o_ref, hpg * pdim)

    @pl.when(ci == pl.num_programs(2) - 1)
    def _():
        for p in range(npair):
            s_ref[0, p * LANE:(p + 1) * LANE, :] = ht_scr[p].T


def _ssd_prompt(h, off_xbc, off_z, conv_w, conv_b, mdt_g, mdt_gt, prm_r, prm_c, nw, b, t, groups, hpg, pdim, dstate,
                name):
    c = SSD_CHUNK
    gw = hpg * pdim
    dinner = groups * gw
    nc = t // c
    npair = gw // LANE
    assert off_xbc % gw == 0 and (off_xbc + dinner) % dstate == 0 and c >= SROWS
    xb0 = off_xbc // gw
    nb0 = dinner // dstate
    hb0 = (off_xbc + dinner) // dstate
    tt = np.arange(c)
    tri = jnp.asarray((tt[None, :] <= tt[:, None]).astype(np.float32))
    rows = lambda bi, ci: bi * nc + ci
    const = lambda bi, gi, ci: (0, 0)
    return pl.pallas_call(
        functools.partial(_ssd_prompt_kernel, c=c, hpg=hpg, pdim=pdim),
        out_shape=(jax.ShapeDtypeStruct((b * t, dinner), BF16),
                   jax.ShapeDtypeStruct((b, groups * gw, dstate), F32)),
        grid=(b, groups, nc),
        in_specs=[pl.BlockSpec((c, gw), lambda bi, gi, ci: (rows(bi, ci), xb0 + gi)),
                  pl.BlockSpec((c, dstate), lambda bi, gi, ci: (rows(bi, ci), hb0 + gi)),
                  pl.BlockSpec((c, dstate), lambda bi, gi, ci: (rows(bi, ci), hb0 + groups + gi)),
                  pl.BlockSpec((c, gw), lambda bi, gi, ci: (rows(bi, ci), off_z // gw + gi)),
                  pl.BlockSpec((1, c, hpg), lambda bi, gi, ci: (gi, rows(bi, ci), 0)),
                  pl.BlockSpec((1, hpg, c), lambda bi, gi, ci: (gi, 0, rows(bi, ci))),
                  pl.BlockSpec((1, 3, hpg), lambda bi, gi, ci: (gi, 0, 0)),
                  pl.BlockSpec((1, hpg, 3), lambda bi, gi, ci: (gi, 0, 0)),
                  pl.BlockSpec((1, gw), lambda bi, gi, ci: (0, gi)),
                  pl.BlockSpec((c, c), const),
                  pl.BlockSpec((c, c), const),
                  pl.BlockSpec((CONV_WIDTH, gw), lambda bi, gi, ci: (0, gi)),
                  pl.BlockSpec((CONV_WIDTH, dstate), lambda bi, gi, ci: (0, nb0 + gi)),
                  pl.BlockSpec((CONV_WIDTH, dstate), lambda bi, gi, ci: (0, nb0 + groups + gi)),
                  pl.BlockSpec((1, gw), lambda bi, gi, ci: (0, gi)),
                  pl.BlockSpec((1, dstate), lambda bi, gi, ci: (0, nb0 + gi)),
                  pl.BlockSpec((1, dstate), lambda bi, gi, ci: (0, nb0 + groups + gi))],
        out_specs=(pl.BlockSpec((c, gw), lambda bi, gi, ci: (rows(bi, ci), gi)),
                   pl.BlockSpec((1, gw, dstate), lambda bi, gi, ci: (bi, gi, 0))),
        scratch_shapes=[pltpu.VMEM((npair, dstate, LANE), F32), pltpu.VMEM((c, gw), F32),
                        pltpu.VMEM((SROWS, gw), F32), pltpu.VMEM((SROWS, dstate), F32),
                        pltpu.VMEM((SROWS, dstate), F32)],
        compiler_params=_cparams(("parallel", "parallel", "arbitrary"), 40 << 20),
        name=name,
    )(h, h, h, h, mdt_g, mdt_gt, prm_r, prm_c, nw, tri, tri.T, conv_w, conv_w, conv_w, conv_b, conv_b, conv_b)


def _ssd_sample_kernel(x_ref, b_ref, c_ref, z_ref, dtx_ref, px_ref, nw_ref, s0_ref, o_ref, s_ref, *, ts, gw, ngb):
    npair = gw // LANE
    dstate = b_ref.shape[1] // ngb
    ridx = lax.broadcasted_iota(jnp.int32, (SROWS, ngb * gw), 0)
    dt = jnp.where(ridx < ts, _softplus(dtx_ref[...] + px_ref[0:1, :]), 0.0)
    dec = jnp.exp(dt * px_ref[1:2, :])
    x = x_ref[...]
    xdt = x * dt
    pad = jnp.zeros((LANE - SROWS, dstate), F32)
    for gs in range(ngb):
        nsl = slice(gs * dstate, (gs + 1) * dstate)
        bt = jnp.concatenate([b_ref[:, nsl], pad], axis=0).T
        ct = jnp.concatenate([c_ref[:, nsl], pad], axis=0).T
        ys, xs, zs, dsk = [], [], [], []
        for p in range(npair):
            sl = slice(gs * gw + p * LANE, gs * gw + (p + 1) * LANE)
            ht = s0_ref[0, sl, :].T
            yrow = []
            for i in range(ts):
                ht = ht * dec[i:i + 1, sl] + bt[:, i:i + 1] * xdt[i:i + 1, sl]
                yrow.append(jnp.sum(ht * ct[:, i:i + 1], axis=0, keepdims=True))
            s_ref[0, sl, :] = ht.T
            ys.append(_row_select(yrow, SROWS))
            xs.append(x[:, sl])
            zs.append(z_ref[:, sl])
            dsk.append(px_ref[2:3, sl])
        _ssd_finish(ys, xs, zs, dsk, nw_ref, o_ref, gw, col0=gs * gw)


def _ssd_sample(xa_s, h, off_z, mdt_x, prm_x, nw, s0_all, layer, row0, bs, ts, groups, gw, dstate, name):
    dinner = groups * gw
    rb0 = row0 // SROWS
    nb0 = dinner // dstate
    ngb = _pick(groups, (4, 2, 1), nb0, nb0 + groups, off_z // gw)
    w = ngb * gw
    sspec = pl.BlockSpec((1, w, dstate), lambda bi, gi: (bi, gi, 0))
    s0spec = pl.BlockSpec((None, 1, w, dstate), lambda bi, gi: (layer, bi, gi, 0))
    return pl.pallas_call(
        functools.partial(_ssd_sample_kernel, ts=ts, gw=gw, ngb=ngb),
        out_shape=(jax.ShapeDtypeStruct((bs * SROWS, dinner), BF16),
                   jax.ShapeDtypeStruct((bs, groups * gw, dstate), F32)),
        grid=(bs, groups // ngb),
        in_specs=[pl.BlockSpec((SROWS, w), lambda bi, gi: (bi, gi)),
                  pl.BlockSpec((SROWS, ngb * dstate), lambda bi, gi: (bi, nb0 // ngb + gi)),
                  pl.BlockSpec((SROWS, ngb * dstate), lambda bi, gi: (bi, (nb0 + groups) // ngb + gi)),
                  pl.BlockSpec((SROWS, w), lambda bi, gi: (rb0 + bi, off_z // w + gi)),
                  pl.BlockSpec((SROWS, w), lambda bi, gi: (bi, gi)),
                  pl.BlockSpec((3, w), lambda bi, gi: (0, gi)),
                  pl.BlockSpec((1, w), lambda bi, gi: (0, gi)),
                  s0spec],
        out_specs=(pl.BlockSpec((SROWS, w), lambda bi, gi: (bi, gi)), sspec),
        compiler_params=_cparams(("parallel", "parallel")),
        name=name,
    )(xa_s, xa_s, xa_s, h, mdt_x, prm_x, nw, s0_all)


def _rot(x, cos, sin):
    return x * cos + pltpu.roll(x, LANE // 2, 1) * sin


def _residue_perm(dil, rows):
    i = np.arange(rows)
    p = np.zeros((rows, rows), np.float32)
    p[(i % dil) * (rows // dil) + i // dil, i] = 1.0
    return p


def _rope_prompt_kernel(q_ref, k_ref, v_ref, cos_ref, sin_ref, p_ref, qd_ref, kd_ref, vd_ref, ks_ref,
                        *, nh, dil, rows):
    n = rows // dil
    cos = cos_ref[...]
    sin = sin_ref[...]

    def emit(xb, dst_ref, sl):
        y = _dot(p_ref[...], xb).astype(BF16) if dil > 1 else xb
        for r in range(dil):
            dst_ref[0, r, :, sl] = y[r * n:(r + 1) * n]

    for hh in range(nh):
        sl = slice(hh * LANE, (hh + 1) * LANE)
        emit(_rot(q_ref[:, sl], cos, sin).astype(BF16), qd_ref, sl)
        kr = _rot(k_ref[:, sl], cos, sin)
        ks_ref[:, sl] = kr
        emit(kr.astype(BF16), kd_ref, sl)
        emit(v_ref[:, sl].astype(BF16), vd_ref, sl)


def _rope_prompt(h, offs, gi, dil, b, t, hpg, name):
    ow = hpg * LANE
    rows = _pick(t, (256, 128, 64, 32, 16))
    assert rows % (2 * SROWS * dil) == 0 and all(offs[k] % ow == 0 for k in ("aq", "ak", "av"))
    l, nb, n = t // dil, t // rows, rows // dil
    half = LANE // 2
    inv = ROPE_THETA ** (-jnp.arange(half, dtype=F32) / half)
    ang = jnp.arange(t, dtype=F32)[:, None] * inv[None, :]
    cos2 = jnp.concatenate([jnp.cos(ang), jnp.cos(ang)], axis=-1)
    sin2 = jnp.concatenate([-jnp.sin(ang), jnp.sin(ang)], axis=-1)
    perm = jnp.asarray(_residue_perm(dil, rows), BF16)

    def hspec(key):
        cb = offs[key] // ow + gi
        return pl.BlockSpec((rows, ow), lambda bi, j: (bi * nb + j, cb))

    tspec = pl.BlockSpec((rows, LANE), lambda bi, j: (j, 0))
    dspec = pl.BlockSpec((1, dil, n, ow), lambda bi, j: (bi, 0, j, 0))
    dshape = jax.ShapeDtypeStruct((b, dil, l, ow), BF16)
    return pl.pallas_call(
        functools.partial(_rope_prompt_kernel, nh=hpg, dil=dil, rows=rows),
        out_shape=(dshape, dshape, dshape, jax.ShapeDtypeStruct((b * t, ow), F32)),
        grid=(b, nb),
        in_specs=[hspec("aq"), hspec("ak"), hspec("av"), tspec, tspec,
                  pl.BlockSpec((rows, rows), lambda bi, j: (0, 0))],
        out_specs=(dspec, dspec, dspec, pl.BlockSpec((rows, ow), lambda bi, j: (bi * nb + j, 0))),
        compiler_params=_cparams(("parallel", "parallel")),
        name=name,
    )(h, h, h, cos2, sin2, perm)


def _rope_sample_kernel(q_ref, k_ref, cos_ref, sin_ref, qo_ref, ko_ref, *, nh):
    cos = cos_ref[...]
    sin = sin_ref[...]
    for hh in range(nh):
        sl = slice(hh * LANE, (hh + 1) * LANE)
        qo_ref[:, sl] = _rot(q_ref[:, sl], cos, sin)
        ko_ref[:, sl] = _rot(k_ref[:, sl], cos, sin)


def _rope_sample(h, offs, adim, row0, nrows, ts, name):
    tr = _pick(nrows, (256, 128, 64, 32, 16, 8), row0)
    tc = _pick(adim, (1024, 512, 256, 128), offs["aq"], offs["ak"])
    half = LANE // 2
    inv = ROPE_THETA ** (-jnp.arange(half, dtype=F32) / half)
    srow = jnp.arange(SROWS)
    pos = jnp.tile(jnp.where(srow < ts, PAST_LEN + srow, 0), nrows // SROWS).astype(F32)
    ang = pos[:, None] * inv[None, :]
    cos2 = jnp.concatenate([jnp.cos(ang), jnp.cos(ang)], axis=1)
    sin2 = jnp.concatenate([-jnp.sin(ang), jnp.sin(ang)], axis=1)
    rb0 = row0 // tr

    def hspec(off):
        return pl.BlockSpec((tr, tc), lambda i, j: (rb0 + i, off // tc + j))

    ospec = pl.BlockSpec((tr, tc), lambda i, j: (i, j))
    tspec = pl.BlockSpec((tr, LANE), lambda i, j: (i, 0))
    return pl.pallas_call(
        functools.partial(_rope_sample_kernel, nh=tc // LANE),
        out_shape=(jax.ShapeDtypeStruct((nrows, adim), F32), jax.ShapeDtypeStruct((nrows, adim), F32)),
        grid=(nrows // tr, adim // tc),
        in_specs=[hspec(offs["aq"]), hspec(offs["ak"]), tspec, tspec],
        out_specs=(ospec, ospec),
        compiler_params=_cparams(("parallel", "parallel")),
        name=name,
    )(h, h, cos2, sin2)


def _attn_prompt_kernel(q_ref, kp_ref, kc_ref, vp_ref, vc_ref, o_ref, l_ref, s_scr, p_scr, *, qb, nh, scale):
    has_prev = pl.program_id(2) > 0
    row = lax.broadcasted_iota(jnp.int32, (qb, 2 * qb), 0)
    col = lax.broadcasted_iota(jnp.int32, (qb, 2 * qb), 1)
    in_prev = jnp.logical_and(jnp.logical_and(col < qb, col >= row), has_prev)
    valid = jnp.logical_or(in_prev, jnp.logical_and(col >= qb, col - qb <= row))
    for hh in range(nh):
        sl = slice(hh * LANE, (hh + 1) * LANE)
        kcat = jnp.concatenate([kp_ref[:, sl], kc_ref[:, sl]], axis=0)
        s_scr[hh] = _dot_nt(q_ref[:, sl], kcat)
    for hh in range(nh):
        sl = slice(hh * LANE, (hh + 1) * LANE)
        s = jnp.where(valid, s_scr[hh] * scale, MASK_VALUE)
        m = jnp.max(s, axis=-1, keepdims=True)
        p = jnp.exp(s - m)
        den = jnp.sum(p, axis=-1, keepdims=True)
        p_scr[hh] = (p * (1.0 / den)).astype(BF16)
        l_ref[:, sl] = jnp.broadcast_to(m + jnp.log(den), (qb, LANE))
    for hh in range(nh):
        sl = slice(hh * LANE, (hh + 1) * LANE)
        vcat = jnp.concatenate([vp_ref[:, sl], vc_ref[:, sl]], axis=0)
        o_ref[:, sl] = _dot(p_scr[hh], vcat)


def _attn_prompt(qd, kd, vd, window, dil, name):
    b, _, l, ow = qd.shape
    qb = window // dil
    nq = l // qb
    cur = pl.BlockSpec((None, None, qb, ow), lambda bi, r, i: (bi, r, i, 0))
    prev = pl.BlockSpec((None, None, qb, ow), lambda bi, r, i: (bi, r, jnp.maximum(i - 1, 0), 0))
    oshape = jax.ShapeDtypeStruct((b, dil, l, ow), F32)
    return pl.pallas_call(
        functools.partial(_attn_prompt_kernel, qb=qb, nh=ow // LANE, scale=LANE ** -0.5),
        out_shape=(oshape, oshape),
        grid=(b, dil, nq),
        in_specs=[cur, prev, cur, prev, cur],
        out_specs=(cur, cur),
        scratch_shapes=[pltpu.VMEM((ow // LANE, qb, 2 * qb), F32), pltpu.VMEM((ow // LANE, qb, 2 * qb), BF16)],
        compiler_params=_cparams(("parallel", "parallel", "arbitrary")),
        name=name,
    )(qd, kd, kd, vd, vd)


def _attn_sample_kernel(q_ref, kn_ref, vn_ref, kc_ref, vc_ref, o_ref, l_ref, *, ts, window, dil, scale):
    nk = window // dil
    o_ref[...] = jnp.zeros_like(o_ref)
    l_ref[...] = jnp.zeros_like(l_ref)
    arow = lax.broadcasted_iota(jnp.int32, (nk, 1, 1), 0)
    nrow = lax.broadcasted_iota(jnp.int32, (SROWS, 1, 1), 0)
    kn = kn_ref[...]
    vn = vn_ref[...]
    for i in range(ts):
        rho = (window + i) % dil
        base = (window + i - rho) // dil
        j0 = i // dil + 1
        a_lo, a_hi = max(base - nk, 0), min(base - j0, nk - 1)
        vcache = jnp.logical_and(arow >= a_lo, arow <= a_hi)
        new_rows = [i - j * dil for j in range(i // dil + 1)]
        vnew = functools.reduce(jnp.logical_or, [nrow == r for r in new_rows])
        q = q_ref[i][None]
        kc = kc_ref[:, rho]
        sc = jnp.where(vcache, jnp.sum(kc * q, axis=-1, keepdims=True) * scale, MASK_VALUE)
        sn = jnp.where(vnew, jnp.sum(kn * q, axis=-1, keepdims=True) * scale, MASK_VALUE)
        m = jnp.maximum(jnp.max(sc, axis=0, keepdims=True), jnp.max(sn, axis=0, keepdims=True))
        pc = jnp.where(vcache, jnp.exp(sc - m), 0.0)
        pn = jnp.where(vnew, jnp.exp(sn - m), 0.0)
        den = jnp.sum(pc, axis=0, keepdims=True) + jnp.sum(pn, axis=0, keepdims=True)
        acc = jnp.sum(pc * vc_ref[:, rho], axis=0) + jnp.sum(pn * vn, axis=0)
        o_ref[i] = acc / den[0]
        l_ref[i] = jnp.broadcast_to(m[0] + jnp.log(den[0]), acc.shape)


def _attn_sample(q3, k3, v3, ck_all, cv_all, layer, window, dil, bs, ts, name):
    hpg = q3.shape[1]
    nk = window // dil
    assert ck_all.shape[2] == window and window % dil == 0 and (dil == 1 or dil >= ts)
    nres = min(dil, ts)
    shape6 = ck_all.shape[:2] + (nk, dil, hpg, LANE)
    nspec = pl.BlockSpec((SROWS, hpg, LANE), lambda bi: (bi, 0, 0))
    cspec = pl.BlockSpec((None, None, nk, nres, hpg, LANE), lambda bi: (layer, bi, 0, 0, 0, 0))
    oshape = jax.ShapeDtypeStruct((bs * SROWS, hpg, LANE), F32)
    return pl.pallas_call(
        functools.partial(_attn_sample_kernel, ts=ts, window=window, dil=dil, scale=LANE ** -0.5),
        out_shape=(oshape, oshape),
        grid=(bs,),
        in_specs=[nspec, nspec, nspec, cspec, cspec],
        out_specs=(nspec, nspec),
        compiler_params=_cparams(("parallel",), 40 << 20),
        name=name,
    )(q3, k3, v3, ck_all.reshape(shape6), cv_all.reshape(shape6))


def _combine_kernel(*refs):
    ng = (len(refs) - 1) // 2
    os_, ls_, out = refs[:ng], refs[ng:2 * ng], refs[-1]
    out[...] = _mix_groups([o[...] for o in os_], [l[...] for l in ls_]).astype(out.dtype)


def _mix_groups(os_, ls):
    m = functools.reduce(jnp.maximum, ls)
    ws = [jnp.exp(l - m) for l in ls]
    den = functools.reduce(lambda a, b: a + b, ws)
    acc = functools.reduce(lambda a, b: a + b, [w * o for w, o in zip(ws, os_)])
    return acc / den


def _combine_prompt_kernel(*refs, dils):
    ng = len(dils)
    o_refs, l_refs, p_refs, out = refs[:ng], refs[ng:2 * ng], refs[2 * ng:3 * ng], refs[3 * ng]

    def token_order(ref, p_ref, d):
        x = jnp.concatenate([ref[0, r] for r in range(d)], axis=0)
        if d == 1:
            return x
        return sum(_dot(p_ref[...], piece) for piece in _split3(x))

    os_ = [token_order(o, p, d) for o, p, d in zip(o_refs, p_refs, dils)]
    ls = [token_order(l, p, d) for l, p, d in zip(l_refs, p_refs, dils)]
    out[...] = _mix_groups(os_, ls).astype(out.dtype)


def _combine_prompt(outs, lses, dils, name):
    b, _, _, ow = outs[0].shape
    t = outs[0].shape[1] * outs[0].shape[2]
    rows = _pick(t, (256, 128, 64, 32, 16))
    assert all(rows % (SROWS * d) == 0 for d in dils)
    nb = t // rows
    specs = [pl.BlockSpec((1, d, rows // d, ow), lambda bi, j: (bi, 0, j, 0)) for d in dils]
    perms = [jnp.asarray(_residue_perm(d, rows).T, BF16) for d in dils]
    pspec = pl.BlockSpec((rows, rows), lambda bi, j: (0, 0))
    return pl.pallas_call(
        functools.partial(_combine_prompt_kernel, dils=tuple(dils)),
        out_shape=jax.ShapeDtypeStruct((b * t, ow), BF16),
        grid=(b, nb),
        in_specs=specs + specs + [pspec] * len(dils),
        out_specs=pl.BlockSpec((rows, ow), lambda bi, j: (bi * nb + j, 0)),
        compiler_params=_cparams(("parallel", "parallel"), 40 << 20),
        name=name,
    )(*outs, *lses, *perms)


def _combine(outs, lses, name):
    n, ow = outs[0].shape
    tr = _pick(n, (512, 256, 128, 64, 8))
    spec = pl.BlockSpec((tr, ow), lambda i: (i, 0))
    return pl.pallas_call(
        _combine_kernel,
        out_shape=jax.ShapeDtypeStruct((n, ow), BF16),
        grid=(n // tr,),
        in_specs=[spec] * (2 * len(outs)),
        out_specs=spec,
        compiler_params=_cparams(("parallel",)),
        name=name,
    )(*outs, *lses)


def _merge_kernel(oa_ref, yb_ref, oc_ref, wa_ref, wb_ref, wc_ref, ga_ref, gb_ref, gc_ref, o_ref):
    acc = _sigmoid(ga_ref[...]) * _dot(oa_ref[...], wa_ref[...])
    acc = acc + _sigmoid(gb_ref[...]) * _dot(yb_ref[...], wb_ref[...])
    acc = acc + _sigmoid(gc_ref[...]) * _dot(oc_ref[...], wc_ref[...])
    o_ref[...] = acc.astype(o_ref.dtype)


def _merge(oa, yb, oc, wa, wb, wc, h, off_g, d, name):
    n = oa.shape[0]
    tm = _pick(n, (384, 256, 128, 88, 64, 8))
    tn = _pick(d, (512, 256, 128), off_g)
    ka, kb, kc = oa.shape[1], yb.shape[1], oc.shape[1]

    def gspec(k):
        return pl.BlockSpec((tm, tn), lambda j, i: (i, (off_g + k * d) // tn + j))

    return pl.pallas_call(
        _merge_kernel,
        out_shape=jax.ShapeDtypeStruct((n, d), BF16),
        grid=(d // tn, n // tm),
        in_specs=[pl.BlockSpec((tm, ka), lambda j, i: (i, 0)),
                  pl.BlockSpec((tm, kb), lambda j, i: (i, 0)),
                  pl.BlockSpec((tm, kc), lambda j, i: (i, 0)),
                  pl.BlockSpec((ka, tn), lambda j, i: (0, j)),
                  pl.BlockSpec((kb, tn), lambda j, i: (0, j)),
                  pl.BlockSpec((kc, tn), lambda j, i: (0, j)),
                  gspec(0), gspec(1), gspec(2)],
        out_specs=pl.BlockSpec((tm, tn), lambda j, i: (i, j)),
        compiler_params=_cparams(("parallel", "parallel"), 48 << 20),
        name=name,
    )(oa, yb, oc, wa, wb, wc, h, h, h)


def _layer_norm(y, g, b):
    mu = jnp.mean(y, axis=-1, keepdims=True)
    yc = y - mu
    var = jnp.mean(yc * yc, axis=-1, keepdims=True)
    return yc * lax.rsqrt(var + EPS) * g + b


def _outln_kernel(m_ref, w_ref, x_ref, g_ref, b_ref, o_ref, ob_ref, *, alpha):
    y = alpha * x_ref[...] + _dot(m_ref[...], w_ref[...])
    out = _layer_norm(y, g_ref[...], b_ref[...])
    o_ref[...] = out
    ob_ref[...] = out.astype(ob_ref.dtype)


def _outln(merged, w_out, x, g, b, alpha, name):
    n, d = x.shape
    tm = _pick(n, (384, 256, 128, 88, 64, 8))
    rspec = pl.BlockSpec((tm, d), lambda i: (i, 0))
    vspec = pl.BlockSpec((1, d), lambda i: (0, 0))
    return pl.pallas_call(
        functools.partial(_outln_kernel, alpha=alpha),
        out_shape=(jax.ShapeDtypeStruct((n, d), F32), jax.ShapeDtypeStruct((n, d), BF16)),
        grid=(n // tm,),
        in_specs=[rspec, pl.BlockSpec((d, d), lambda i: (0, 0)), rspec, vspec, vspec],
        out_specs=(rspec, rspec),
        compiler_params=_cparams(("parallel",), 48 << 20),
        name=name,
    )(merged, w_out, x, g, b)


def _resln_kernel(x_ref, f_ref, g_ref, b_ref, o_ref, ob_ref, *, alpha):
    out = _layer_norm(alpha * x_ref[...] + f_ref[...], g_ref[...], b_ref[...])
    o_ref[...] = out
    ob_ref[...] = out.astype(ob_ref.dtype)


def _resln(x, f, g, b, alpha, name):
    n, d = x.shape
    tm = _pick(n, (384, 256, 128, 88, 64, 8))
    rspec = pl.BlockSpec((tm, d), lambda i: (i, 0))
    vspec = pl.BlockSpec((1, d), lambda i: (0, 0))
    return pl.pallas_call(
        functools.partial(_resln_kernel, alpha=alpha),
        out_shape=(jax.ShapeDtypeStruct((n, d), F32), jax.ShapeDtypeStruct((n, d), BF16)),
        grid=(n // tm,),
        in_specs=[rspec, rspec, vspec, vspec],
        out_specs=(rspec, rspec),
        compiler_params=_cparams(("parallel",)),
        name=name,
    )(x, f, g, b)


def _ffn_a_kernel(te_ref, x_ref, w1_ref, w3_ref, o_ref):
    used = pl.program_id(1) < te_ref[pl.num_programs(1)]

    @pl.when(used)
    def _():
        x = x_ref[...].astype(BF16)
        a = _dot(x, w1_ref[0])
        o_ref[...] = (_silu(a) * _dot(x, w3_ref[0])).astype(o_ref.dtype)

    @pl.when(jnp.logical_not(used))
    def _():
        o_ref[...] = jnp.zeros_like(o_ref)


def _ffn_a(x, w1, w3, tile_expert, tm, name):
    n, d = x.shape
    f = w1.shape[2]
    tf = _pick(f, (1408, 1024, 512, 256, 128))
    xbytes = x.dtype.itemsize
    vm = 2 * (tm * d * xbytes + 2 * d * tf * 2 + tm * tf * 2) + 3 * tm * tf * 4 + (6 << 20)
    return pl.pallas_call(
        _ffn_a_kernel,
        out_shape=jax.ShapeDtypeStruct((n, f), BF16),
        grid_spec=pltpu.PrefetchScalarGridSpec(
            num_scalar_prefetch=1,
            grid=(f // tf, n // tm),
            in_specs=[pl.BlockSpec((tm, d), lambda j, i, te: (i, 0)),
                      pl.BlockSpec((1, d, tf), lambda j, i, te: (te[i], 0, j)),
                      pl.BlockSpec((1, d, tf), lambda j, i, te: (te[i], 0, j))],
            out_specs=pl.BlockSpec((tm, tf), lambda j, i, te: (i, j))),
        compiler_params=_cparams(("parallel", "arbitrary"), vm),
        name=name,
    )(tile_expert, x, w1, w3)


def _ffn_b_kernel(te_ref, h_ref, w2_ref, o_ref):
    used = pl.program_id(1) < te_ref[pl.num_programs(1)]

    @pl.when(used)
    def _():
        o_ref[...] = _dot(h_ref[...], w2_ref[0])

    @pl.when(jnp.logical_not(used))
    def _():
        o_ref[...] = jnp.zeros_like(o_ref)


def _ffn_b(hmid, w2, tile_expert, tm, name):
    n, f = hmid.shape
    d = w2.shape[2]
    tn = _pick(d, (1024, 512, 256, 128))
    vm = 2 * (tm * f * 2 + f * tn * 2 + tm * tn * 4) + (4 << 20)
    return pl.pallas_call(
        _ffn_b_kernel,
        out_shape=jax.ShapeDtypeStruct((n, d), F32),
        grid_spec=pltpu.PrefetchScalarGridSpec(
            num_scalar_prefetch=1,
            grid=(d // tn, n // tm),
            in_specs=[pl.BlockSpec((tm, f), lambda j, i, te: (i, 0)),
                      pl.BlockSpec((1, f, tn), lambda j, i, te: (te[i], 0, j))],
            out_specs=pl.BlockSpec((tm, tn), lambda j, i, te: (i, j))),
        compiler_params=_cparams(("parallel", "arbitrary"), vm),
        name=name,
    )(tile_expert, hmid, w2)


def _router_kernel(x_ref, w_ref, b_ref, e_ref, g_ref, *, n_exp):
    xs = _split3(x_ref[...])
    ws = _split3(w_ref[...])
    logits = b_ref[...]
    for i in range(3):
        for j in range(3 - i):
            logits = logits + _dot(xs[i], ws[j])
    lane = lax.broadcasted_iota(jnp.int32, logits.shape, 1)
    neg = jnp.float32(-jnp.inf)
    logits = jnp.where(lane < n_exp, logits, neg)
    m0 = jnp.max(logits, axis=-1, keepdims=True)
    i0 = jnp.min(jnp.where(logits == m0, lane, LANE), axis=-1, keepdims=True)
    rest = jnp.where(lane == i0, neg, logits)
    m1 = jnp.max(rest, axis=-1, keepdims=True)
    i1 = jnp.min(jnp.where(rest == m1, lane, LANE), axis=-1, keepdims=True)
    e1 = jnp.exp(m1 - m0)
    den = 1.0 + e1
    e_ref[...] = jnp.where(lane == 0, i0, jnp.where(lane == 1, i1, 0))
    g_ref[...] = jnp.where(lane == 0, 1.0 / den, jnp.where(lane == 1, e1 / den, 0.0))


def _router(x, w_router, b_router, name):
    n, d = x.shape
    n_exp = w_router.shape[1]
    wp = jnp.zeros((d, LANE), F32).at[:, :n_exp].set(w_router.astype(F32))
    bp = jnp.zeros((1, LANE), F32).at[0, :n_exp].set(b_router.astype(F32))
    tm = _pick(n, (384, 256, 128, 88, 64, 8))
    rspec = pl.BlockSpec((tm, LANE), lambda i: (i, 0))
    return pl.pallas_call(
        functools.partial(_router_kernel, n_exp=n_exp),
        out_shape=(jax.ShapeDtypeStruct((n, LANE), jnp.int32), jax.ShapeDtypeStruct((n, LANE), F32)),
        grid=(n // tm,),
        in_specs=[pl.BlockSpec((tm, d), lambda i: (i, 0)),
                  pl.BlockSpec((d, LANE), lambda i: (0, 0)),
                  pl.BlockSpec((1, LANE), lambda i: (0, 0))],
        out_specs=(rspec, rspec),
        compiler_params=_cparams(("parallel",)),
        name=name,
    )(x, wp, bp)


def _row_copy(src_hbm, row, dst_ref, dst_row, sem):
    return pltpu.make_async_copy(src_hbm.at[pl.ds(row, 1)], dst_ref.at[pl.ds(dst_row, 1)], sem)


def _gather_kernel(idx_ref, x_hbm, o_ref, sem, *, tm):
    base = pl.program_id(0) * tm

    def start(r, carry):
        _row_copy(x_hbm, idx_ref[base + r], o_ref, r, sem).start()
        return carry

    def wait(r, carry):
        _row_copy(x_hbm, 0, o_ref, r, sem).wait()
        return carry

    lax.fori_loop(0, tm, start, 0, unroll=8)
    lax.fori_loop(0, tm, wait, 0)


def _gather_rows(x, idx, tm, name):
    n_out = idx.shape[0]
    d = x.shape[1]
    return pl.pallas_call(
        functools.partial(_gather_kernel, tm=tm),
        out_shape=jax.ShapeDtypeStruct((n_out, d), x.dtype),
        grid_spec=pltpu.PrefetchScalarGridSpec(
            num_scalar_prefetch=1,
            grid=(n_out // tm,),
            in_specs=[pl.BlockSpec(memory_space=pl.ANY)],
            out_specs=pl.BlockSpec((tm, d), lambda i, idx_ref: (i, 0)),
            scratch_shapes=[pltpu.SemaphoreType.DMA(())]),
        compiler_params=_cparams(("arbitrary",)),
        name=name,
    )(idx, x)


def _moe_out_kernel(pos_ref, y_hbm, x_ref, gate_ref, g_ref, b_ref, o_ref, ob_ref, buf, sem, *, tm, alpha):
    base = pl.program_id(0) * tm

    def start(r, carry):
        for k in range(2):
            _row_copy(y_hbm, pos_ref[2 * (base + r) + k], buf.at[k], r, sem).start()
        return carry

    def wait(r, carry):
        for k in range(2):
            _row_copy(y_hbm, 0, buf.at[k], r, sem).wait()
        return carry

    lax.fori_loop(0, tm, start, 0, unroll=8)
    lax.fori_loop(0, tm, wait, 0)
    gt = gate_ref[...]
    f = buf[0] * gt[:, 0:1] + buf[1] * gt[:, 1:2]
    out = _layer_norm(alpha * x_ref[...] + f, g_ref[...], b_ref[...])
    o_ref[...] = out
    ob_ref[...] = out.astype(ob_ref.dtype)


def _moe_out(ys, pos, x, gates, g, b, alpha, name):
    n, d = x.shape
    tm = _pick(n, (256, 128, 88, 64, 8))
    rspec = pl.BlockSpec((tm, d), lambda i, p: (i, 0))
    vspec = pl.BlockSpec((1, d), lambda i, p: (0, 0))
    return pl.pallas_call(
        functools.partial(_moe_out_kernel, tm=tm, alpha=alpha),
        out_shape=(jax.ShapeDtypeStruct((n, d), F32), jax.ShapeDtypeStruct((n, d), BF16)),
        grid_spec=pltpu.PrefetchScalarGridSpec(
            num_scalar_prefetch=1,
            grid=(n // tm,),
            in_specs=[pl.BlockSpec(memory_space=pl.ANY), rspec,
                      pl.BlockSpec((tm, LANE), lambda i, p: (i, 0)), vspec, vspec],
            out_specs=(rspec, rspec),
            scratch_shapes=[pltpu.VMEM((2, tm, d), F32), pltpu.SemaphoreType.DMA(())]),
        compiler_params=_cparams(("arbitrary",)),
        name=name,
    )(pos, ys, x, gates, g, b)


def _moe_ffn(x, w_router, b_router, w1, w3, w2, g, b, alpha, tag):
    n, d = x.shape
    n_exp = w1.shape[0]
    tm = MOE_TILE
    e_pad, gates = _router(x, w_router, b_router, f"router_{tag}")
    e_flat = e_pad[:, :2].reshape(-1)
    n_slots = 2 * n
    onehot = (e_flat[:, None] == jnp.arange(n_exp, dtype=jnp.int32)[None, :]).astype(jnp.int32)
    csum = jnp.cumsum(onehot, axis=0)
    rank = jnp.take_along_axis(csum, e_flat[:, None], axis=1)[:, 0] - 1
    counts = csum[-1]
    padded = (counts + tm - 1) // tm * tm
    pad_end = jnp.cumsum(padded)
    pos = (pad_end - padded)[e_flat] + rank
    n_tiles = -(-(n_slots + n_exp * (tm - 1)) // tm)
    slot_tok = jnp.zeros((n_tiles * tm,), jnp.int32).at[pos].set(jnp.arange(n_slots, dtype=jnp.int32) // 2)
    tile_expert = jnp.minimum(
        jnp.searchsorted(pad_end, jnp.arange(n_tiles, dtype=jnp.int32) * tm, side="right"), n_exp - 1
    ).astype(jnp.int32)
    tile_expert = jnp.concatenate([tile_expert, (pad_end[-1:] // tm).astype(jnp.int32)])
    xs = _gather_rows(x, slot_tok, tm, f"moe_gather_{tag}")
    hmid = _ffn_a(xs, w1, w3, tile_expert, tm, f"moe_a_{tag}")
    ys = _ffn_b(hmid, w2, tile_expert, tm, f"moe_b_{tag}")
    return _moe_out(ys, pos.astype(jnp.int32), x, gates, g, b, alpha, f"moe_out_{tag}")


def kernel(x_prompt, x_sample, state_hgrn, state_ssm, state_conv, cache_k0, cache_v0, cache_k1, cache_v1,
           cache_k2, cache_v2, w_in, hg_lb, hg_norm_w, conv_w, conv_b, dt_bias, a_log, d_skip, ssm_norm_w,
           w_proj_a, w_proj_b, w_proj_c, w_out, ln1_g, ln1_b, ln2_g, ln2_b, ffn_w1, ffn_w3, ffn_w2,
           moe_router, moe_router_b, moe_w1, moe_w3, moe_w2):
    caches = ((cache_k0, cache_v0), (cache_k1, cache_v1), (cache_k2, cache_v2))
    depth = w_in.shape[0]
    b, t, d = x_prompt.shape
    bs, ts, _ = x_sample.shape
    _, _, hg_heads, hg_dk, hg_dv = state_hgrn.shape
    _, _, m_heads, pdim, dstate = state_ssm.shape
    conv_dim = state_conv.shape[-1]
    hpg, adh = cache_k0.shape[-2:]
    assert hg_dk == LANE and hg_dv == LANE and dstate == LANE and adh == LANE and ts <= SROWS
    hg_dim = hg_heads * LANE
    dinner = m_heads * pdim
    groups = (conv_dim - dinner) // (2 * dstate)
    m_hpg = m_heads // groups
    gw = m_hpg * pdim
    adim = len(A_GROUPS) * hpg * LANE
    assert gw % LANE == 0 and LANE % pdim == 0 and t % HG_CHUNK == 0 and t % SSD_CHUNK == 0
    ow = hpg * LANE
    alpha = (2 * depth) ** 0.25
    bt = b * t
    n = bt + bs * SROWS

    sizes = (("hq", hg_dim), ("hf", hg_dim), ("hi", hg_dim), ("hg", hg_dim), ("mz", dinner), ("xbc", conv_dim),
             ("mdt", m_heads), ("aq", adim), ("ak", adim), ("av", adim), ("gate", 3 * d))
    src, acc = {}, 0
    for name, sz in sizes:
        src[name] = acc
        acc += sz
    assert acc == w_in.shape[2]
    n_a = src["mdt"]
    n_b = 3 * adim + 3 * d
    offs = {k: src[k] for k in ("hq", "hf", "hi", "hg", "mz", "xbc")}
    offs.update({k: src[k] - src["aq"] for k in ("aq", "ak", "av", "gate")})
    assert n_a % LANE == 0 and m_heads < LANE and n_b % LANE == 0 and n_a + LANE <= w_in.shape[2]
    assert offs["mz"] % gw == 0 and all(offs[k] % ow == 0 for k in ("aq", "ak", "av"))

    xs_pad = jnp.zeros((bs, SROWS, d), F32).at[:, :ts].set(x_sample)
    x = jnp.concatenate([x_prompt.reshape(bt, d), xs_pad.reshape(bs * SROWS, d)], axis=0)
    xb = x.astype(BF16)

    state_ssm_r = state_ssm.reshape(depth, bs, m_heads * pdim, dstate)
    w_in_t = jnp.swapaxes(w_in, 1, 2)
    lb_sm = jax.nn.softmax(hg_lb.astype(F32), axis=0)
    lb_all = jnp.cumsum(lb_sm, axis=0) - lb_sm[0]

    out_hg_p, out_ssm_p, out_conv_p, out_hg_s, out_ssm_s, out_conv_s = [], [], [], [], [], []
    out_kv_p = [[] for _ in range(2 * len(A_GROUPS))]
    out_kv_s = [[] for _ in range(2 * len(A_GROUPS))]

    for l in range(depth):
        h = _matmul_wcast(xb, w_in_t, l, 0, n_a, 0, F32, f"in_proj_a_{l}")
        h_b = _matmul_wcast(xb, w_in_t, l, n_a, n_b, m_heads, F32, f"in_proj_b_{l}")
        mdt = _matmul_wcast(xb, w_in_t, l, n_a, LANE, 0, F32, f"in_proj_dt_{l}")[:, :m_heads]

        lb = lb_all[l]
        lbc = jnp.stack([jnp.log(jnp.maximum(lb, LB_FLOOR)), jnp.log1p(-lb), 1.0 - lb])
        nw_a = hg_norm_w[l][None, :]
        oa_p, hg_p = _hgrn_prompt(h, offs, lbc, nw_a, b, t, hg_heads, f"hgrn_prompt_{l}")
        oa_s, hg_s = _hgrn_sample(h, offs, lbc, nw_a, state_hgrn, l, bt, bs, ts, hg_heads, f"hgrn_sample_{l}")
        o_a = jnp.concatenate([oa_p, oa_s], axis=0)

        cw, cbias = conv_w[l], conv_b[l][None, :]
        prev_s = jnp.zeros((bs, SROWS, conv_dim), F32).at[:, SROWS - (CONV_WIDTH - 1):].set(state_conv[l])
        xa_s = _conv_sample(h, offs["xbc"], conv_dim, cw, cbias, bt, prev_s.reshape(bs * SROWS, conv_dim),
                            f"conv_sample_{l}")
        a_neg = -jnp.exp(a_log[l].astype(F32))
        prm = jnp.stack([dt_bias[l].astype(F32), a_neg, d_skip[l].astype(F32)])
        prm_r = prm.reshape(3, groups, m_hpg).transpose(1, 0, 2)
        mdt_g = mdt[:bt].reshape(bt, groups, m_hpg).transpose(1, 0, 2)
        nw_b = ssm_norm_w[l][None, :]
        yb_p, ssm_p = _ssd_prompt(h, offs["xbc"], offs["mz"], cw, cbias, mdt_g, mdt_g.transpose(0, 2, 1), prm_r,
                                  prm_r.transpose(0, 2, 1), nw_b, b, t, groups, m_hpg, pdim, dstate,
                                  f"ssd_prompt_{l}")
        yb_s, ssm_s = _ssd_sample(xa_s, h, offs["mz"], jnp.repeat(mdt[bt:], pdim, axis=1),
                                  jnp.repeat(prm, pdim, axis=1), nw_b,
                                  state_ssm_r, l, bt, bs, ts, groups, gw, dstate, f"ssd_sample_{l}")
        y_b = jnp.concatenate([yb_p, yb_s], axis=0)

        q_s, k_s = _rope_sample(h_b, offs, adim, bt, bs * SROWS, ts, f"rope_sample_{l}")
        v_s = h_b[bt:, offs["av"]:offs["av"] + adim]
        outs_p, lses_p, outs_s, lses_s, k_std = [], [], [], [], []
        for gi, (window, dil) in enumerate(A_GROUPS):
            qd, kd, vd, ks = _rope_prompt(h_b, offs, gi, dil, b, t, hpg, f"rope_prompt_{l}_{gi}")
            k_std.append(ks)
            o_g, l_g = _attn_prompt(qd, kd, vd, window, dil, f"attn_prompt_{l}_{gi}")
            outs_p.append(o_g)
            lses_p.append(l_g)
            g3 = lambda a: a[:, gi * ow:(gi + 1) * ow].reshape(bs * SROWS, hpg, LANE)
            o_g, l_g = _attn_sample(g3(q_s), g3(k_s), g3(v_s), caches[gi][0], caches[gi][1], l, window, dil,
                                    bs, ts, f"attn_sample_{l}_{gi}")
            outs_s.append(o_g.reshape(bs * SROWS, ow))
            lses_s.append(l_g.reshape(bs * SROWS, ow))
        o_c = jnp.concatenate([_combine_prompt(outs_p, lses_p, [dl for _, dl in A_GROUPS], f"attn_mix_prompt_{l}"),
                               _combine(outs_s, lses_s, f"attn_mix_sample_{l}")], axis=0)

        merged = _merge(o_a, y_b, o_c, w_proj_a[l].astype(BF16), w_proj_b[l].astype(BF16),
                        w_proj_c[l].astype(BF16), h_b, offs["gate"], d, f"merge_{l}")
        x, xb = _outln(merged, w_out[l].astype(BF16), x, ln1_g[l][None, :], ln1_b[l][None, :], alpha, f"out_ln1_{l}")

        if l % 2 == 0:
            i = l // 2
            tm = _pick(n, (768, 512, 384, 256, 128, 88, 64, 8))
            te = jnp.zeros((n // tm + 1,), jnp.int32).at[-1].set(n // tm)
            hmid = _ffn_a(xb, ffn_w1[i:i + 1].astype(BF16), ffn_w3[i:i + 1].astype(BF16), te, tm, f"ffn_a_{l}")
            f = _ffn_b(hmid, ffn_w2[i:i + 1].astype(BF16), te, tm, f"ffn_b_{l}")
            x, xb = _resln(x, f, ln2_g[l][None, :], ln2_b[l][None, :], alpha, f"ln2_{l}")
        else:
            i = l // 2
            x, xb = _moe_ffn(x, moe_router[i], moe_router_b[i], moe_w1[i].astype(BF16), moe_w3[i].astype(BF16),
                             moe_w2[i].astype(BF16), ln2_g[l][None, :], ln2_b[l][None, :], alpha, str(l))

        keep_c = CONV_WIDTH - 1
        assert t >= keep_c

        def tail_rows(arr, c0, c1, keep):
            return jnp.stack([arr[(bi + 1) * t - keep:(bi + 1) * t, c0:c1] for bi in range(b)])

        out_conv_p.append(tail_rows(h, offs["xbc"], offs["xbc"] + conv_dim, keep_c))
        xbc_s = h[bt:, offs["xbc"]:offs["xbc"] + conv_dim].reshape(bs, SROWS, conv_dim)[:, :ts]
        out_conv_s.append(jnp.concatenate([state_conv[l], xbc_s], axis=1)[:, -keep_c:])
        out_hg_p.append(hg_p)
        out_hg_s.append(hg_s)
        out_ssm_p.append(ssm_p.reshape(b, m_heads, pdim, dstate))
        out_ssm_s.append(ssm_s.reshape(bs, m_heads, pdim, dstate))
        ksm = k_s.reshape(bs, SROWS, adim // LANE, LANE)[:, :ts]
        vsm = v_s.reshape(bs, SROWS, adim // LANE, LANE)[:, :ts]
        for gi, (window, _) in enumerate(A_GROUPS):
            hs = slice(gi * hpg, (gi + 1) * hpg)
            keep = min(window, t)
            v0 = offs["av"] + gi * ow
            out_kv_p[2 * gi].append(k_std[gi].reshape(b, t, hpg, LANE)[:, t - keep:])
            out_kv_p[2 * gi + 1].append(tail_rows(h_b, v0, v0 + ow, keep).reshape(b, keep, hpg, LANE))
            out_kv_s[2 * gi].append(ksm[:, :, hs])
            out_kv_s[2 * gi + 1].append(vsm[:, :, hs])

    y_prompt = x[:bt].reshape(b, t, d)
    y_sample = x[bt:].reshape(bs, SROWS, d)[:, :ts]
    return (y_prompt, y_sample, jnp.stack(out_hg_p), jnp.stack(out_ssm_p), jnp.stack(out_conv_p),
            *[jnp.stack(o) for o in out_kv_p],
            jnp.stack(out_hg_s), jnp.stack(out_ssm_s), jnp.stack(out_conv_s),
            *[jnp.stack(o) for o in out_kv_s])
```
